```python
import math
import jax, jax.numpy as jnp
from jax import lax
import numpy as np

D_MODEL = 1024
BATCH = 8
SEQ = 8192
DEPTH = 1

EPS = 1e-6
NEG_INF = -1e30

POOL_WINDOWS = (2, 4, 8, 16)
POOL_GROUPS = len(POOL_WINDOWS)
POOL_IN = 128
POOL_WIDTH = POOL_GROUPS * POOL_IN
POOL_OUT = D_MODEL // POOL_GROUPS

ATT_HEADS = 16
HEAD_DIM = D_MODEL // ATT_HEADS
DILATED_PATTERNS = ((128, 1), (512, 4), (2048, 16))
N_PATTERNS = len(DILATED_PATTERNS)
Q_WIDTH = N_PATTERNS * ATT_HEADS * HEAD_DIM
KV_WIDTH = ATT_HEADS * HEAD_DIM

N_BRANCHES = 2
GATE_WIDTH = N_BRANCHES * D_MODEL
IN_WIDTH = POOL_WIDTH + Q_WIDTH + 2 * KV_WIDTH + GATE_WIDTH

PEER_HEADS = 8
PEER_NKEYS = 128
PEER_N = PEER_NKEYS * PEER_NKEYS
PEER_QDIM = 256
PEER_HALF = PEER_QDIM // 2
PEER_TOPK = 16
PEER_CHUNK = 128

kernel_name = "hybrid_pool_dilattn_peer_block"


def rms_norm(x, g):
    xf = x.astype(jnp.float32)
    y = xf * lax.rsqrt(jnp.mean(xf * xf, axis=-1, keepdims=True) + EPS)
    return (y * g.astype(jnp.float32)).astype(x.dtype)


def alibi_slopes(n):
    def geometric(k):
        start = 2.0 ** (-8.0 / k)
        return [start ** (i + 1) for i in range(k)]
    p = 2 ** int(math.floor(math.log2(n)))
    s = geometric(p) + geometric(2 * p)[0::2][: n - p]
    return np.sort(np.array(s, dtype=np.float32))[::-1].copy()


def pool_mixer(a, w_pool, pool_scale):
    B, S = a.shape[0], a.shape[1]
    af = a.astype(jnp.float32)
    cs = jnp.pad(jnp.cumsum(af, axis=1), ((0, 0), (1, 0), (0, 0), (0, 0)))
    t = jnp.arange(S)[:, None]
    win = jnp.array(POOL_WINDOWS, dtype=jnp.int32)[None, :]
    lo = jnp.maximum(t + 1 - win, 0)
    grp = jnp.arange(POOL_GROUPS)[None, :]
    window_sum = cs[:, 1:] - cs[:, lo, grp, :]
    count = jnp.minimum(t + 1, win).astype(jnp.float32)[None, :, :, None]
    d = (window_sum / count - af).astype(a.dtype)
    y = jnp.einsum('bsgc,gco->bsgo', d, w_pool)
    return y.reshape(B, S, D_MODEL) * pool_scale


def dilated_group(q, k, v, window, dilation, slopes):
    B, S, H, Dh = q.shape
    band = window // dilation
    n = -(-S // (dilation * band)) * band
    L = n * dilation
    nb = n // band

    def blocks(x):
        x = jnp.pad(x, ((0, 0), (0, L - S), (0, 0), (0, 0)))
        return x.reshape(B, nb, band, dilation, H, Dh)

    def with_prev(xb):
        prev = jnp.pad(xb, ((0, 0), (1, 0), (0, 0), (0, 0), (0, 0), (0, 0)))[:, :nb]
        return jnp.concatenate([prev, xb], axis=2)

    qb = blocks(q)
    kb = with_prev(blocks(k))
    vb = with_prev(blocks(v))
    s = jnp.einsum('bnqrhe,bnkrhe->bnrhqk', qb, kb).astype(jnp.float32)
    qi = jnp.arange(band)[:, None]
    ki = jnp.arange(2 * band)[None, :]
    step = qi + band - ki
    key_idx = jnp.arange(nb)[:, None, None] * band + ki[None] - band
    valid = (step >= 0)[None] & (step <= band)[None] & (key_idx >= 0)
    alibi = -slopes[:, None, None] * (step * dilation).astype(jnp.float32)[None]
    s = jnp.where(valid[None, :, None, None], s + alibi[None, None, None], NEG_INF)
    m = jnp.max(s, axis=-1, keepdims=True)
    p = jnp.exp(s - m)
    den = jnp.sum(p, axis=-1)
    o = jnp.einsum('bnrhqk,bnkrhe->bnrhqe', p, vb.astype(jnp.float32)) / den[..., None]
    o = o.transpose(0, 1, 4, 2, 3, 5).reshape(B, L, H, Dh)[:, :S]
    m = m[..., 0].transpose(0, 1, 4, 2, 3).reshape(B, L, H)[:, :S]
    den = den.transpose(0, 1, 4, 2, 3).reshape(B, L, H)[:, :S]
    return o, m, den


def dilated_attention(q, k, v, slopes):
    outs, maxs, dens = [], [], []
    for g, (window, dilation) in enumerate(DILATED_PATTERNS):
        o, m, den = dilated_group(q[:, :, g], k, v, window, dilation, slopes[g])
        outs.append(o)
        maxs.append(m)
        dens.append(den)
    o = jnp.stack(outs)
    m = jnp.stack(maxs)
    den = jnp.stack(dens)
    w = den * jnp.exp(m - jnp.max(m, axis=0, keepdims=True))
    w = w / jnp.sum(w, axis=0, keepdims=True)
    return jnp.sum(w[..., None] * o, axis=0)


def peer(hn, w_query, sub_keys, expert_u, expert_v):
    B, S, D = hn.shape
    tokens = hn.reshape(-1, PEER_CHUNK, D)

    def chunk_fn(xc):
        C = xc.shape[0]
        q = (xc @ w_query).reshape(C, PEER_HEADS, 2, PEER_HALF)
        sc = jnp.einsum('chpe,hpne->chpn', q, sub_keys).astype(jnp.float32)
        top_s, top_i = lax.top_k(sc, PEER_TOPK)
        cand_s = top_s[:, :, 0, :, None] + top_s[:, :, 1, None, :]
        cand_i = top_i[:, :, 0, :, None] * PEER_NKEYS + top_i[:, :, 1, None, :]
        best_s, best_j = lax.top_k(cand_s.reshape(C, PEER_HEADS, -1), PEER_TOPK)
        experts = jnp.take_along_axis(cand_i.reshape(C, PEER_HEADS, -1), best_j, axis=-1)
        gate = jax.nn.softmax(best_s, axis=-1)
        u = expert_u[experts]
        v = expert_v[experts]
        act = jax.nn.gelu(jnp.einsum('cd,chkd->chk', xc, u).astype(jnp.float32))
        return jnp.einsum('chk,chkd->cd', (gate * act).astype(xc.dtype), v)

    return lax.map(chunk_fn, tokens).reshape(B, S, D)


def setup_inputs(seed: int = 0) -> dict:
    key = jax.random.key(seed)
    ks = jax.random.split(key, 13)
    nrm = jax.random.normal
    return {
        "x": nrm(ks[0], (BATCH, SEQ, D_MODEL), jnp.float32),
        "norm1_g": 1.0 + 0.05 * nrm(ks[1], (DEPTH, D_MODEL), jnp.float32),
        "w_in": nrm(ks[2], (DEPTH, D_MODEL, IN_WIDTH), jnp.float32) * D_MODEL ** -0.5,
        "q_norm_g": 1.0 + 0.05 * nrm(ks[3], (DEPTH, HEAD_DIM), jnp.float32),
        "k_norm_g": 1.0 + 0.05 * nrm(ks[4], (DEPTH, HEAD_DIM), jnp.float32),
        "w_pool": nrm(ks[5], (DEPTH, POOL_GROUPS, POOL_IN, POOL_OUT), jnp.float32) * POOL_IN ** -0.5,
        "pool_scale": 1.0 + 0.1 * nrm(ks[6], (DEPTH, D_MODEL), jnp.float32),
        "w_out": nrm(ks[7], (DEPTH, D_MODEL, D_MODEL), jnp.float32) * D_MODEL ** -0.5,
        "norm2_g": 1.0 + 0.05 * nrm(ks[8], (DEPTH, D_MODEL), jnp.float32),
        "w_query": nrm(ks[9], (DEPTH, D_MODEL, PEER_HEADS * PEER_QDIM), jnp.float32) * D_MODEL ** -0.5,
        "sub_keys": nrm(ks[10], (DEPTH, PEER_HEADS, 2, PEER_NKEYS, PEER_HALF), jnp.float32) * PEER_HALF ** -0.5,
        "expert_u": nrm(ks[11], (DEPTH, PEER_N, D_MODEL), jnp.float32) * D_MODEL ** -0.5,
        "expert_v": nrm(ks[12], (DEPTH, PEER_N, D_MODEL), jnp.float32) * PEER_HEADS ** -0.5,
    }


def reference(x, norm1_g, w_in, q_norm_g, k_norm_g, w_pool, pool_scale, w_out, norm2_g, w_query, sub_keys, expert_u, expert_v):
    B, S, D = x.shape
    slopes = jnp.asarray(alibi_slopes(N_PATTERNS * ATT_HEADS)).reshape(N_PATTERNS, ATT_HEADS)
    splits = [POOL_WIDTH, POOL_WIDTH + Q_WIDTH, POOL_WIDTH + Q_WIDTH + KV_WIDTH,
              POOL_WIDTH + Q_WIDTH + 2 * KV_WIDTH]
    h = x
    for l in range(DEPTH):
        xn = rms_norm(h, norm1_g[l])
        proj = xn @ w_in[l]
        a_in, q, k, v, gate_pre = jnp.split(proj, splits, axis=-1)
        pool_out = pool_mixer(a_in.reshape(B, S, POOL_GROUPS, POOL_IN), w_pool[l], pool_scale[l])
        q = rms_norm(q.reshape(B, S, N_PATTERNS, ATT_HEADS, HEAD_DIM), q_norm_g[l]) * (HEAD_DIM ** -0.5)
        k = rms_norm(k.reshape(B, S, ATT_HEADS, HEAD_DIM), k_norm_g[l])
        v = v.reshape(B, S, ATT_HEADS, HEAD_DIM)
        attn_out = dilated_attention(q, k, v, slopes).astype(h.dtype).reshape(B, S, D)
        gates = jax.nn.sigmoid(gate_pre.astype(jnp.float32)).astype(h.dtype).reshape(B, S, N_BRANCHES, D)
        merged = gates[:, :, 0] * pool_out + gates[:, :, 1] * attn_out
        h = h + merged @ w_out[l]
        hn = rms_norm(h, norm2_g[l])
        h = h + peer(hn, w_query[l], sub_keys[l], expert_u[l], expert_v[l])
    return h
```

```python
import functools
import math

import numpy as np
import jax
import jax.numpy as jnp
from jax import lax
from jax.experimental import pallas as pl
from jax.experimental.pallas import tpu as pltpu

D_MODEL = 1024
EPS = 1e-6
NEG_INF = -1e30

POOL_WINDOWS = (2, 4, 8, 16)
POOL_GROUPS = 4
POOL_IN = 128
POOL_WIDTH = 512
POOL_OUT = 256
POOL_HALO = 16

ATT_HEADS = 16
HEAD_DIM = 64
DILATED_PATTERNS = ((128, 1), (512, 4), (2048, 16))
N_PATTERNS = 3
BAND = 128
KV_WIDTH = 1024
Q_OFF = POOL_WIDTH
K_OFF = Q_OFF + N_PATTERNS * KV_WIDTH
V_OFF = K_OFF + KV_WIDTH
GATE_OFF = V_OFF + KV_WIDTH
IN_WIDTH = GATE_OFF + 2 * D_MODEL

PEER_HEADS = 8
PEER_NKEYS = 128
PEER_N = PEER_NKEYS * PEER_NKEYS
PEER_HALF = 128
PEER_TOPK = 16
PEER_GROUPS = 2 * PEER_HEADS
PEER_SEL = PEER_HEADS * PEER_TOPK

LANES = 128
SUBLANES = 8
VMEM_LIMIT = 56 * 1024 * 1024

BF16 = jnp.bfloat16
F32 = jnp.float32


def _dot(a, b):
    return jnp.dot(a, b, preferred_element_type=F32)


def _dot_nt(a, b):
    return lax.dot_general(a, b, (((1,), (1,)), ((), ())), preferred_element_type=F32)


def _params(semantics):
    return pltpu.CompilerParams(dimension_semantics=semantics, vmem_limit_bytes=VMEM_LIMIT)


def _const_spec(shape):
    n = len(shape)
    return pl.BlockSpec(shape, lambda *_: (0,) * n)


def _proj_kernel(x_ref, g1_ref, w_ref, bd_ref, qg_ref, kg_ref,
                 a_ref, q0_ref, q1_ref, q2_ref, k_ref, v_ref, gate_ref):
    x = x_ref[...]
    inv = lax.rsqrt(jnp.mean(x * x, axis=-1, keepdims=True) + EPS)
    xn = (x * inv * g1_ref[...]).astype(BF16)

    def head_norm(y, gain):
        ss = _dot((y * y).astype(BF16), bd_ref[...])
        return (y * lax.rsqrt(ss * (1.0 / HEAD_DIM) + EPS) * gain).astype(BF16)

    a_ref[...] = _dot(xn, w_ref[:, 0:POOL_WIDTH])
    for g, q_ref in enumerate((q0_ref, q1_ref, q2_ref)):
        lo = Q_OFF + g * KV_WIDTH
        q_ref[...] = head_norm(_dot(xn, w_ref[:, lo:lo + KV_WIDTH]), qg_ref[...])
    k_ref[...] = head_norm(_dot(xn, w_ref[:, K_OFF:K_OFF + KV_WIDTH]), kg_ref[...])
    v_ref[...] = _dot(xn, w_ref[:, V_OFF:V_OFF + KV_WIDTH]).astype(BF16)
    gate_ref[...] = jax.nn.sigmoid(_dot(xn, w_ref[:, GATE_OFF:IN_WIDTH])).astype(BF16)


def _proj(x2, g1, w_in, bd, qg, kg, tm):
    T = x2.shape[0]
    row = lambda w: pl.BlockSpec((tm, w), lambda i: (i, 0))
    once = lambda shape: pl.BlockSpec(shape, lambda i: (0, 0), pipeline_mode=pl.Buffered(1))
    out_shape = (
        jax.ShapeDtypeStruct((T, POOL_WIDTH), F32),
        jax.ShapeDtypeStruct((T, KV_WIDTH), BF16),
        jax.ShapeDtypeStruct((T, KV_WIDTH), BF16),
        jax.ShapeDtypeStruct((T, KV_WIDTH), BF16),
        jax.ShapeDtypeStruct((T, KV_WIDTH), BF16),
        jax.ShapeDtypeStruct((T, KV_WIDTH), BF16),
        jax.ShapeDtypeStruct((T, 2 * D_MODEL), BF16),
    )
    return pl.pallas_call(
        _proj_kernel,
        grid=(T // tm,),
        in_specs=[row(D_MODEL), once((1, D_MODEL)), once((D_MODEL, IN_WIDTH)),
                  once((KV_WIDTH, KV_WIDTH)), once((1, KV_WIDTH)), once((1, KV_WIDTH))],
        out_specs=(row(POOL_WIDTH), row(KV_WIDTH), row(KV_WIDTH), row(KV_WIDTH),
                   row(KV_WIDTH), row(KV_WIDTH), row(2 * D_MODEL)),
        out_shape=out_shape,
        compiler_params=_params(("arbitrary",)),
        name="proj",
    )(x2, g1, w_in, bd, qg, kg)


def _alibi_slopes(n):
    def geometric(k):
        start = 2.0 ** (-8.0 / k)
        return [start ** (i + 1) for i in range(k)]
    p = 2 ** int(math.floor(math.log2(n)))
    s = geometric(p) + geometric(2 * p)[0::2][: n - p]
    return np.sort(np.array(s, dtype=np.float32))[::-1].copy()


def _attn_bias(slopes, dilation):
    qi = np.arange(BAND)[:, None]
    ki = np.arange(2 * BAND)[None, :]
    step = qi + BAND - ki
    valid = (step >= 0) & (step <= BAND)
    alibi = -slopes[:, None, None] * (step * dilation).astype(np.float32)[None]
    general = np.where(valid[None], alibi, np.float32(NEG_INF)).astype(np.float32)
    first = np.where((ki >= BAND)[None], general, np.float32(NEG_INF)).astype(np.float32)
    return np.stack([general, first])


def _attn_kernel(*refs, has_state, final):
    q_ref, kp_ref, kc_ref, vp_ref, vc_ref, bias_ref = refs[:6]
    refs = refs[6:]
    if has_state:
        acc_in_ref, ml_in_ref = refs[:2]
        refs = refs[2:]
    if final:
        (out_ref,) = refs
    else:
        acc_out_ref, ml_out_ref = refs

    first = (pl.program_id(2) == 0).astype(jnp.int32)
    lane = lax.broadcasted_iota(jnp.int32, (BAND, LANES), 1)
    ml_new = jnp.zeros((BAND, LANES), F32)
    for h in range(ATT_HEADS):
        sl = slice(h * HEAD_DIM, (h + 1) * HEAD_DIM)
        qh = q_ref[0, :, sl]
        sp = _dot_nt(qh, kp_ref[0, :, sl]) + bias_ref[first, h, :, 0:BAND]
        sc = _dot_nt(qh, kc_ref[0, :, sl]) + bias_ref[first, h, :, BAND:2 * BAND]
        m = jnp.maximum(jnp.max(sp, axis=-1, keepdims=True), jnp.max(sc, axis=-1, keepdims=True))
        if has_state:
            m_prev = ml_in_ref[0, :, h:h + 1]
            l_prev = ml_in_ref[0, :, ATT_HEADS + h:ATT_HEADS + h + 1]
            m_new = jnp.maximum(m_prev, m)
            alpha = jnp.exp(m_prev - m_new)
        else:
            m_new = m
        pp = jnp.exp(sp - m_new)
        pc = jnp.exp(sc - m_new)
        l = jnp.sum(pp, axis=-1, keepdims=True) + jnp.sum(pc, axis=-1, keepdims=True)
        acc = _dot(pp.astype(BF16), vp_ref[0, :, sl]) + _dot(pc.astype(BF16), vc_ref[0, :, sl])
        if has_state:
            l = l + alpha * l_prev
            acc = acc + alpha * acc_in_ref[0, :, sl]
        if final:
            out_ref[0, :, sl] = (acc / l).astype(BF16)
        else:
            acc_out_ref[0, :, sl] = acc
            ml_new = jnp.where(lane == h, m_new, jnp.where(lane == ATT_HEADS + h, l, ml_new))
    if not final:
        ml_out_ref[0] = ml_new


def _attn_pattern(q, k, v, bias, state, dilation, final):
    B, S, _ = q.shape
    d = dilation
    rows = S // d
    nb = rows // BAND
    view = lambda a, w: a.reshape(B, rows, d * w)
    cur = lambda w: pl.BlockSpec((1, BAND, w), lambda b, r, n: (b, n, r))
    prev = pl.BlockSpec((1, BAND, KV_WIDTH), lambda b, r, n: (b, jnp.maximum(n - 1, 0), r))
    has_state = state is not None

    args = [view(q, KV_WIDTH), view(k, KV_WIDTH), view(k, KV_WIDTH), view(v, KV_WIDTH), view(v, KV_WIDTH), bias]
    in_specs = [cur(KV_WIDTH), prev, cur(KV_WIDTH), prev, cur(KV_WIDTH),
                pl.BlockSpec(bias.shape, lambda b, r, n: (0, 0, 0, 0))]
    aliases = {}
    if has_state:
        acc, ml = state
        args += [view(acc, KV_WIDTH), view(ml, LANES)]
        in_specs += [cur(KV_WIDTH), cur(LANES)]
        if not final:
            aliases = {6: 0, 7: 1}
    if final:
        out_shape = jax.ShapeDtypeStruct((B, rows, d * KV_WIDTH), BF16)
        out_specs = cur(KV_WIDTH)
    else:
        out_shape = (jax.ShapeDtypeStruct((B, rows, d * KV_WIDTH), F32),
                     jax.ShapeDtypeStruct((B, rows, d * LANES), F32))
        out_specs = (cur(KV_WIDTH), cur(LANES))
    res = pl.pallas_call(
        functools.partial(_attn_kernel, has_state=has_state, final=final),
        grid=(B, d, nb),
        in_specs=in_specs,
        out_specs=out_specs,
        out_shape=out_shape,
        input_output_aliases=aliases,
        compiler_params=_params(("arbitrary", "arbitrary", "arbitrary")),
        name=f"attn_d{d}",
    )(*args)
    if final:
        return res.reshape(B, S, KV_WIDTH)
    acc, ml = res
    return acc.reshape(B, S, KV_WIDTH), ml.reshape(B, S, LANES)


def _mix_kernel(x_ref, a_ref, ah_ref, attn_ref, gate_ref, wp_ref, ps_ref, wo_ref, g2_ref, wq_ref, sk_ref,
                h_ref, hn_ref, sc_ref, ext_ref, merged_ref):
    tm = x_ref.shape[1]
    i = pl.program_id(1)
    ext_ref[0:POOL_HALO, :] = jnp.where(i == 0, 0.0, ah_ref[0])
    ext_ref[POOL_HALO:POOL_HALO + tm, :] = a_ref[0]
    t = i * tm + lax.broadcasted_iota(jnp.int32, (tm, 1), 0)
    for g, w in enumerate(POOL_WINDOWS):
        cols = slice(g * POOL_IN, (g + 1) * POOL_IN)
        wsum = ext_ref[POOL_HALO:POOL_HALO + tm, cols]
        for j in range(1, w):
            wsum = wsum + ext_ref[POOL_HALO - j:POOL_HALO - j + tm, cols]
        count = jnp.minimum(t + 1, w).astype(F32)
        dlt = (wsum / count - a_ref[0, :, cols]).astype(BF16)
        ocol = slice(g * POOL_OUT, (g + 1) * POOL_OUT)
        pool = _dot(dlt, wp_ref[g]) * ps_ref[:, ocol]
        merged = (gate_ref[0, :, ocol].astype(F32) * pool
                  + gate_ref[0, :, D_MODEL + g * POOL_OUT:D_MODEL + (g + 1) * POOL_OUT].astype(F32)
                  * attn_ref[0, :, ocol].astype(F32))
        merged_ref[:, ocol] = merged.astype(BF16)
    h = x_ref[0] + _dot(merged_ref[...], wo_ref[...])
    h_ref[0] = h
    hn = (h * lax.rsqrt(jnp.mean(h * h, axis=-1, keepdims=True) + EPS) * g2_ref[...]).astype(BF16)
    hn_ref[0] = hn
    qt = _dot_nt(wq_ref[...], hn).astype(BF16)
    for g in range(PEER_GROUPS):
        rows = slice(g * PEER_NKEYS, (g + 1) * PEER_NKEYS)
        sc_ref[rows, :] = _dot(sk_ref[g], qt[rows, :])


def _mix(x, a, attn, gates, w_pool, pool_scale, w_out, g2, wq_t, sub_keys, tm):
    B, S, _ = x.shape
    T = B * S
    nt = S // tm
    halo_per_tile = tm // POOL_HALO
    tile = lambda w: pl.BlockSpec((1, tm, w), lambda b, i: (b, i, 0))
    halo = pl.BlockSpec((1, POOL_HALO, POOL_WIDTH), lambda b, i: (b, jnp.maximum(i * halo_per_tile - 1, 0), 0))
    once = lambda shape: pl.BlockSpec(shape, lambda b, i: (0,) * len(shape), pipeline_mode=pl.Buffered(1))
    return pl.pallas_call(
        _mix_kernel,
        grid=(B, nt),
        in_specs=[tile(D_MODEL), tile(POOL_WIDTH), halo, tile(KV_WIDTH), tile(2 * D_MODEL),
                  once(w_pool.shape), once((1, D_MODEL)), once((D_MODEL, D_MODEL)), once((1, D_MODEL)),
                  once(wq_t.shape), once(sub_keys.shape)],
        out_specs=(tile(D_MODEL), tile(D_MODEL),
                   pl.BlockSpec((PEER_GROUPS * PEER_NKEYS, tm), lambda b, i: (0, b * nt + i))),
        out_shape=(jax.ShapeDtypeStruct((B, S, D_MODEL), F32),
                   jax.ShapeDtypeStruct((B, S, D_MODEL), BF16),
                   jax.ShapeDtypeStruct((PEER_GROUPS * PEER_NKEYS, T), F32)),
        scratch_shapes=[pltpu.VMEM((POOL_HALO + tm, POOL_WIDTH), F32),
                        pltpu.VMEM((tm, D_MODEL), BF16)],
        compiler_params=_params(("arbitrary", "arbitrary")),
        name="mix",
    )(x, a, a, attn, gates, w_pool, pool_scale, w_out, g2, wq_t, sub_keys)


def _col_max(vs):
    m = vs[0]
    for v in vs[1:]:
        m = jnp.maximum(m, v)
    return jnp.max(m, axis=0, keepdims=True)


def _col_min(vs):
    m = vs[0]
    for v in vs[1:]:
        m = jnp.minimum(m, v)
    return jnp.min(m, axis=0, keepdims=True)


def _extract(vals, order, payload, n):
    big = jnp.float32(1e9)
    out = []
    for _ in range(n):
        m = _col_max(vals)
        first = _col_min([jnp.where(v == m, o, big) for v, o in zip(vals, order)])
        hits = [o == first for o in order]
        if payload is None:
            picked = first
        else:
            picked = _col_max([jnp.where(hh, p, -1.0) for hh, p in zip(hits, payload)])
        vals = [jnp.where(hh, -jnp.inf, v) for hh, v in zip(hits, vals)]
        out.append((m, picked))
    return out


def _topk_kernel(sc_ref, isel_ref, jsel_ref, g_ref, ts_ref, ti_ref, e_ref, s_ref):
    nv = PEER_NKEYS // SUBLANES
    sub = lax.broadcasted_iota(jnp.int32, (SUBLANES, LANES), 0).astype(F32)
    row16 = lax.broadcasted_iota(jnp.int32, (PEER_TOPK, LANES), 0)

    def pack_rows(rows):
        acc = jnp.zeros((PEER_TOPK, LANES), F32)
        for r, v in enumerate(rows):
            acc = jnp.where(row16 == r, v, acc)
        return acc

    def stage1(g, carry):
        base = pl.multiple_of(g * PEER_NKEYS, PEER_NKEYS)
        vals = [sc_ref[pl.ds(base + v * SUBLANES, SUBLANES), :] for v in range(nv)]
        keys = [sub + float(v * SUBLANES) for v in range(nv)]
        res = _extract(vals, keys, None, PEER_TOPK)
        out = pl.multiple_of(g * PEER_TOPK, PEER_TOPK)
        ts_ref[pl.ds(out, PEER_TOPK), :] = pack_rows([m for m, _ in res])
        ti_ref[pl.ds(out, PEER_TOPK), :] = pack_rows([p for _, p in res])
        return carry

    lax.fori_loop(0, PEER_GROUPS, stage1, 0)

    def stage2(h, carry):
        o0 = pl.multiple_of(h * 2 * PEER_TOPK, PEER_TOPK)
        o1 = o0 + PEER_TOPK
        row = lambda ref, o, a: jnp.broadcast_to(ref[pl.ds(o + a, 1), :], (SUBLANES, LANES))
        slab = lambda ref, o, a: ref[pl.ds(o + a, SUBLANES), :]
        s0_lo, s0_hi, s1_lo, s1_hi = slab(ts_ref, o0, 0), slab(ts_ref, o0, 8), slab(ts_ref, o1, 0), slab(ts_ref, o1, 8)
        i0_lo, i0_hi, i1_lo, i1_hi = slab(ti_ref, o0, 0), slab(ti_ref, o0, 8), slab(ti_ref, o1, 0), slab(ti_ref, o1, 8)
        nk = float(PEER_NKEYS)
        cands = [
            (row(ts_ref, o0, 0) + s1_lo, sub, row(ti_ref, o0, 0) * nk + i1_lo, None),
            (row(ts_ref, o0, 0) + s1_hi, sub + 8.0, row(ti_ref, o0, 0) * nk + i1_hi, None),
            (row(ts_ref, o0, 1) + s1_lo, sub + 16.0, row(ti_ref, o0, 1) * nk + i1_lo, None),
            (s0_hi + row(ts_ref, o1, 0), (sub + 8.0) * 16.0, i0_hi * nk + row(ti_ref, o1, 0), None),
        ]
        for b, top in ((0, 7), (1, 7), (2, 4), (3, 3), (4, 2)):
            cands.append((s0_lo + row(ts_ref, o1, b), sub * 16.0 + float(b),
                          i0_lo * nk + row(ti_ref, o1, b), (sub >= 2.0) & (sub <= float(top))))
        vals = [c[0] if c[3] is None else jnp.where(c[3], c[0], -jnp.inf) for c in cands]
        res = _extract(vals, [c[1] for c in cands], [c[2] for c in cands], PEER_TOPK)
        out = pl.multiple_of(h * PEER_TOPK, PEER_TOPK)
        best = pack_rows([m for m, _ in res])
        ex = jnp.exp(best - res[0][0])
        s_ref[pl.ds(out, PEER_TOPK), :] = ex / jnp.sum(ex, axis=0, keepdims=True)
        e_ref[pl.ds(out, PEER_TOPK), :] = pack_rows([p for _, p in res])
        return carry

    lax.fori_loop(0, PEER_HEADS, stage2, 0)

    e = e_ref[...].T
    i_sel = jnp.floor(e * (1.0 / PEER_NKEYS))
    isel_ref[...] = i_sel
    jsel_ref[...] = e - i_sel * PEER_NKEYS
    g_ref[...] = s_ref[...].T


def _topk(sc_t):
    T = sc_t.shape[1]
    out = pl.BlockSpec((LANES, PEER_SEL), lambda i: (i, 0))
    shp = jax.ShapeDtypeStruct((T, PEER_SEL), F32)
    return pl.pallas_call(
        _topk_kernel,
        grid=(T // LANES,),
        in_specs=[pl.BlockSpec((PEER_GROUPS * PEER_NKEYS, LANES), lambda i: (0, i))],
        out_specs=(out, out, out),
        out_shape=(shp, shp, shp),
        scratch_shapes=[pltpu.VMEM((PEER_GROUPS * PEER_TOPK, LANES), F32),
                        pltpu.VMEM((PEER_GROUPS * PEER_TOPK, LANES), F32),
                        pltpu.VMEM((PEER_SEL, LANES), F32),
                        pltpu.VMEM((PEER_SEL, LANES), F32)],
        compiler_params=_params(("arbitrary",)),
        name="topk",
    )(sc_t)


G_TILE = 128
G_PITCH = G_TILE + SUBLANES


def _gbuild_kernel(isel_ref, jsel_ref, g_ref, out_ref, scr_ref):
    sub = lax.broadcasted_iota(jnp.int32, (PEER_NKEYS, PEER_SEL), 0).astype(F32)

    def body(c, carry):
        irow = isel_ref[pl.ds(c, 1), :]
        jrow = jsel_ref[pl.ds(c, 1), :]
        grow = g_ref[pl.ds(c, 1), :]
        a = jnp.where(sub == irow, grow, 0.0).astype(BF16)
        b = jnp.where(sub == jrow, 1.0, 0.0).astype(BF16)
        scr_ref[pl.ds(c, PEER_NKEYS, stride=G_PITCH), :] = _dot_nt(a, b)
        return carry

    lax.fori_loop(0, G_TILE, body, 0)
    for i in range(PEER_NKEYS):
        out_ref[i] = scr_ref[i * G_PITCH:i * G_PITCH + G_TILE, :].astype(BF16)


def _gbuild(isel, jsel, gates):
    T = isel.shape[0]
    row = pl.BlockSpec((G_TILE, PEER_SEL), lambda t: (t, 0))
    return pl.pallas_call(
        _gbuild_kernel,
        grid=(T // G_TILE,),
        in_specs=[row, row, row],
        out_specs=pl.BlockSpec((PEER_NKEYS, G_TILE, PEER_NKEYS), lambda t: (0, t, 0)),
        out_shape=jax.ShapeDtypeStruct((PEER_NKEYS, T, PEER_NKEYS), BF16),
        scratch_shapes=[pltpu.VMEM((PEER_NKEYS * G_PITCH, PEER_NKEYS), F32)],
        compiler_params=_params(("arbitrary",)),
        name="gbuild",
    )(isel, jsel, gates)


def _gelu_tanh(x):
    return 0.5 * x * (1.0 + jnp.tanh(math.sqrt(2.0 / math.pi) * (x + 0.044715 * (x * x * x))))


def _peer_kernel(hn_ref, u_ref, v_ref, g_ref, h_ref, out_ref, w_ref):
    e = pl.program_id(1)
    n_rows = g_ref.shape[0]

    @pl.when(e == 0)
    def _():
        out_ref[...] = h_ref[...]

    hn = hn_ref[...]
    for i in range(0, n_rows, 2):
        cols = slice(i * PEER_NKEYS, (i + 2) * PEER_NKEYS)
        act = _gelu_tanh(_dot_nt(hn, u_ref[cols, :]))
        gate = jnp.concatenate([g_ref[i], g_ref[i + 1]], axis=1).astype(F32)
        w_ref[:, cols] = (act * gate).astype(BF16)
    out_ref[...] += _dot(w_ref[...], v_ref[...])


def _peer(hn2, h2, u, v, gmat, tt, eb):
    T = hn2.shape[0]
    n_rows = eb // PEER_NKEYS
    tok = lambda dt: pl.BlockSpec((tt, D_MODEL), lambda t, e: (t, 0))
    exp = pl.BlockSpec((eb, D_MODEL), lambda t, e: (e, 0))
    return pl.pallas_call(
        _peer_kernel,
        grid=(T // tt, PEER_N // eb),
        in_specs=[tok(BF16), exp, exp,
                  pl.BlockSpec((n_rows, tt, PEER_NKEYS), lambda t, e: (e, t, 0)),
                  tok(F32)],
        out_specs=pl.BlockSpec((tt, D_MODEL), lambda t, e: (t, 0)),
        out_shape=jax.ShapeDtypeStruct((T, D_MODEL), F32),
        scratch_shapes=[pltpu.VMEM((tt, eb), BF16)],
        compiler_params=_params(("arbitrary", "arbitrary")),
        name="peer",
    )(hn2, u, v, gmat, h2)


def _layer(h, norm1_g, w_in, q_norm_g, k_norm_g, w_pool, pool_scale, w_out, norm2_g,
           w_query, sub_keys, expert_u, expert_v, biases):
    B, S, D = h.shape
    T = B * S
    tm = min(512, S)
    head = np.arange(KV_WIDTH) // HEAD_DIM
    bd = jnp.asarray(head[:, None] == head[None, :], BF16)
    qg = (jnp.tile(q_norm_g, ATT_HEADS) * (HEAD_DIM ** -0.5)).reshape(1, KV_WIDTH)
    kg = jnp.tile(k_norm_g, ATT_HEADS).reshape(1, KV_WIDTH)

    a, q0, q1, q2, k, v, gates = _proj(h.reshape(T, D), norm1_g.reshape(1, D), w_in.astype(BF16), bd, qg, kg, tm)
    k3, v3 = k.reshape(B, S, KV_WIDTH), v.reshape(B, S, KV_WIDTH)
    state = None
    for g, (qp, (_, dilation)) in enumerate(zip((q0, q1, q2), DILATED_PATTERNS)):
        state = _attn_pattern(qp.reshape(B, S, KV_WIDTH), k3, v3, biases[g], state, dilation,
                              final=(g == N_PATTERNS - 1))
    attn = state

    wq_t = w_query.T.astype(BF16)
    sk = sub_keys.reshape(PEER_GROUPS, PEER_NKEYS, PEER_HALF).astype(BF16)
    h_new, hn, sc_t = _mix(h, a.reshape(B, S, POOL_WIDTH), attn, gates.reshape(B, S, 2 * D),
                           w_pool.astype(BF16), pool_scale.reshape(1, D), w_out.astype(BF16),
                           norm2_g.reshape(1, D), wq_t, sk, tm)
    isel, jsel, gsel = _topk(sc_t)
    gmat = _gbuild(isel, jsel, gsel)
    out = _peer(hn.reshape(T, D), h_new.reshape(T, D), expert_u.astype(BF16), expert_v.astype(BF16),
                gmat, min(1024, T), 1024)
    return out.reshape(B, S, D)


def kernel(x, norm1_g, w_in, q_norm_g, k_norm_g, w_pool, pool_scale, w_out, norm2_g, w_query, sub_keys,
           expert_u, expert_v):
    slopes = _alibi_slopes(N_PATTERNS * ATT_HEADS).reshape(N_PATTERNS, ATT_HEADS)
    biases = [jnp.asarray(_attn_bias(slopes[g], dil)) for g, (_, dil) in enumerate(DILATED_PATTERNS)]
    h = x
    for l in range(norm1_g.shape[0]):
        h = _layer(h, norm1_g[l], w_in[l], q_norm_g[l], k_norm_g[l], w_pool[l], pool_scale[l], w_out[l],
                   norm2_g[l], w_query[l], sub_keys[l], expert_u[l], expert_v[l], biases)
    return h
```

```python
import functools
import math

import numpy as np
import jax
import jax.numpy as jnp
from jax import lax
from jax.experimental import pallas as pl
from jax.experimental.pallas import tpu as pltpu

D_MODEL = 1024
EPS = 1e-6
NEG_INF = -1e30

POOL_WINDOWS = (2, 4, 8, 16)
POOL_GROUPS = 4
POOL_IN = 128
POOL_WIDTH = 512
POOL_OUT = 256
POOL_HALO = 16

ATT_HEADS = 16
HEAD_DIM = 64
DILATED_PATTERNS = ((128, 1), (512, 4), (2048, 16))
N_PATTERNS = 3
BAND = 128
KV_WIDTH = 1024
Q_OFF = POOL_WIDTH
K_OFF = Q_OFF + N_PATTERNS * KV_WIDTH
V_OFF = K_OFF + KV_WIDTH
GATE_OFF = V_OFF + KV_WIDTH
IN_WIDTH = GATE_OFF + 2 * D_MODEL

PEER_HEADS = 8
PEER_NKEYS = 128
PEER_N = PEER_NKEYS * PEER_NKEYS
PEER_HALF = 128
PEER_TOPK = 16
PEER_GROUPS = 2 * PEER_HEADS
PEER_SEL = PEER_HEADS * PEER_TOPK

LANES = 128
SUBLANES = 8
VMEM_LIMIT = 56 * 1024 * 1024

BF16 = jnp.bfloat16
F32 = jnp.float32


def _dot(a, b):
    return jnp.dot(a, b, preferred_element_type=F32)


def _dot_nt(a, b):
    return lax.dot_general(a, b, (((1,), (1,)), ((), ())), preferred_element_type=F32)


def _params(semantics):
    return pltpu.CompilerParams(dimension_semantics=semantics, vmem_limit_bytes=VMEM_LIMIT)


def _const_spec(shape):
    n = len(shape)
    return pl.BlockSpec(shape, lambda *_: (0,) * n)


def _proj_kernel(x_ref, g1_ref, w_ref, bd_ref, qg_ref, kg_ref,
                 a_ref, q0_ref, q1_ref, q2_ref, k_ref, v_ref, gate_ref):
    x = x_ref[...]
    inv = lax.rsqrt(jnp.mean(x * x, axis=-1, keepdims=True) + EPS)
    xn = (x * inv * g1_ref[...]).astype(BF16)

    def head_norm(y, gain):
        ss = _dot((y * y).astype(BF16), bd_ref[...])
        return (y * lax.rsqrt(ss * (1.0 / HEAD_DIM) + EPS) * gain).astype(BF16)

    a_ref[...] = _dot(xn, w_ref[:, 0:POOL_WIDTH])
    for g, q_ref in enumerate((q0_ref, q1_ref, q2_ref)):
        lo = Q_OFF + g * KV_WIDTH
        q_ref[...] = head_norm(_dot(xn, w_ref[:, lo:lo + KV_WIDTH]), qg_ref[...])
    k_ref[...] = head_norm(_dot(xn, w_ref[:, K_OFF:K_OFF + KV_WIDTH]), kg_ref[...])
    v_ref[...] = _dot(xn, w_ref[:, V_OFF:V_OFF + KV_WIDTH]).astype(BF16)
    gate_ref[...] = jax.nn.sigmoid(_dot(xn, w_ref[:, GATE_OFF:IN_WIDTH])).astype(BF16)


def _proj(x2, g1, w_in, bd, qg, kg, tm):
    T = x2.shape[0]
    row = lambda w: pl.BlockSpec((tm, w), lambda i: (i, 0))
    once = lambda shape: pl.BlockSpec(shape, lambda i: (0, 0), pipeline_mode=pl.Buffered(1))
    out_shape = (
        jax.ShapeDtypeStruct((T, POOL_WIDTH), F32),
        jax.ShapeDtypeStruct((T, KV_WIDTH), BF16),
        jax.ShapeDtypeStruct((T, KV_WIDTH), BF16),
        jax.ShapeDtypeStruct((T, KV_WIDTH), BF16),
        jax.ShapeDtypeStruct((T, KV_WIDTH), BF16),
        jax.ShapeDtypeStruct((T, KV_WIDTH), BF16),
        jax.ShapeDtypeStruct((T, 2 * D_MODEL), BF16),
    )
    return pl.pallas_call(
        _proj_kernel,
        grid=(T // tm,),
        in_specs=[row(D_MODEL), once((1, D_MODEL)), once((D_MODEL, IN_WIDTH)),
                  once((KV_WIDTH, KV_WIDTH)), once((1, KV_WIDTH)), once((1, KV_WIDTH))],
        out_specs=(row(POOL_WIDTH), row(KV_WIDTH), row(KV_WIDTH), row(KV_WIDTH),
                   row(KV_WIDTH), row(KV_WIDTH), row(2 * D_MODEL)),
        out_shape=out_shape,
        compiler_params=_params(("arbitrary",)),
        name="proj",
    )(x2, g1, w_in, bd, qg, kg)


def _alibi_slopes(n):
    def geometric(k):
        start = 2.0 ** (-8.0 / k)
        return [start ** (i + 1) for i in range(k)]
    p = 2 ** int(math.floor(math.log2(n)))
    s = geometric(p) + geometric(2 * p)[0::2][: n - p]
    return np.sort(np.array(s, dtype=np.float32))[::-1].copy()


def _attn_bias(slopes, dilation):
    qi = np.arange(BAND)[:, None]
    ki = np.arange(2 * BAND)[None, :]
    step = qi + BAND - ki
    valid = (step >= 0) & (step <= BAND)
    alibi = -slopes[:, None, None] * (step * dilation).astype(np.float32)[None]
    general = np.where(valid[None], alibi, np.float32(NEG_INF)).astype(np.float32)
    first = np.where((ki >= BAND)[None], general, np.float32(NEG_INF)).astype(np.float32)
    return np.stack([general, first])


HEAD_PAIRS = ATT_HEADS // 2


def _attn_kernel(q_ref, kp_ref, kc_ref, vp_ref, vc_ref, bias_ref, o_ref, lse_ref, s_ref, m_ref):
    first = (pl.program_id(2) == 0).astype(jnp.int32)
    lane = lax.broadcasted_iota(jnp.int32, (BAND, LANES), 1)
    low = lane < HEAD_DIM
    low_b = (lax.broadcasted_iota(jnp.int32, (1, LANES), 1) < HEAD_DIM).astype(BF16)
    lane_sel = (low_b, 1 - low_b)

    for h in range(ATT_HEADS):
        cols = slice((h // 2) * LANES, (h // 2 + 1) * LANES)
        qm = q_ref[0, :, cols] * lane_sel[h % 2]
        sp = _dot_nt(qm, kp_ref[0, :, cols]) + bias_ref[first, h, :, 0:BAND]
        sc = _dot_nt(qm, kc_ref[0, :, cols]) + bias_ref[first, h, :, BAND:2 * BAND]
        s_ref[h, :, 0:BAND] = sp
        s_ref[h, :, BAND:2 * BAND] = sc
        m_ref[h] = jnp.broadcast_to(jnp.max(jnp.maximum(sp, sc), axis=-1, keepdims=True), (BAND, LANES))

    lse_tile = jnp.zeros((BAND, LANES), F32)
    for j in range(HEAD_PAIRS):
        cols = slice(j * LANES, (j + 1) * LANES)
        res = []
        for half in range(2):
            h = 2 * j + half
            own, other = lane_sel[half], lane_sel[1 - half]
            pp = jnp.exp(s_ref[h, :, 0:BAND] - m_ref[h]).astype(BF16)
            pc = jnp.exp(s_ref[h, :, BAND:2 * BAND] - m_ref[h]).astype(BF16)
            res.append(_dot(pp, vp_ref[0, :, cols] * own + other) + _dot(pc, vc_ref[0, :, cols] * own + other))
        num = jnp.where(low, res[0], res[1])
        den = pltpu.roll(jnp.where(low, res[1], res[0]), HEAD_DIM, axis=1)
        o_ref[0, :, cols] = (num / den).astype(BF16)
        lse = jnp.where(low, m_ref[2 * j], m_ref[2 * j + 1]) + jnp.log(den)
        lse_tile = jnp.where((lane == j) | (lane == HEAD_DIM + j), lse, lse_tile)
    lse_ref[0] = lse_tile


def _attn_pattern(q, k, v, bias, dilation):
    B, S, _ = q.shape
    d = dilation
    rows = S // d
    nb = rows // BAND
    view = lambda a: a.reshape(B, rows, d * KV_WIDTH)
    cur = lambda w: pl.BlockSpec((1, BAND, w), lambda b, r, n: (b, n, r))
    prev = pl.BlockSpec((1, BAND, KV_WIDTH), lambda b, r, n: (b, jnp.maximum(n - 1, 0), r))
    o, lse = pl.pallas_call(
        _attn_kernel,
        grid=(B, d, nb),
        in_specs=[cur(KV_WIDTH), prev, cur(KV_WIDTH), prev, cur(KV_WIDTH),
                  pl.BlockSpec(bias.shape, lambda b, r, n: (0, 0, 0, 0))],
        out_specs=(cur(KV_WIDTH), cur(LANES)),
        out_shape=(jax.ShapeDtypeStruct((B, rows, d * KV_WIDTH), BF16),
                   jax.ShapeDtypeStruct((B, rows, d * LANES), F32)),
        scratch_shapes=[pltpu.VMEM((ATT_HEADS, BAND, 2 * BAND), F32),
                        pltpu.VMEM((ATT_HEADS, BAND, LANES), F32)],
        compiler_params=_params(("arbitrary", "arbitrary", "arbitrary")),
        name=f"attn_d{d}",
    )(view(q), view(k), view(k), view(v), view(v), bias)
    return o.reshape(B, S, KV_WIDTH), lse.reshape(B, S, LANES)


def _head_expand():
    e = np.zeros((LANES, KV_WIDTH), np.float32)
    for h in range(ATT_HEADS):
        e[(h % 2) * HEAD_DIM + h // 2, h * HEAD_DIM:(h + 1) * HEAD_DIM] = 1.0
    return e


def _mix_kernel(x_ref, a_ref, ah_ref, o0_ref, o1_ref, o2_ref, l0_ref, l1_ref, l2_ref, ex_ref, gate_ref,
                wp_ref, ps_ref, wo_ref, g2_ref, wq_ref, sk_ref,
                h_ref, hn_ref, sc_ref, ext_ref, merged_ref):
    tm = x_ref.shape[1]
    i = pl.program_id(1)
    lses = [r[0] for r in (l0_ref, l1_ref, l2_ref)]
    top = jnp.maximum(jnp.maximum(lses[0], lses[1]), lses[2])
    es = [jnp.exp(l - top) for l in lses]
    tot = es[0] + es[1] + es[2]
    w_hi, w_lo = [], []
    for e in es:
        w = e / tot
        hi = w.astype(BF16)
        w_hi.append(hi)
        w_lo.append((w - hi.astype(F32)).astype(BF16))
    ext_ref[0:POOL_HALO, :] = jnp.where(i == 0, 0.0, ah_ref[0])
    ext_ref[POOL_HALO:POOL_HALO + tm, :] = a_ref[0]
    t = i * tm + lax.broadcasted_iota(jnp.int32, (tm, 1), 0)
    for g, w in enumerate(POOL_WINDOWS):
        cols = slice(g * POOL_IN, (g + 1) * POOL_IN)
        wsum = ext_ref[POOL_HALO:POOL_HALO + tm, cols]
        for j in range(1, w):
            wsum = wsum + ext_ref[POOL_HALO - j:POOL_HALO - j + tm, cols]
        count = jnp.minimum(t + 1, w).astype(F32)
        dlt = (wsum / count - a_ref[0, :, cols]).astype(BF16)
        ocol = slice(g * POOL_OUT, (g + 1) * POOL_OUT)
        pool = _dot(dlt, wp_ref[g]) * ps_ref[:, ocol]
        attn = jnp.zeros((tm, POOL_OUT), F32)
        for o_ref, hi, lo in zip((o0_ref, o1_ref, o2_ref), w_hi, w_lo):
            weight = _dot(hi, ex_ref[:, ocol]) + _dot(lo, ex_ref[:, ocol])
            attn = attn + weight * o_ref[0, :, ocol].astype(F32)
        merged = (gate_ref[0, :, ocol].astype(F32) * pool
                  + gate_ref[0, :, D_MODEL + g * POOL_OUT:D_MODEL + (g + 1) * POOL_OUT].astype(F32) * attn)
        merged_ref[:, ocol] = merged.astype(BF16)
    h = x_ref[0] + _dot(merged_ref[...], wo_ref[...])
    h_ref[0] = h
    hn = (h * lax.rsqrt(jnp.mean(h * h, axis=-1, keepdims=True) + EPS) * g2_ref[...]).astype(BF16)
    hn_ref[0] = hn
    qt = _dot_nt(wq_ref[...], hn).astype(BF16)
    for g in range(PEER_GROUPS):
        head, half = divmod(g, 2)
        rows = slice(g * PEER_NKEYS, (g + 1) * PEER_NKEYS)
        sc = _dot(sk_ref[g], qt[rows, :])
        for j in range(tm // LANES):
            sc_ref[j, pl.ds(half * PEER_HEADS * PEER_NKEYS + head, PEER_NKEYS, stride=PEER_HEADS), :] = (
                sc[:, j * LANES:(j + 1) * LANES])


def _mix(x, a, attn_outs, attn_lses, expand, gates, w_pool, pool_scale, w_out, g2, wq_t, sub_keys, tm):
    B, S, _ = x.shape
    T = B * S
    nt = S // tm
    halo_per_tile = tm // POOL_HALO
    tile = lambda w: pl.BlockSpec((1, tm, w), lambda b, i: (b, i, 0))
    halo = pl.BlockSpec((1, POOL_HALO, POOL_WIDTH), lambda b, i: (b, jnp.maximum(i * halo_per_tile - 1, 0), 0))
    once = lambda shape: pl.BlockSpec(shape, lambda b, i: (0,) * len(shape), pipeline_mode=pl.Buffered(1))
    return pl.pallas_call(
        _mix_kernel,
        grid=(B, nt),
        in_specs=[tile(D_MODEL), tile(POOL_WIDTH), halo,
                  tile(KV_WIDTH), tile(KV_WIDTH), tile(KV_WIDTH), tile(LANES), tile(LANES), tile(LANES),
                  once(expand.shape), tile(2 * D_MODEL),
                  once(w_pool.shape), once((1, D_MODEL)), once((D_MODEL, D_MODEL)), once((1, D_MODEL)),
                  once(wq_t.shape), once(sub_keys.shape)],
        out_specs=(tile(D_MODEL), tile(D_MODEL),
                   pl.BlockSpec((tm // LANES, PEER_GROUPS * PEER_NKEYS, LANES), lambda b, i: (b * nt + i, 0, 0))),
        out_shape=(jax.ShapeDtypeStruct((B, S, D_MODEL), F32),
                   jax.ShapeDtypeStruct((B, S, D_MODEL), BF16),
                   jax.ShapeDtypeStruct((T // LANES, PEER_GROUPS * PEER_NKEYS, LANES), F32)),
        scratch_shapes=[pltpu.VMEM((POOL_HALO + tm, POOL_WIDTH), F32),
                        pltpu.VMEM((tm, D_MODEL), BF16)],
        compiler_params=_params(("arbitrary", "arbitrary")),
        name="mix",
    )(x, a, a, *attn_outs, *attn_lses, expand, gates, w_pool, pool_scale, w_out, g2, wq_t, sub_keys)


def _tree(op, xs):
    xs = list(xs)
    while len(xs) > 1:
        nxt = [op(xs[k], xs[k + 1]) for k in range(0, len(xs) - 1, 2)]
        if len(xs) % 2:
            nxt.append(xs[-1])
        xs = nxt
    return xs[0]


_PAIRS = tuple((a, b) for a in range(PEER_TOPK) for b in range(PEER_TOPK) if (a + 1) * (b + 1) <= PEER_TOPK)
_TIE_LAST = 1e9


def _topk_kernel(sc_ref, isel_ref, jsel_ref, g_ref, val_ref, ts_ref, ti_ref, cand_ref, code_ref, e_ref, s_ref):
    half_rows = PEER_HEADS * PEER_NKEYS

    def key_rows(half, n):
        return pl.ds(half * half_rows + n * PEER_HEADS, PEER_HEADS)

    first_max = []
    for half in range(2):
        vals = [sc_ref[0, key_rows(half, n), :] for n in range(PEER_NKEYS)]
        for n, v in enumerate(vals):
            val_ref[key_rows(half, n), :] = v
        first_max.append(_tree(jnp.maximum, vals))

    def stage1(r, maxes):
        nxt = []
        for half in range(2):
            m = maxes[half]
            vals = [val_ref[key_rows(half, n), :] for n in range(PEER_NKEYS)]
            idx = _tree(jnp.minimum, [jnp.where(v == m, float(n), _TIE_LAST) for n, v in enumerate(vals)])
            ts_ref[half, r] = m
            ti_ref[half, r] = idx
            vals = [jnp.where(idx == float(n), -jnp.inf, v) for n, v in enumerate(vals)]
            for n, v in enumerate(vals):
                val_ref[key_rows(half, n), :] = v
            nxt.append(_tree(jnp.maximum, vals))
        return tuple(nxt)

    lax.fori_loop(0, PEER_TOPK, stage1, tuple(first_max))

    cands = []
    for k, (a, b) in enumerate(_PAIRS):
        c = ts_ref[0, a] + ts_ref[1, b]
        cand_ref[k] = c
        code_ref[k] = ti_ref[0, a] * float(PEER_NKEYS) + ti_ref[1, b]
        cands.append(c)
    flat = [float(a * PEER_TOPK + b) for a, b in _PAIRS]

    def stage2(r, m):
        vals = [cand_ref[k] for k in range(len(_PAIRS))]
        first = _tree(jnp.minimum, [jnp.where(v == m, f, _TIE_LAST) for f, v in zip(flat, vals)])
        hits = [first == f for f in flat]
        e_ref[r] = _tree(jnp.maximum, [jnp.where(hh, code_ref[k], -1.0) for k, hh in enumerate(hits)])
        s_ref[r] = m
        vals = [jnp.where(hh, -jnp.inf, v) for hh, v in zip(hits, vals)]
        for k, v in enumerate(vals):
            cand_ref[k] = v
        return _tree(jnp.maximum, vals)

    lax.fori_loop(0, PEER_TOPK, stage2, _tree(jnp.maximum, cands))

    best = s_ref[...]
    ex = jnp.exp(best - best[0:1])
    gate = ex / jnp.sum(ex, axis=0, keepdims=True)
    e = e_ref[...].reshape(PEER_SEL, LANES).T
    i_sel = jnp.floor(e * (1.0 / PEER_NKEYS))
    isel_ref[...] = i_sel
    jsel_ref[...] = e - i_sel * PEER_NKEYS
    g_ref[...] = gate.reshape(PEER_SEL, LANES).T


def _topk(sc_t):
    n_tiles = sc_t.shape[0]
    out = pl.BlockSpec((LANES, PEER_SEL), lambda i: (i, 0))
    shp = jax.ShapeDtypeStruct((n_tiles * LANES, PEER_SEL), F32)
    vreg = (PEER_HEADS, LANES)
    return pl.pallas_call(
        _topk_kernel,
        grid=(n_tiles,),
        in_specs=[pl.BlockSpec((1, PEER_GROUPS * PEER_NKEYS, LANES), lambda i: (i, 0, 0))],
        out_specs=(out, out, out),
        out_shape=(shp, shp, shp),
        scratch_shapes=[pltpu.VMEM((PEER_GROUPS * PEER_NKEYS, LANES), F32),
                        pltpu.VMEM((2, PEER_TOPK) + vreg, F32),
                        pltpu.VMEM((2, PEER_TOPK) + vreg, F32),
                        pltpu.VMEM((len(_PAIRS),) + vreg, F32),
                        pltpu.VMEM((len(_PAIRS),) + vreg, F32),
                        pltpu.VMEM((PEER_TOPK,) + vreg, F32),
                        pltpu.VMEM((PEER_TOPK,) + vreg, F32)],
        compiler_params=_params(("arbitrary",)),
        name="topk",
    )(sc_t)


G_TILE = 128
G_PITCH = G_TILE + SUBLANES
G_UNROLL = 8


def _gbuild_kernel(isel_ref, jsel_ref, g_ref, out_ref, scr_ref):
    sub = lax.broadcasted_iota(jnp.int32, (PEER_NKEYS, PEER_SEL), 0).astype(F32)

    def body(c, carry):
        irow = isel_ref[pl.ds(c, 1), :]
        jrow = jsel_ref[pl.ds(c, 1), :]
        grow = g_ref[pl.ds(c, 1), :]
        a = jnp.where(sub == irow, grow, 0.0).astype(BF16)
        b = jnp.where(sub == jrow, 1.0, 0.0).astype(BF16)
        scr_ref[pl.ds(c, PEER_NKEYS, stride=G_PITCH), :] = _dot_nt(a, b)
        return carry

    lax.fori_loop(0, G_TILE, body, 0, unroll=G_UNROLL)
    for i in range(PEER_NKEYS):
        out_ref[i] = scr_ref[i * G_PITCH:i * G_PITCH + G_TILE, :].astype(BF16)


def _gbuild(isel, jsel, gates):
    T = isel.shape[0]
    row = pl.BlockSpec((G_TILE, PEER_SEL), lambda t: (t, 0))
    return pl.pallas_call(
        _gbuild_kernel,
        grid=(T // G_TILE,),
        in_specs=[row, row, row],
        out_specs=pl.BlockSpec((PEER_NKEYS, G_TILE, PEER_NKEYS), lambda t: (0, t, 0)),
        out_shape=jax.ShapeDtypeStruct((PEER_NKEYS, T, PEER_NKEYS), BF16),
        scratch_shapes=[pltpu.VMEM((PEER_NKEYS * G_PITCH, PEER_NKEYS), F32)],
        compiler_params=_params(("arbitrary",)),
        name="gbuild",
    )(isel, jsel, gates)


def _gelu_tanh(x):
    return 0.5 * x * (1.0 + jnp.tanh(math.sqrt(2.0 / math.pi) * (x + 0.044715 * (x * x * x))))


def _peer_kernel(hn_ref, u_ref, v_ref, g_ref, h_ref, out_ref, w_ref):
    e = pl.program_id(1)
    n_rows = g_ref.shape[0]

    @pl.when(e == 0)
    def _():
        out_ref[...] = h_ref[...]

    hn = hn_ref[...]
    for i in range(0, n_rows, 2):
        cols = slice(i * PEER_NKEYS, (i + 2) * PEER_NKEYS)
        act = _gelu_tanh(_dot_nt(hn, u_ref[cols, :]))
        gate = jnp.concatenate([g_ref[i], g_ref[i + 1]], axis=1).astype(F32)
        w_ref[:, cols] = (act * gate).astype(BF16)
    out_ref[...] += _dot(w_ref[...], v_ref[...])


def _peer(hn2, h2, u, v, gmat, tt, eb):
    T = hn2.shape[0]
    n_rows = eb // PEER_NKEYS
    tok = lambda dt: pl.BlockSpec((tt, D_MODEL), lambda t, e: (t, 0))
    exp = pl.BlockSpec((eb, D_MODEL), lambda t, e: (e, 0))
    return pl.pallas_call(
        _peer_kernel,
        grid=(T // tt, PEER_N // eb),
        in_specs=[tok(BF16), exp, exp,
                  pl.BlockSpec((n_rows, tt, PEER_NKEYS), lambda t, e: (e, t, 0)),
                  tok(F32)],
        out_specs=pl.BlockSpec((tt, D_MODEL), lambda t, e: (t, 0)),
        out_shape=jax.ShapeDtypeStruct((T, D_MODEL), F32),
        scratch_shapes=[pltpu.VMEM((tt, eb), BF16)],
        compiler_params=_params(("arbitrary", "arbitrary")),
        name="peer",
    )(hn2, u, v, gmat, h2)


def _layer(h, norm1_g, w_in, q_norm_g, k_norm_g, w_pool, pool_scale, w_out, norm2_g,
           w_query, sub_keys, expert_u, expert_v, biases):
    B, S, D = h.shape
    T = B * S
    tm = min(512, S)
    head = np.arange(KV_WIDTH) // HEAD_DIM
    bd = jnp.asarray(head[:, None] == head[None, :], BF16)
    qg = (jnp.tile(q_norm_g, ATT_HEADS) * (HEAD_DIM ** -0.5)).reshape(1, KV_WIDTH)
    kg = jnp.tile(k_norm_g, ATT_HEADS).reshape(1, KV_WIDTH)

    a, q0, q1, q2, k, v, gates = _proj(h.reshape(T, D), norm1_g.reshape(1, D), w_in.astype(BF16), bd, qg, kg, tm)
    k3, v3 = k.reshape(B, S, KV_WIDTH), v.reshape(B, S, KV_WIDTH)
    attn_outs, attn_lses = [], []
    for g, (qp, (_, dilation)) in enumerate(zip((q0, q1, q2), DILATED_PATTERNS)):
        o, lse = _attn_pattern(qp.reshape(B, S, KV_WIDTH), k3, v3, biases[g], dilation)
        attn_outs.append(o)
        attn_lses.append(lse)

    wq_t = w_query.T.astype(BF16)
    sk = sub_keys.reshape(PEER_GROUPS, PEER_NKEYS, PEER_HALF).astype(BF16)
    h_new, hn, sc_t = _mix(h, a.reshape(B, S, POOL_WIDTH), attn_outs, attn_lses,
                           jnp.asarray(_head_expand(), BF16), gates.reshape(B, S, 2 * D),
                           w_pool.astype(BF16), pool_scale.reshape(1, D), w_out.astype(BF16),
                           norm2_g.reshape(1, D), wq_t, sk, tm)
    isel, jsel, gsel = _topk(sc_t)
    gmat = _gbuild(isel, jsel, gsel)
    out = _peer(hn.reshape(T, D), h_new.reshape(T, D), expert_u.astype(BF16), expert_v.astype(BF16),
                gmat, min(1024, T), 1024)
    return out.reshape(B, S, D)


def kernel(x, norm1_g, w_in, q_norm_g, k_norm_g, w_pool, pool_scale, w_out, norm2_g, w_query, sub_keys,
           expert_u, expert_v):
    slopes = _alibi_slopes(N_PATTERNS * ATT_HEADS).reshape(N_PATTERNS, ATT_HEADS)
    biases = [jnp.asarray(_attn_bias(slopes[g], dil)) for g, (_, dil) in enumerate(DILATED_PATTERNS)]
    h = x
    for l in range(norm1_g.shape[0]):
        h = _layer(h, norm1_g[l], w_in[l], q_norm_g[l], k_norm_g[l], w_pool[l], pool_scale[l], w_out[l],
                   norm2_g[l], w_query[l], sub_keys[l], expert_u[l], expert_v[l], biases)
    return h
```

```python
import functools
import math

import numpy as np
import jax
import jax.numpy as jnp
from jax import lax
from jax.experimental import pallas as pl
from jax.experimental.pallas import tpu as pltpu

D_MODEL = 1024
EPS = 1e-6
NEG_INF = -1e30

POOL_WINDOWS = (2, 4, 8, 16)
POOL_GROUPS = 4
POOL_IN = 128
POOL_WIDTH = 512
POOL_OUT = 256
POOL_HALO = 16

ATT_HEADS = 16
HEAD_DIM = 64
DILATED_PATTERNS = ((128, 1), (512, 4), (2048, 16))
N_PATTERNS = 3
BAND = 128
KV_WIDTH = 1024
Q_OFF = POOL_WIDTH
K_OFF = Q_OFF + N_PATTERNS * KV_WIDTH
V_OFF = K_OFF + KV_WIDTH
GATE_OFF = V_OFF + KV_WIDTH
IN_WIDTH = GATE_OFF + 2 * D_MODEL

PEER_HEADS = 8
PEER_NKEYS = 128
PEER_N = PEER_NKEYS * PEER_NKEYS
PEER_HALF = 128
PEER_TOPK = 16
PEER_GROUPS = 2 * PEER_HEADS
PEER_SEL = PEER_HEADS * PEER_TOPK

LANES = 128
SUBLANES = 8
VMEM_LIMIT = 56 * 1024 * 1024

BF16 = jnp.bfloat16
F32 = jnp.float32


def _dot(a, b):
    return jnp.dot(a, b, preferred_element_type=F32)


def _dot_nt(a, b):
    return lax.dot_general(a, b, (((1,), (1,)), ((), ())), preferred_element_type=F32)


def _params(semantics):
    return pltpu.CompilerParams(dimension_semantics=semantics, vmem_limit_bytes=VMEM_LIMIT)


def _const_spec(shape):
    n = len(shape)
    return pl.BlockSpec(shape, lambda *_: (0,) * n)


KV_SLABS = KV_WIDTH // LANES
NORM_COLS = 256
DILATIONS = tuple(d for _, d in DILATED_PATTERNS)


def _store_dilated(y, slab_ref, out_refs):
    tm = y.shape[0]
    for d, ref in out_refs:
        if d == 1:
            ref[...] = y.astype(BF16)
    if all(d == 1 for d, _ in out_refs):
        return
    for s in range(KV_SLABS):
        slab_ref[s] = y[:, s * LANES:(s + 1) * LANES]
    for d, ref in out_refs:
        if d == 1:
            continue
        for r in range(d):
            for s in range(KV_SLABS):
                col = r * KV_WIDTH + s * LANES
                ref[:, col:col + LANES] = slab_ref[s, pl.ds(r, tm // d, stride=d), :].astype(BF16)


def _proj_kernel(x_ref, g1_ref, w_ref, bd_ref, qg_ref, kg_ref,
                 a_ref, q0_ref, q1_ref, q2_ref, k0_ref, k1_ref, k2_ref, v0_ref, v1_ref, v2_ref, gate_ref,
                 slab_ref):
    x = x_ref[...]
    inv = lax.rsqrt(jnp.mean(x * x, axis=-1, keepdims=True) + EPS)
    xn = (x * inv * g1_ref[...]).astype(BF16)

    def head_norm(y, gain):
        parts = []
        for c in range(0, KV_WIDTH, NORM_COLS):
            yc = y[:, c:c + NORM_COLS]
            ss = _dot((yc * yc).astype(BF16), bd_ref[...])
            parts.append(yc * lax.rsqrt(ss * (1.0 / HEAD_DIM) + EPS))
        return jnp.concatenate(parts, axis=1) * gain

    a_ref[...] = _dot(xn, w_ref[:, 0:POOL_WIDTH])
    for g, q_ref in enumerate((q0_ref, q1_ref, q2_ref)):
        lo = Q_OFF + g * KV_WIDTH
        q = head_norm(_dot(xn, w_ref[:, lo:lo + KV_WIDTH]), qg_ref[...])
        _store_dilated(q, slab_ref, [(DILATIONS[g], q_ref)])
    k = head_norm(_dot(xn, w_ref[:, K_OFF:K_OFF + KV_WIDTH]), kg_ref[...])
    _store_dilated(k, slab_ref, list(zip(DILATIONS, (k0_ref, k1_ref, k2_ref))))
    v = _dot(xn, w_ref[:, V_OFF:V_OFF + KV_WIDTH])
    _store_dilated(v, slab_ref, list(zip(DILATIONS, (v0_ref, v1_ref, v2_ref))))
    gate_ref[...] = jax.nn.sigmoid(_dot(xn, w_ref[:, GATE_OFF:IN_WIDTH])).astype(BF16)


def _proj(x2, g1, w_in, bd, qg, kg, tm):
    T = x2.shape[0]
    row = lambda w: pl.BlockSpec((tm, w), lambda i: (i, 0))
    dil = lambda d: pl.BlockSpec((tm // d, d * KV_WIDTH), lambda i: (i, 0))
    dil_shape = lambda d: jax.ShapeDtypeStruct((T // d, d * KV_WIDTH), BF16)
    once = lambda shape: pl.BlockSpec(shape, lambda i: (0, 0), pipeline_mode=pl.Buffered(1))
    views = [dil(d) for d in DILATIONS]
    view_shapes = [dil_shape(d) for d in DILATIONS]
    return pl.pallas_call(
        _proj_kernel,
        grid=(T // tm,),
        in_specs=[row(D_MODEL), once((1, D_MODEL)), once((D_MODEL, IN_WIDTH)),
                  once((NORM_COLS, NORM_COLS)), once((1, KV_WIDTH)), once((1, KV_WIDTH))],
        out_specs=(row(POOL_WIDTH), *views, *views, *views, row(2 * D_MODEL)),
        out_shape=(jax.ShapeDtypeStruct((T, POOL_WIDTH), F32), *view_shapes, *view_shapes, *view_shapes,
                   jax.ShapeDtypeStruct((T, 2 * D_MODEL), BF16)),
        scratch_shapes=[pltpu.VMEM((KV_SLABS, tm, LANES), F32)],
        compiler_params=_params(("arbitrary",)),
        name="proj",
    )(x2, g1, w_in, bd, qg, kg)


def _alibi_slopes(n):
    def geometric(k):
        start = 2.0 ** (-8.0 / k)
        return [start ** (i + 1) for i in range(k)]
    p = 2 ** int(math.floor(math.log2(n)))
    s = geometric(p) + geometric(2 * p)[0::2][: n - p]
    return np.sort(np.array(s, dtype=np.float32))[::-1].copy()


def _attn_bias(slopes, dilation):
    qi = np.arange(BAND)[:, None]
    ki = np.arange(2 * BAND)[None, :]
    step = qi + BAND - ki
    valid = (step >= 0) & (step <= BAND)
    alibi = -slopes[:, None, None] * (step * dilation).astype(np.float32)[None]
    general = np.where(valid[None], alibi, np.float32(NEG_INF)).astype(np.float32)
    first = np.where((ki >= BAND)[None], general, np.float32(NEG_INF)).astype(np.float32)
    return np.stack([general, first])


HEAD_PAIRS = ATT_HEADS // 2


def _attn_kernel(q_ref, kp_ref, kc_ref, vp_ref, vc_ref, bias_ref, o_ref, lse_ref, s_ref, m_ref):
    first = (pl.program_id(2) == 0).astype(jnp.int32)
    lane = lax.broadcasted_iota(jnp.int32, (BAND, LANES), 1)
    low = lane < HEAD_DIM
    low_b = (lax.broadcasted_iota(jnp.int32, (1, LANES), 1) < HEAD_DIM).astype(BF16)
    lane_sel = (low_b, 1 - low_b)

    for h in range(ATT_HEADS):
        cols = slice((h // 2) * LANES, (h // 2 + 1) * LANES)
        qm = q_ref[0, :, cols] * lane_sel[h % 2]
        sp = _dot_nt(qm, kp_ref[0, :, cols]) + bias_ref[first, h, :, 0:BAND]
        sc = _dot_nt(qm, kc_ref[0, :, cols]) + bias_ref[first, h, :, BAND:2 * BAND]
        s_ref[h, :, 0:BAND] = sp
        s_ref[h, :, BAND:2 * BAND] = sc
        m_ref[h] = jnp.broadcast_to(jnp.max(jnp.maximum(sp, sc), axis=-1, keepdims=True), (BAND, LANES))

    lse_tile = jnp.zeros((BAND, LANES), F32)
    for j in range(HEAD_PAIRS):
        cols = slice(j * LANES, (j + 1) * LANES)
        res = []
        for half in range(2):
            h = 2 * j + half
            own, other = lane_sel[half], lane_sel[1 - half]
            pp = jnp.exp(s_ref[h, :, 0:BAND] - m_ref[h]).astype(BF16)
            pc = jnp.exp(s_ref[h, :, BAND:2 * BAND] - m_ref[h]).astype(BF16)
            res.append(_dot(pp, vp_ref[0, :, cols] * own + other) + _dot(pc, vc_ref[0, :, cols] * own + other))
        num = jnp.where(low, res[0], res[1])
        den = pltpu.roll(jnp.where(low, res[1], res[0]), HEAD_DIM, axis=1)
        o_ref[0, :, cols] = (num / den).astype(BF16)
        lse = jnp.where(low, m_ref[2 * j], m_ref[2 * j + 1]) + jnp.log(den)
        lse_tile = jnp.where((lane == j) | (lane == HEAD_DIM + j), lse, lse_tile)
    lse_ref[0] = lse_tile


def _attn_pattern(q, k, v, bias, dilation, B):
    d = dilation
    rows = q.shape[0] // B
    nb = rows // BAND
    view = lambda a: a.reshape(B, rows, d * KV_WIDTH)
    cur = lambda w: pl.BlockSpec((1, BAND, w), lambda b, r, n: (b, n, r))
    prev = pl.BlockSpec((1, BAND, KV_WIDTH), lambda b, r, n: (b, jnp.maximum(n - 1, 0), r))
    o, lse = pl.pallas_call(
        _attn_kernel,
        grid=(B, d, nb),
        in_specs=[cur(KV_WIDTH), prev, cur(KV_WIDTH), prev, cur(KV_WIDTH),
                  pl.BlockSpec(bias.shape, lambda b, r, n: (0, 0, 0, 0))],
        out_specs=(cur(KV_WIDTH), cur(LANES)),
        out_shape=(jax.ShapeDtypeStruct((B, rows, d * KV_WIDTH), BF16),
                   jax.ShapeDtypeStruct((B, rows, d * LANES), F32)),
        scratch_shapes=[pltpu.VMEM((ATT_HEADS, BAND, 2 * BAND), F32),
                        pltpu.VMEM((ATT_HEADS, BAND, LANES), F32)],
        compiler_params=_params(("arbitrary", "arbitrary", "arbitrary")),
        name=f"attn_d{d}",
    )(view(q), view(k), view(k), view(v), view(v), bias)
    return o, lse


def _head_expand():
    e = np.zeros((LANES, KV_WIDTH), np.float32)
    for h in range(ATT_HEADS):
        e[(h % 2) * HEAD_DIM + h // 2, h * HEAD_DIM:(h + 1) * HEAD_DIM] = 1.0
    return e


def _mix_kernel(x_ref, a_ref, ah_ref, o0_ref, o1_ref, o2_ref, l0_ref, l1_ref, l2_ref, ex_ref, gate_ref,
                wp_ref, ps_ref, wo_ref, g2_ref, wq_ref, sk_ref,
                h_ref, hn_ref, sc_ref, ext_ref, merged_ref, onat_ref, lnat_ref):
    tm = x_ref.shape[1]
    i = pl.program_id(1)

    def natural_rows(src_ref, d, slabs, dst_ref):
        n = src_ref.shape[1]
        for r in range(d):
            for s in range(slabs):
                col = (r * slabs + s) * LANES
                dst_ref[s, pl.ds(r, n, stride=d), :] = src_ref[0, :, col:col + LANES].astype(F32)

    for p, (o_ref, l_ref) in enumerate(((o1_ref, l1_ref), (o2_ref, l2_ref))):
        natural_rows(o_ref, DILATIONS[p + 1], KV_SLABS, onat_ref.at[p])
        natural_rows(l_ref, DILATIONS[p + 1], 1, lnat_ref.at[p])

    def attn_cols(p, g):
        if p == 0:
            return o0_ref[0, :, g * POOL_OUT:(g + 1) * POOL_OUT].astype(F32)
        return jnp.concatenate([onat_ref[p - 1, 2 * g], onat_ref[p - 1, 2 * g + 1]], axis=1)

    lses = [l0_ref[0], lnat_ref[0, 0], lnat_ref[1, 0]]
    top = jnp.maximum(jnp.maximum(lses[0], lses[1]), lses[2])
    es = [jnp.exp(l - top) for l in lses]
    tot = es[0] + es[1] + es[2]
    w_hi, w_lo = [], []
    for e in es:
        w = e / tot
        hi = w.astype(BF16)
        w_hi.append(hi)
        w_lo.append((w - hi.astype(F32)).astype(BF16))
    ext_ref[0:POOL_HALO, :] = jnp.where(i == 0, 0.0, ah_ref[0])
    ext_ref[POOL_HALO:POOL_HALO + tm, :] = a_ref[0]
    t = i * tm + lax.broadcasted_iota(jnp.int32, (tm, 1), 0)
    for g, w in enumerate(POOL_WINDOWS):
        cols = slice(g * POOL_IN, (g + 1) * POOL_IN)
        wsum = ext_ref[POOL_HALO:POOL_HALO + tm, cols]
        for j in range(1, w):
            wsum = wsum + ext_ref[POOL_HALO - j:POOL_HALO - j + tm, cols]
        count = jnp.minimum(t + 1, w).astype(F32)
        dlt = (wsum / count - a_ref[0, :, cols]).astype(BF16)
        ocol = slice(g * POOL_OUT, (g + 1) * POOL_OUT)
        pool = _dot(dlt, wp_ref[g]) * ps_ref[:, ocol]
        attn = jnp.zeros((tm, POOL_OUT), F32)
        for p, (hi, lo) in enumerate(zip(w_hi, w_lo)):
            weight = _dot(hi, ex_ref[:, ocol]) + _dot(lo, ex_ref[:, ocol])
            attn = attn + weight * attn_cols(p, g)
        merged = (gate_ref[0, :, ocol].astype(F32) * pool
                  + gate_ref[0, :, D_MODEL + g * POOL_OUT:D_MODEL + (g + 1) * POOL_OUT].astype(F32) * attn)
        merged_ref[:, ocol] = merged.astype(BF16)
    h = x_ref[0] + _dot(merged_ref[...], wo_ref[...])
    h_ref[0] = h
    hn = (h * lax.rsqrt(jnp.mean(h * h, axis=-1, keepdims=True) + EPS) * g2_ref[...]).astype(BF16)
    hn_ref[0] = hn
    qt = _dot_nt(wq_ref[...], hn).astype(BF16)
    for g in range(PEER_GROUPS):
        head, half = divmod(g, 2)
        rows = slice(g * PEER_NKEYS, (g + 1) * PEER_NKEYS)
        sc = _dot(sk_ref[g], qt[rows, :])
        for j in range(tm // LANES):
            sc_ref[j, pl.ds(half * PEER_HEADS * PEER_NKEYS + head, PEER_NKEYS, stride=PEER_HEADS), :] = (
                sc[:, j * LANES:(j + 1) * LANES])


def _mix(x, a, attn_outs, attn_lses, expand, gates, w_pool, pool_scale, w_out, g2, wq_t, sub_keys, tm):
    B, S, _ = x.shape
    T = B * S
    nt = S // tm
    halo_per_tile = tm // POOL_HALO
    tile = lambda w: pl.BlockSpec((1, tm, w), lambda b, i: (b, i, 0))
    dil = lambda d, w: pl.BlockSpec((1, tm // d, d * w), lambda b, i: (b, i, 0))
    halo = pl.BlockSpec((1, POOL_HALO, POOL_WIDTH), lambda b, i: (b, jnp.maximum(i * halo_per_tile - 1, 0), 0))
    once = lambda shape: pl.BlockSpec(shape, lambda b, i: (0,) * len(shape), pipeline_mode=pl.Buffered(1))
    return pl.pallas_call(
        _mix_kernel,
        grid=(B, nt),
        in_specs=[tile(D_MODEL), tile(POOL_WIDTH), halo,
                  *[dil(d, KV_WIDTH) for d in DILATIONS], *[dil(d, LANES) for d in DILATIONS],
                  once(expand.shape), tile(2 * D_MODEL),
                  once(w_pool.shape), once((1, D_MODEL)), once((D_MODEL, D_MODEL)), once((1, D_MODEL)),
                  once(wq_t.shape), once(sub_keys.shape)],
        out_specs=(tile(D_MODEL), tile(D_MODEL),
                   pl.BlockSpec((tm // LANES, PEER_GROUPS * PEER_NKEYS, LANES), lambda b, i: (b * nt + i, 0, 0))),
        out_shape=(jax.ShapeDtypeStruct((B, S, D_MODEL), F32),
                   jax.ShapeDtypeStruct((B, S, D_MODEL), BF16),
                   jax.ShapeDtypeStruct((T // LANES, PEER_GROUPS * PEER_NKEYS, LANES), F32)),
        scratch_shapes=[pltpu.VMEM((POOL_HALO + tm, POOL_WIDTH), F32),
                        pltpu.VMEM((tm, D_MODEL), BF16),
                        pltpu.VMEM((N_PATTERNS - 1, KV_SLABS, tm, LANES), F32),
                        pltpu.VMEM((N_PATTERNS - 1, 1, tm, LANES), F32)],
        compiler_params=_params(("arbitrary", "arbitrary")),
        name="mix",
    )(x, a, a, *attn_outs, *attn_lses, expand, gates, w_pool, pool_scale, w_out, g2, wq_t, sub_keys)


def _tree(op, xs):
    xs = list(xs)
    while len(xs) > 1:
        nxt = [op(xs[k], xs[k + 1]) for k in range(0, len(xs) - 1, 2)]
        if len(xs) % 2:
            nxt.append(xs[-1])
        xs = nxt
    return xs[0]


_PAIRS = tuple((a, b) for a in range(PEER_TOPK) for b in range(PEER_TOPK) if (a + 1) * (b + 1) <= PEER_TOPK)
_TIE_LAST = 1e9


def _topk_kernel(sc_ref, isel_ref, jsel_ref, g_ref, val_ref, ts_ref, ti_ref, cand_ref, code_ref, e_ref, s_ref):
    half_rows = PEER_HEADS * PEER_NKEYS

    def key_rows(half, n):
        return pl.ds(half * half_rows + n * PEER_HEADS, PEER_HEADS)

    first_max = []
    for half in range(2):
        vals = [sc_ref[0, key_rows(half, n), :] for n in range(PEER_NKEYS)]
        for n, v in enumerate(vals):
            val_ref[key_rows(half, n), :] = v
        first_max.append(_tree(jnp.maximum, vals))

    def stage1(r, maxes):
        nxt = []
        for half in range(2):
            m = maxes[half]
            vals = [val_ref[key_rows(half, n), :] for n in range(PEER_NKEYS)]
            idx = _tree(jnp.minimum, [jnp.where(v == m, float(n), _TIE_LAST) for n, v in enumerate(vals)])
            ts_ref[half, r] = m
            ti_ref[half, r] = idx
            vals = [jnp.where(idx == float(n), -jnp.inf, v) for n, v in enumerate(vals)]
            for n, v in enumerate(vals):
                val_ref[key_rows(half, n), :] = v
            nxt.append(_tree(jnp.maximum, vals))
        return tuple(nxt)

    lax.fori_loop(0, PEER_TOPK, stage1, tuple(first_max))

    cands = []
    for k, (a, b) in enumerate(_PAIRS):
        c = ts_ref[0, a] + ts_ref[1, b]
        cand_ref[k] = c
        code_ref[k] = ti_ref[0, a] * float(PEER_NKEYS) + ti_ref[1, b]
        cands.append(c)
    flat = [float(a * PEER_TOPK + b) for a, b in _PAIRS]

    def stage2(r, m):
        vals = [cand_ref[k] for k in range(len(_PAIRS))]
        first = _tree(jnp.minimum, [jnp.where(v == m, f, _TIE_LAST) for f, v in zip(flat, vals)])
        hits = [first == f for f in flat]
        e_ref[r] = _tree(jnp.maximum, [jnp.where(hh, code_ref[k], -1.0) for k, hh in enumerate(hits)])
        s_ref[r] = m
        vals = [jnp.where(hh, -jnp.inf, v) for hh, v in zip(hits, vals)]
        for k, v in enumerate(vals):
            cand_ref[k] = v
        return _tree(jnp.maximum, vals)

    lax.fori_loop(0, PEER_TOPK, stage2, _tree(jnp.maximum, cands))

    best = s_ref[...]
    ex = jnp.exp(best - best[0:1])
    gate = ex / jnp.sum(ex, axis=0, keepdims=True)
    e = e_ref[...].reshape(PEER_SEL, LANES).T
    i_sel = jnp.floor(e * (1.0 / PEER_NKEYS))
    isel_ref[...] = i_sel
    jsel_ref[...] = e - i_sel * PEER_NKEYS
    g_ref[...] = gate.reshape(PEER_SEL, LANES).T


def _topk(sc_t):
    n_tiles = sc_t.shape[0]
    out = pl.BlockSpec((LANES, PEER_SEL), lambda i: (i, 0))
    shp = jax.ShapeDtypeStruct((n_tiles * LANES, PEER_SEL), F32)
    vreg = (PEER_HEADS, LANES)
    return pl.pallas_call(
        _topk_kernel,
        grid=(n_tiles,),
        in_specs=[pl.BlockSpec((1, PEER_GROUPS * PEER_NKEYS, LANES), lambda i: (i, 0, 0))],
        out_specs=(out, out, out),
        out_shape=(shp, shp, shp),
        scratch_shapes=[pltpu.VMEM((PEER_GROUPS * PEER_NKEYS, LANES), F32),
                        pltpu.VMEM((2, PEER_TOPK) + vreg, F32),
                        pltpu.VMEM((2, PEER_TOPK) + vreg, F32),
                        pltpu.VMEM((len(_PAIRS),) + vreg, F32),
                        pltpu.VMEM((len(_PAIRS),) + vreg, F32),
                        pltpu.VMEM((PEER_TOPK,) + vreg, F32),
                        pltpu.VMEM((PEER_TOPK,) + vreg, F32)],
        compiler_params=_params(("arbitrary",)),
        name="topk",
    )(sc_t)


G_TILE = 128
G_PITCH = G_TILE + SUBLANES
G_UNROLL = 128


def _gbuild_kernel(isel_ref, jsel_ref, g_ref, out_ref, scr_ref):
    sub = lax.broadcasted_iota(jnp.int32, (PEER_NKEYS, PEER_SEL), 0).astype(F32)

    def body(c, carry):
        irow = isel_ref[pl.ds(c, 1), :]
        jrow = jsel_ref[pl.ds(c, 1), :]
        grow = g_ref[pl.ds(c, 1), :]
        a = jnp.where(sub == irow, grow, 0.0).astype(BF16)
        b = jnp.where(sub == jrow, 1.0, 0.0).astype(BF16)
        scr_ref[pl.ds(c, PEER_NKEYS, stride=G_PITCH), :] = _dot_nt(a, b)
        return carry

    lax.fori_loop(0, G_TILE, body, 0, unroll=G_UNROLL)
    for i in range(PEER_NKEYS):
        out_ref[i] = scr_ref[i * G_PITCH:i * G_PITCH + G_TILE, :].astype(BF16)


def _gbuild(isel, jsel, gates):
    T = isel.shape[0]
    row = pl.BlockSpec((G_TILE, PEER_SEL), lambda t: (t, 0))
    return pl.pallas_call(
        _gbuild_kernel,
        grid=(T // G_TILE,),
        in_specs=[row, row, row],
        out_specs=pl.BlockSpec((PEER_NKEYS, G_TILE, PEER_NKEYS), lambda t: (0, t, 0)),
        out_shape=jax.ShapeDtypeStruct((PEER_NKEYS, T, PEER_NKEYS), BF16),
        scratch_shapes=[pltpu.VMEM((PEER_NKEYS * G_PITCH, PEER_NKEYS), F32)],
        compiler_params=_params(("arbitrary",)),
        name="gbuild",
    )(isel, jsel, gates)


def _gelu_tanh(x):
    return 0.5 * x * (1.0 + jnp.tanh(math.sqrt(2.0 / math.pi) * (x + 0.044715 * (x * x * x))))


def _peer_kernel(hn_ref, u_ref, v_ref, g_ref, h_ref, out_ref, w_ref):
    e = pl.program_id(1)
    n_rows = g_ref.shape[0]

    @pl.when(e == 0)
    def _():
        out_ref[...] = h_ref[...]

    hn = hn_ref[...]
    for i in range(0, n_rows, 2):
        cols = slice(i * PEER_NKEYS, (i + 2) * PEER_NKEYS)
        act = _gelu_tanh(_dot_nt(hn, u_ref[cols, :]))
        gate = jnp.concatenate([g_ref[i], g_ref[i + 1]], axis=1).astype(F32)
        w_ref[:, cols] = (act * gate).astype(BF16)
    out_ref[...] += _dot(w_ref[...], v_ref[...])


def _peer(hn2, h2, u, v, gmat, tt, eb):
    T = hn2.shape[0]
    n_rows = eb // PEER_NKEYS
    tok = lambda dt: pl.BlockSpec((tt, D_MODEL), lambda t, e: (t, 0))
    exp = pl.BlockSpec((eb, D_MODEL), lambda t, e: (e, 0))
    return pl.pallas_call(
        _peer_kernel,
        grid=(T // tt, PEER_N // eb),
        in_specs=[tok(BF16), exp, exp,
                  pl.BlockSpec((n_rows, tt, PEER_NKEYS), lambda t, e: (e, t, 0)),
                  tok(F32)],
        out_specs=pl.BlockSpec((tt, D_MODEL), lambda t, e: (t, 0)),
        out_shape=jax.ShapeDtypeStruct((T, D_MODEL), F32),
        scratch_shapes=[pltpu.VMEM((tt, eb), BF16)],
        compiler_params=_params(("arbitrary", "arbitrary")),
        name="peer",
    )(hn2, u, v, gmat, h2)


def _layer(h, norm1_g, w_in, q_norm_g, k_norm_g, w_pool, pool_scale, w_out, norm2_g,
           w_query, sub_keys, expert_u, expert_v, biases):
    B, S, D = h.shape
    T = B * S
    tm = min(512, S)
    head = np.arange(NORM_COLS) // HEAD_DIM
    bd = jnp.asarray(head[:, None] == head[None, :], BF16)
    qg = (jnp.tile(q_norm_g, ATT_HEADS) * (HEAD_DIM ** -0.5)).reshape(1, KV_WIDTH)
    kg = jnp.tile(k_norm_g, ATT_HEADS).reshape(1, KV_WIDTH)

    res = _proj(h.reshape(T, D), norm1_g.reshape(1, D), w_in.astype(BF16), bd, qg, kg, tm)
    a, qs, ks, vs, gates = res[0], res[1:4], res[4:7], res[7:10], res[10]
    attn_outs, attn_lses = [], []
    for g, dilation in enumerate(DILATIONS):
        o, lse = _attn_pattern(qs[g], ks[g], vs[g], biases[g], dilation, B)
        attn_outs.append(o)
        attn_lses.append(lse)

    wq_t = w_query.T.astype(BF16)
    sk = sub_keys.reshape(PEER_GROUPS, PEER_NKEYS, PEER_HALF).astype(BF16)
    h_new, hn, sc_t = _mix(h, a.reshape(B, S, POOL_WIDTH), attn_outs, attn_lses,
                           jnp.asarray(_head_expand(), BF16), gates.reshape(B, S, 2 * D),
                           w_pool.astype(BF16), pool_scale.reshape(1, D), w_out.astype(BF16),
                           norm2_g.reshape(1, D), wq_t, sk, tm)
    isel, jsel, gsel = _topk(sc_t)
    gmat = _gbuild(isel, jsel, gsel)
    out = _peer(hn.reshape(T, D), h_new.reshape(T, D), expert_u.astype(BF16), expert_v.astype(BF16),
                gmat, min(1024, T), 1024)
    return out.reshape(B, S, D)


def kernel(x, norm1_g, w_in, q_norm_g, k_norm_g, w_pool, pool_scale, w_out, norm2_g, w_query, sub_keys,
           expert_u, expert_v):
    slopes = _alibi_slopes(N_PATTERNS * ATT_HEADS).reshape(N_PATTERNS, ATT_HEADS)
    biases = [jnp.asarray(_attn_bias(slopes[g], dil)) for g, (_, dil) in enumerate(DILATED_PATTERNS)]
    h = x
    for l in range(norm1_g.shape[0]):
        h = _layer(h, norm1_g[l], w_in[l], q_norm_g[l], k_norm_g[l], w_pool[l], pool_scale[l], w_out[l],
                   norm2_g[l], w_query[l], sub_keys[l], expert_u[l], expert_v[l], biases)
    return h
```

```python
import functools
import math

import numpy as np
import jax
import jax.numpy as jnp
from jax import lax
from jax.experimental import pallas as pl
from jax.experimental.pallas import tpu as pltpu

D_MODEL = 1024
EPS = 1e-6
NEG_INF = -1e30

POOL_WINDOWS = (2, 4, 8, 16)
POOL_GROUPS = 4
POOL_IN = 128
POOL_WIDTH = 512
POOL_OUT = 256
POOL_HALO = 16

ATT_HEADS = 16
HEAD_DIM = 64
DILATED_PATTERNS = ((128, 1), (512, 4), (2048, 16))
N_PATTERNS = 3
BAND = 128
KV_WIDTH = 1024
Q_OFF = POOL_WIDTH
K_OFF = Q_OFF + N_PATTERNS * KV_WIDTH
V_OFF = K_OFF + KV_WIDTH
GATE_OFF = V_OFF + KV_WIDTH
IN_WIDTH = GATE_OFF + 2 * D_MODEL

PEER_HEADS = 8
PEER_NKEYS = 128
PEER_N = PEER_NKEYS * PEER_NKEYS
PEER_HALF = 128
PEER_TOPK = 16
PEER_GROUPS = 2 * PEER_HEADS
PEER_SEL = PEER_HEADS * PEER_TOPK

LANES = 128
SUBLANES = 8
VMEM_LIMIT = 56 * 1024 * 1024

BF16 = jnp.bfloat16
F32 = jnp.float32


def _dot(a, b):
    return jnp.dot(a, b, preferred_element_type=F32)


def _dot_nt(a, b):
    return lax.dot_general(a, b, (((1,), (1,)), ((), ())), preferred_element_type=F32)


def _params(semantics):
    return pltpu.CompilerParams(dimension_semantics=semantics, vmem_limit_bytes=VMEM_LIMIT)


def _const_spec(shape):
    n = len(shape)
    return pl.BlockSpec(shape, lambda *_: (0,) * n)


KV_SLABS = KV_WIDTH // LANES
NORM_COLS = 256
DILATIONS = tuple(d for _, d in DILATED_PATTERNS)


def _store_dilated(y, slab_ref, out_refs):
    tm = y.shape[0]
    for d, ref in out_refs:
        if d == 1:
            ref[...] = y.astype(BF16)
    if all(d == 1 for d, _ in out_refs):
        return
    for s in range(KV_SLABS):
        slab_ref[s] = y[:, s * LANES:(s + 1) * LANES]
    for d, ref in out_refs:
        if d == 1:
            continue
        for r in range(d):
            for s in range(KV_SLABS):
                col = r * KV_WIDTH + s * LANES
                ref[:, col:col + LANES] = slab_ref[s, pl.ds(r, tm // d, stride=d), :].astype(BF16)


def _proj_kernel(x_ref, g1_ref, w_ref, bd_ref, qg_ref, kg_ref,
                 a_ref, q0_ref, q1_ref, q2_ref, k0_ref, k1_ref, k2_ref, v0_ref, v1_ref, v2_ref, gate_ref,
                 slab_ref):
    x = x_ref[...]
    inv = lax.rsqrt(jnp.mean(x * x, axis=-1, keepdims=True) + EPS)
    xn = (x * inv * g1_ref[...]).astype(BF16)

    def head_norm(y, gain):
        parts = []
        for c in range(0, KV_WIDTH, NORM_COLS):
            yc = y[:, c:c + NORM_COLS]
            ss = _dot((yc * yc).astype(BF16), bd_ref[...])
            parts.append(yc * lax.rsqrt(ss * (1.0 / HEAD_DIM) + EPS))
        return jnp.concatenate(parts, axis=1) * gain

    a_ref[...] = _dot(xn, w_ref[:, 0:POOL_WIDTH])
    for g, q_ref in enumerate((q0_ref, q1_ref, q2_ref)):
        lo = Q_OFF + g * KV_WIDTH
        q = head_norm(_dot(xn, w_ref[:, lo:lo + KV_WIDTH]), qg_ref[...])
        _store_dilated(q, slab_ref, [(DILATIONS[g], q_ref)])
    k = head_norm(_dot(xn, w_ref[:, K_OFF:K_OFF + KV_WIDTH]), kg_ref[...])
    _store_dilated(k, slab_ref, list(zip(DILATIONS, (k0_ref, k1_ref, k2_ref))))
    v = _dot(xn, w_ref[:, V_OFF:V_OFF + KV_WIDTH])
    _store_dilated(v, slab_ref, list(zip(DILATIONS, (v0_ref, v1_ref, v2_ref))))
    gate_ref[...] = jax.nn.sigmoid(_dot(xn, w_ref[:, GATE_OFF:IN_WIDTH])).astype(BF16)


def _proj(x2, g1, w_in, bd, qg, kg, tm):
    T = x2.shape[0]
    row = lambda w: pl.BlockSpec((tm, w), lambda i: (i, 0))
    dil = lambda d: pl.BlockSpec((tm // d, d * KV_WIDTH), lambda i: (i, 0))
    dil_shape = lambda d: jax.ShapeDtypeStruct((T // d, d * KV_WIDTH), BF16)
    once = lambda shape: pl.BlockSpec(shape, lambda i: (0, 0), pipeline_mode=pl.Buffered(1))
    views = [dil(d) for d in DILATIONS]
    view_shapes = [dil_shape(d) for d in DILATIONS]
    return pl.pallas_call(
        _proj_kernel,
        grid=(T // tm,),
        in_specs=[row(D_MODEL), once((1, D_MODEL)), once((D_MODEL, IN_WIDTH)),
                  once((NORM_COLS, NORM_COLS)), once((1, KV_WIDTH)), once((1, KV_WIDTH))],
        out_specs=(row(POOL_WIDTH), *views, *views, *views, row(2 * D_MODEL)),
        out_shape=(jax.ShapeDtypeStruct((T, POOL_WIDTH), F32), *view_shapes, *view_shapes, *view_shapes,
                   jax.ShapeDtypeStruct((T, 2 * D_MODEL), BF16)),
        scratch_shapes=[pltpu.VMEM((KV_SLABS, tm, LANES), F32)],
        compiler_params=_params(("arbitrary",)),
        name="proj",
    )(x2, g1, w_in, bd, qg, kg)


def _alibi_slopes(n):
    def geometric(k):
        start = 2.0 ** (-8.0 / k)
        return [start ** (i + 1) for i in range(k)]
    p = 2 ** int(math.floor(math.log2(n)))
    s = geometric(p) + geometric(2 * p)[0::2][: n - p]
    return np.sort(np.array(s, dtype=np.float32))[::-1].copy()


def _attn_bias(slopes, dilation):
    qi = np.arange(BAND)[:, None]
    ki = np.arange(2 * BAND)[None, :]
    step = qi + BAND - ki
    valid = (step >= 0) & (step <= BAND)
    alibi = -slopes[:, None, None] * (step * dilation).astype(np.float32)[None]
    general = np.where(valid[None], alibi, np.float32(NEG_INF)).astype(np.float32)
    first = np.where((ki >= BAND)[None], general, np.float32(NEG_INF)).astype(np.float32)
    return np.stack([general, first])


HEAD_PAIRS = ATT_HEADS // 2


def _attn_kernel(q_ref, kp_ref, kc_ref, vp_ref, vc_ref, bias_ref, o_ref, lse_ref, s_ref, m_ref):
    q_blocks = q_ref.shape[1] // BAND
    first_step = (pl.program_id(2) == 0).astype(jnp.int32)
    lane = lax.broadcasted_iota(jnp.int32, (BAND, LANES), 1)
    low = lane < HEAD_DIM
    low_b = (lax.broadcasted_iota(jnp.int32, (1, LANES), 1) < HEAD_DIM).astype(BF16)
    lane_sel = (low_b, 1 - low_b)

    for blk in range(q_blocks):
        rows = slice(blk * BAND, (blk + 1) * BAND)
        if blk == 0:
            k_prev, v_prev, first = kp_ref, vp_ref, first_step
            before = slice(0, BAND)
        else:
            k_prev, v_prev, first = kc_ref, vc_ref, 0
            before = slice((blk - 1) * BAND, blk * BAND)

        for h in range(ATT_HEADS):
            cols = slice((h // 2) * LANES, (h // 2 + 1) * LANES)
            qm = q_ref[0, rows, cols] * lane_sel[h % 2]
            sp = _dot_nt(qm, k_prev[0, before, cols]) + bias_ref[first, h, :, 0:BAND]
            sc = _dot_nt(qm, kc_ref[0, rows, cols]) + bias_ref[first, h, :, BAND:2 * BAND]
            s_ref[blk, h, :, 0:BAND] = sp
            s_ref[blk, h, :, BAND:2 * BAND] = sc
            m_ref[blk, h] = jnp.broadcast_to(jnp.max(jnp.maximum(sp, sc), axis=-1, keepdims=True), (BAND, LANES))

        lse_tile = jnp.zeros((BAND, LANES), F32)
        for j in range(HEAD_PAIRS):
            cols = slice(j * LANES, (j + 1) * LANES)
            res = []
            for half in range(2):
                h = 2 * j + half
                own, other = lane_sel[half], lane_sel[1 - half]
                pp = jnp.exp(s_ref[blk, h, :, 0:BAND] - m_ref[blk, h]).astype(BF16)
                pc = jnp.exp(s_ref[blk, h, :, BAND:2 * BAND] - m_ref[blk, h]).astype(BF16)
                res.append(_dot(pp, v_prev[0, before, cols] * own + other)
                           + _dot(pc, vc_ref[0, rows, cols] * own + other))
            num = jnp.where(low, res[0], res[1])
            den = pltpu.roll(jnp.where(low, res[1], res[0]), HEAD_DIM, axis=1)
            o_ref[0, rows, cols] = (num / den).astype(BF16)
            lse = jnp.where(low, m_ref[blk, 2 * j], m_ref[blk, 2 * j + 1]) + jnp.log(den)
            lse_tile = jnp.where((lane == j) | (lane == HEAD_DIM + j), lse, lse_tile)
        lse_ref[0, rows, :] = lse_tile


def _attn_pattern(q, k, v, bias, dilation, B):
    d = dilation
    rows = q.shape[0] // B
    q_blocks = 2 if rows % (2 * BAND) == 0 else 1
    qr = q_blocks * BAND
    view = lambda a: a.reshape(B, rows, d * KV_WIDTH)
    cur = lambda w: pl.BlockSpec((1, qr, w), lambda b, r, n: (b, n, r))
    prev = pl.BlockSpec((1, BAND, KV_WIDTH), lambda b, r, n: (b, jnp.maximum(q_blocks * n - 1, 0), r))
    o, lse = pl.pallas_call(
        _attn_kernel,
        grid=(B, d, rows // qr),
        in_specs=[cur(KV_WIDTH), prev, cur(KV_WIDTH), prev, cur(KV_WIDTH),
                  pl.BlockSpec(bias.shape, lambda b, r, n: (0, 0, 0, 0))],
        out_specs=(cur(KV_WIDTH), cur(LANES)),
        out_shape=(jax.ShapeDtypeStruct((B, rows, d * KV_WIDTH), BF16),
                   jax.ShapeDtypeStruct((B, rows, d * LANES), F32)),
        scratch_shapes=[pltpu.VMEM((q_blocks, ATT_HEADS, BAND, 2 * BAND), F32),
                        pltpu.VMEM((q_blocks, ATT_HEADS, BAND, LANES), F32)],
        compiler_params=_params(("arbitrary", "arbitrary", "arbitrary")),
        name=f"attn_d{d}",
    )(view(q), view(k), view(k), view(v), view(v), bias)
    return o, lse


def _head_expand():
    e = np.zeros((LANES, KV_WIDTH), np.float32)
    for h in range(ATT_HEADS):
        e[(h % 2) * HEAD_DIM + h // 2, h * HEAD_DIM:(h + 1) * HEAD_DIM] = 1.0
    return e


def _mix_kernel(x_ref, a_ref, ah_ref, o0_ref, o1_ref, o2_ref, l0_ref, l1_ref, l2_ref, ex_ref, gate_ref,
                wp_ref, ps_ref, wo_ref, g2_ref, wq_ref, sk_ref,
                h_ref, hn_ref, sc_ref, ext_ref, merged_ref, onat_ref, lnat_ref):
    tm = x_ref.shape[1]
    i = pl.program_id(1)

    def natural_rows(src_ref, d, slabs, dst_ref):
        n = src_ref.shape[1]
        for r in range(d):
            for s in range(slabs):
                col = (r * slabs + s) * LANES
                dst_ref[s, pl.ds(r, n, stride=d), :] = src_ref[0, :, col:col + LANES].astype(F32)

    for p, (o_ref, l_ref) in enumerate(((o1_ref, l1_ref), (o2_ref, l2_ref))):
        natural_rows(o_ref, DILATIONS[p + 1], KV_SLABS, onat_ref.at[p])
        natural_rows(l_ref, DILATIONS[p + 1], 1, lnat_ref.at[p])

    def attn_cols(p, g):
        if p == 0:
            return o0_ref[0, :, g * POOL_OUT:(g + 1) * POOL_OUT].astype(F32)
        return jnp.concatenate([onat_ref[p - 1, 2 * g], onat_ref[p - 1, 2 * g + 1]], axis=1)

    lses = [l0_ref[0], lnat_ref[0, 0], lnat_ref[1, 0]]
    top = jnp.maximum(jnp.maximum(lses[0], lses[1]), lses[2])
    es = [jnp.exp(l - top) for l in lses]
    tot = es[0] + es[1] + es[2]
    w_hi, w_lo = [], []
    for e in es:
        w = e / tot
        hi = w.astype(BF16)
        w_hi.append(hi)
        w_lo.append((w - hi.astype(F32)).astype(BF16))
    ext_ref[0:POOL_HALO, :] = jnp.where(i == 0, 0.0, ah_ref[0])
    ext_ref[POOL_HALO:POOL_HALO + tm, :] = a_ref[0]
    t = i * tm + lax.broadcasted_iota(jnp.int32, (tm, 1), 0)
    for g, w in enumerate(POOL_WINDOWS):
        cols = slice(g * POOL_IN, (g + 1) * POOL_IN)
        wsum = ext_ref[POOL_HALO:POOL_HALO + tm, cols]
        for j in range(1, w):
            wsum = wsum + ext_ref[POOL_HALO - j:POOL_HALO - j + tm, cols]
        count = jnp.minimum(t + 1, w).astype(F32)
        dlt = (wsum / count - a_ref[0, :, cols]).astype(BF16)
        ocol = slice(g * POOL_OUT, (g + 1) * POOL_OUT)
        pool = _dot(dlt, wp_ref[g]) * ps_ref[:, ocol]
        attn = jnp.zeros((tm, POOL_OUT), F32)
        for p, (hi, lo) in enumerate(zip(w_hi, w_lo)):
            weight = _dot(hi, ex_ref[:, ocol]) + _dot(lo, ex_ref[:, ocol])
            attn = attn + weight * attn_cols(p, g)
        merged = (gate_ref[0, :, ocol].astype(F32) * pool
                  + gate_ref[0, :, D_MODEL + g * POOL_OUT:D_MODEL + (g + 1) * POOL_OUT].astype(F32) * attn)
        merged_ref[:, ocol] = merged.astype(BF16)
    h = x_ref[0] + _dot(merged_ref[...], wo_ref[...])
    h_ref[0] = h
    hn = (h * lax.rsqrt(jnp.mean(h * h, axis=-1, keepdims=True) + EPS) * g2_ref[...]).astype(BF16)
    hn_ref[0] = hn
    qt = _dot_nt(wq_ref[...], hn).astype(BF16)
    for g in range(PEER_GROUPS):
        head, half = divmod(g, 2)
        rows = slice(g * PEER_NKEYS, (g + 1) * PEER_NKEYS)
        sc = _dot(sk_ref[g], qt[rows, :])
        for j in range(tm // LANES):
            sc_ref[j, pl.ds(half * PEER_HEADS * PEER_NKEYS + head, PEER_NKEYS, stride=PEER_HEADS), :] = (
                sc[:, j * LANES:(j + 1) * LANES])


def _mix(x, a, attn_outs, attn_lses, expand, gates, w_pool, pool_scale, w_out, g2, wq_t, sub_keys, tm):
    B, S, _ = x.shape
    T = B * S
    nt = S // tm
    halo_per_tile = tm // POOL_HALO
    tile = lambda w: pl.BlockSpec((1, tm, w), lambda b, i: (b, i, 0))
    dil = lambda d, w: pl.BlockSpec((1, tm // d, d * w), lambda b, i: (b, i, 0))
    halo = pl.BlockSpec((1, POOL_HALO, POOL_WIDTH), lambda b, i: (b, jnp.maximum(i * halo_per_tile - 1, 0), 0))
    once = lambda shape: pl.BlockSpec(shape, lambda b, i: (0,) * len(shape), pipeline_mode=pl.Buffered(1))
    return pl.pallas_call(
        _mix_kernel,
        grid=(B, nt),
        in_specs=[tile(D_MODEL), tile(POOL_WIDTH), halo,
                  *[dil(d, KV_WIDTH) for d in DILATIONS], *[dil(d, LANES) for d in DILATIONS],
                  once(expand.shape), tile(2 * D_MODEL),
                  once(w_pool.shape), once((1, D_MODEL)), once((D_MODEL, D_MODEL)), once((1, D_MODEL)),
                  once(wq_t.shape), once(sub_keys.shape)],
        out_specs=(tile(D_MODEL), tile(D_MODEL),
                   pl.BlockSpec((tm // LANES, PEER_GROUPS * PEER_NKEYS, LANES), lambda b, i: (b * nt + i, 0, 0))),
        out_shape=(jax.ShapeDtypeStruct((B, S, D_MODEL), F32),
                   jax.ShapeDtypeStruct((B, S, D_MODEL), BF16),
                   jax.ShapeDtypeStruct((T // LANES, PEER_GROUPS * PEER_NKEYS, LANES), F32)),
        scratch_shapes=[pltpu.VMEM((POOL_HALO + tm, POOL_WIDTH), F32),
                        pltpu.VMEM((tm, D_MODEL), BF16),
                        pltpu.VMEM((N_PATTERNS - 1, KV_SLABS, tm, LANES), F32),
                        pltpu.VMEM((N_PATTERNS - 1, 1, tm, LANES), F32)],
        compiler_params=_params(("arbitrary", "arbitrary")),
        name="mix",
    )(x, a, a, *attn_outs, *attn_lses, expand, gates, w_pool, pool_scale, w_out, g2, wq_t, sub_keys)


def _tree(op, xs):
    xs = list(xs)
    while len(xs) > 1:
        nxt = [op(xs[k], xs[k + 1]) for k in range(0, len(xs) - 1, 2)]
        if len(xs) % 2:
            nxt.append(xs[-1])
        xs = nxt
    return xs[0]


_PAIRS = tuple((a, b) for a in range(PEER_TOPK) for b in range(PEER_TOPK) if (a + 1) * (b + 1) <= PEER_TOPK)
_TIE_LAST = 1e9


def _topk_kernel(sc_ref, isel_ref, jsel_ref, g_ref, val_ref, ts_ref, ti_ref, cand_ref, code_ref, e_ref, s_ref):
    half_rows = PEER_HEADS * PEER_NKEYS

    def key_rows(half, n):
        return pl.ds(half * half_rows + n * PEER_HEADS, PEER_HEADS)

    first_max = []
    for half in range(2):
        vals = [sc_ref[0, key_rows(half, n), :] for n in range(PEER_NKEYS)]
        for n, v in enumerate(vals):
            val_ref[key_rows(half, n), :] = v
        first_max.append(_tree(jnp.maximum, vals))

    def stage1(r, maxes):
        nxt = []
        for half in range(2):
            m = maxes[half]
            vals = [val_ref[key_rows(half, n), :] for n in range(PEER_NKEYS)]
            idx = _tree(jnp.minimum, [jnp.where(v == m, float(n), _TIE_LAST) for n, v in enumerate(vals)])
            ts_ref[half, r] = m
            ti_ref[half, r] = idx
            vals = [jnp.where(idx == float(n), -jnp.inf, v) for n, v in enumerate(vals)]
            for n, v in enumerate(vals):
                val_ref[key_rows(half, n), :] = v
            nxt.append(_tree(jnp.maximum, vals))
        return tuple(nxt)

    lax.fori_loop(0, PEER_TOPK, stage1, tuple(first_max))

    cands = []
    for k, (a, b) in enumerate(_PAIRS):
        c = ts_ref[0, a] + ts_ref[1, b]
        cand_ref[k] = c
        code_ref[k] = ti_ref[0, a] * float(PEER_NKEYS) + ti_ref[1, b]
        cands.append(c)
    flat = [float(a * PEER_TOPK + b) for a, b in _PAIRS]

    def stage2(r, m):
        vals = [cand_ref[k] for k in range(len(_PAIRS))]
        first = _tree(jnp.minimum, [jnp.where(v == m, f, _TIE_LAST) for f, v in zip(flat, vals)])
        hits = [first == f for f in flat]
        e_ref[r] = _tree(jnp.maximum, [jnp.where(hh, code_ref[k], -1.0) for k, hh in enumerate(hits)])
        s_ref[r] = m
        vals = [jnp.where(hh, -jnp.inf, v) for hh, v in zip(hits, vals)]
        for k, v in enumerate(vals):
            cand_ref[k] = v
        return _tree(jnp.maximum, vals)

    lax.fori_loop(0, PEER_TOPK, stage2, _tree(jnp.maximum, cands))

    best = s_ref[...]
    ex = jnp.exp(best - best[0:1])
    gate = ex / jnp.sum(ex, axis=0, keepdims=True)
    e = e_ref[...].reshape(PEER_SEL, LANES).T
    i_sel = jnp.floor(e * (1.0 / PEER_NKEYS))
    isel_ref[...] = i_sel
    jsel_ref[...] = e - i_sel * PEER_NKEYS
    g_ref[...] = gate.reshape(PEER_SEL, LANES).T


def _topk(sc_t):
    n_tiles = sc_t.shape[0]
    out = pl.BlockSpec((LANES, PEER_SEL), lambda i: (i, 0))
    shp = jax.ShapeDtypeStruct((n_tiles * LANES, PEER_SEL), F32)
    vreg = (PEER_HEADS, LANES)
    return pl.pallas_call(
        _topk_kernel,
        grid=(n_tiles,),
        in_specs=[pl.BlockSpec((1, PEER_GROUPS * PEER_NKEYS, LANES), lambda i: (i, 0, 0))],
        out_specs=(out, out, out),
        out_shape=(shp, shp, shp),
        scratch_shapes=[pltpu.VMEM((PEER_GROUPS * PEER_NKEYS, LANES), F32),
                        pltpu.VMEM((2, PEER_TOPK) + vreg, F32),
                        pltpu.VMEM((2, PEER_TOPK) + vreg, F32),
                        pltpu.VMEM((len(_PAIRS),) + vreg, F32),
                        pltpu.VMEM((len(_PAIRS),) + vreg, F32),
                        pltpu.VMEM((PEER_TOPK,) + vreg, F32),
                        pltpu.VMEM((PEER_TOPK,) + vreg, F32)],
        compiler_params=_params(("arbitrary",)),
        name="topk",
    )(sc_t)


G_TILE = 128
G_PITCH = G_TILE + SUBLANES


def _gbuild_kernel(isel_ref, jsel_ref, g_ref, out_ref, tile_ref):
    sub = lax.broadcasted_iota(jnp.int32, (PEER_NKEYS, PEER_SEL), 0).astype(F32)

    def body(c, carry):
        irow = isel_ref[pl.ds(c, 1), :]
        jrow = jsel_ref[pl.ds(c, 1), :]
        grow = g_ref[pl.ds(c, 1), :]
        a = jnp.where(sub == irow, grow, 0.0).astype(BF16)
        b = jnp.where(sub == jrow, 1.0, 0.0).astype(BF16)
        tile_ref[pl.ds(c, PEER_NKEYS, stride=G_PITCH), :] = _dot_nt(a, b)
        return carry

    lax.fori_loop(0, G_TILE, body, 0, unroll=True)
    for i in range(PEER_NKEYS):
        out_ref[i] = tile_ref[i * G_PITCH:i * G_PITCH + G_TILE, :].astype(BF16)


def _gbuild(isel, jsel, gates):
    T = isel.shape[0]
    row = pl.BlockSpec((G_TILE, PEER_SEL), lambda t: (t, 0))
    return pl.pallas_call(
        _gbuild_kernel,
        grid=(T // G_TILE,),
        in_specs=[row, row, row],
        out_specs=pl.BlockSpec((PEER_NKEYS, G_TILE, PEER_NKEYS), lambda t: (0, t, 0)),
        out_shape=jax.ShapeDtypeStruct((PEER_NKEYS, T, PEER_NKEYS), BF16),
        scratch_shapes=[pltpu.VMEM((PEER_NKEYS * G_PITCH, PEER_NKEYS), F32)],
        compiler_params=_params(("arbitrary",)),
        name="gbuild",
    )(isel, jsel, gates)


def _gelu_tanh(x):
    return 0.5 * x * (1.0 + jnp.tanh(math.sqrt(2.0 / math.pi) * (x + 0.044715 * (x * x * x))))


def _peer_kernel(hn_ref, u_ref, v_ref, g_ref, h_ref, out_ref, w_ref):
    e = pl.program_id(1)
    n_rows = g_ref.shape[0]

    @pl.when(e == 0)
    def _():
        out_ref[...] = h_ref[...]

    hn = hn_ref[...]
    for i in range(0, n_rows, 2):
        cols = slice(i * PEER_NKEYS, (i + 2) * PEER_NKEYS)
        act = _gelu_tanh(_dot_nt(hn, u_ref[cols, :]))
        gate = jnp.concatenate([g_ref[i], g_ref[i + 1]], axis=1).astype(F32)
        w_ref[:, cols] = (act * gate).astype(BF16)
    out_ref[...] += _dot(w_ref[...], v_ref[...])


def _peer(hn2, h2, u, v, gmat, tt, eb):
    T = hn2.shape[0]
    n_rows = eb // PEER_NKEYS
    tok = lambda dt: pl.BlockSpec((tt, D_MODEL), lambda t, e: (t, 0))
    exp = pl.BlockSpec((eb, D_MODEL), lambda t, e: (e, 0))
    return pl.pallas_call(
        _peer_kernel,
        grid=(T // tt, PEER_N // eb),
        in_specs=[tok(BF16), exp, exp,
                  pl.BlockSpec((n_rows, tt, PEER_NKEYS), lambda t, e: (e, t, 0)),
                  tok(F32)],
        out_specs=pl.BlockSpec((tt, D_MODEL), lambda t, e: (t, 0)),
        out_shape=jax.ShapeDtypeStruct((T, D_MODEL), F32),
        scratch_shapes=[pltpu.VMEM((tt, eb), BF16)],
        compiler_params=_params(("arbitrary", "arbitrary")),
        name="peer",
    )(hn2, u, v, gmat, h2)


def _layer(h, norm1_g, w_in, q_norm_g, k_norm_g, w_pool, pool_scale, w_out, norm2_g,
           w_query, sub_keys, expert_u, expert_v, biases):
    B, S, D = h.shape
    T = B * S
    tm = min(512, S)
    head = np.arange(NORM_COLS) // HEAD_DIM
    bd = jnp.asarray(head[:, None] == head[None, :], BF16)
    qg = (jnp.tile(q_norm_g, ATT_HEADS) * (HEAD_DIM ** -0.5)).reshape(1, KV_WIDTH)
    kg = jnp.tile(k_norm_g, ATT_HEADS).reshape(1, KV_WIDTH)

    res = _proj(h.reshape(T, D), norm1_g.reshape(1, D), w_in.astype(BF16), bd, qg, kg, tm)
    a, qs, ks, vs, gates = res[0], res[1:4], res[4:7], res[7:10], res[10]
    attn_outs, attn_lses = [], []
    for g, dilation in enumerate(DILATIONS):
        o, lse = _attn_pattern(qs[g], ks[g], vs[g], biases[g], dilation, B)
        attn_outs.append(o)
        attn_lses.append(lse)

    wq_t = w_query.T.astype(BF16)
    sk = sub_keys.reshape(PEER_GROUPS, PEER_NKEYS, PEER_HALF).astype(BF16)
    h_new, hn, sc_t = _mix(h, a.reshape(B, S, POOL_WIDTH), attn_outs, attn_lses,
                           jnp.asarray(_head_expand(), BF16), gates.reshape(B, S, 2 * D),
                           w_pool.astype(BF16), pool_scale.reshape(1, D), w_out.astype(BF16),
                           norm2_g.reshape(1, D), wq_t, sk, tm)
    isel, jsel, gsel = _topk(sc_t)
    gmat = _gbuild(isel, jsel, gsel)
    out = _peer(hn.reshape(T, D), h_new.reshape(T, D), expert_u.astype(BF16), expert_v.astype(BF16),
                gmat, min(1024, T), 1024)
    return out.reshape(B, S, D)


def kernel(x, norm1_g, w_in, q_norm_g, k_norm_g, w_pool, pool_scale, w_out, norm2_g, w_query, sub_keys,
           expert_u, expert_v):
    slopes = _alibi_slopes(N_PATTERNS * ATT_HEADS).reshape(N_PATTERNS, ATT_HEADS)
    biases = [jnp.asarray(_attn_bias(slopes[g], dil)) for g, (_, dil) in enumerate(DILATED_PATTERNS)]
    h = x
    for l in range(norm1_g.shape[0]):
        h = _layer(h, norm1_g[l], w_in[l], q_norm_g[l], k_norm_g[l], w_pool[l], pool_scale[l], w_out[l],
                   norm2_g[l], w_query[l], sub_keys[l], expert_u[l], expert_v[l], biases)
    return h
```

```python
import functools
import math

import numpy as np
import jax
import jax.numpy as jnp
from jax import lax
from jax.experimental import pallas as pl
from jax.experimental.pallas import tpu as pltpu

D_MODEL = 1024
EPS = 1e-6
NEG_INF = -1e30

POOL_WINDOWS = (2, 4, 8, 16)
POOL_GROUPS = 4
POOL_IN = 128
POOL_WIDTH = 512
POOL_OUT = 256
POOL_HALO = 16

ATT_HEADS = 16
HEAD_DIM = 64
DILATED_PATTERNS = ((128, 1), (512, 4), (2048, 16))
N_PATTERNS = 3
BAND = 128
KV_WIDTH = 1024
Q_OFF = POOL_WIDTH
K_OFF = Q_OFF + N_PATTERNS * KV_WIDTH
V_OFF = K_OFF + KV_WIDTH
GATE_OFF = V_OFF + KV_WIDTH
IN_WIDTH = GATE_OFF + 2 * D_MODEL

PEER_HEADS = 8
PEER_NKEYS = 128
PEER_N = PEER_NKEYS * PEER_NKEYS
PEER_HALF = 128
PEER_TOPK = 16
PEER_GROUPS = 2 * PEER_HEADS
PEER_SEL = PEER_HEADS * PEER_TOPK

LANES = 128
SUBLANES = 8
VMEM_LIMIT = 56 * 1024 * 1024

BF16 = jnp.bfloat16
F32 = jnp.float32


def _dot(a, b):
    return jnp.dot(a, b, preferred_element_type=F32)


def _dot_nt(a, b):
    return lax.dot_general(a, b, (((1,), (1,)), ((), ())), preferred_element_type=F32)


def _params(semantics):
    return pltpu.CompilerParams(dimension_semantics=semantics, vmem_limit_bytes=VMEM_LIMIT)


def _const_spec(shape):
    n = len(shape)
    return pl.BlockSpec(shape, lambda *_: (0,) * n)


KV_SLABS = KV_WIDTH // LANES
NORM_COLS = 256
DILATIONS = tuple(d for _, d in DILATED_PATTERNS)


def _store_dilated(y, slab_ref, out_refs):
    tm = y.shape[0]
    for d, ref in out_refs:
        if d == 1:
            ref[...] = y.astype(BF16)
    if all(d == 1 for d, _ in out_refs):
        return
    for s in range(KV_SLABS):
        slab_ref[s] = y[:, s * LANES:(s + 1) * LANES]
    for d, ref in out_refs:
        if d == 1:
            continue
        for r in range(d):
            for s in range(KV_SLABS):
                col = r * KV_WIDTH + s * LANES
                ref[:, col:col + LANES] = slab_ref[s, pl.ds(r, tm // d, stride=d), :].astype(BF16)


def _proj_kernel(x_ref, g1_ref, w_ref, bd_ref, qg_ref, kg_ref,
                 a_ref, q0_ref, q1_ref, q2_ref, k0_ref, k1_ref, k2_ref, v0_ref, v1_ref, v2_ref, gate_ref,
                 slab_ref):
    x = x_ref[...]
    inv = lax.rsqrt(jnp.mean(x * x, axis=-1, keepdims=True) + EPS)
    xn = (x * inv * g1_ref[...]).astype(BF16)

    def head_norm(y, gain):
        parts = []
        for c in range(0, KV_WIDTH, NORM_COLS):
            yc = y[:, c:c + NORM_COLS]
            ss = _dot((yc * yc).astype(BF16), bd_ref[...])
            parts.append(yc * lax.rsqrt(ss * (1.0 / HEAD_DIM) + EPS))
        return jnp.concatenate(parts, axis=1) * gain

    a_ref[...] = _dot(xn, w_ref[:, 0:POOL_WIDTH])
    for g, q_ref in enumerate((q0_ref, q1_ref, q2_ref)):
        lo = Q_OFF + g * KV_WIDTH
        q = head_norm(_dot(xn, w_ref[:, lo:lo + KV_WIDTH]), qg_ref[...])
        _store_dilated(q, slab_ref, [(DILATIONS[g], q_ref)])
    k = head_norm(_dot(xn, w_ref[:, K_OFF:K_OFF + KV_WIDTH]), kg_ref[...])
    _store_dilated(k, slab_ref, list(zip(DILATIONS, (k0_ref, k1_ref, k2_ref))))
    v = _dot(xn, w_ref[:, V_OFF:V_OFF + KV_WIDTH])
    _store_dilated(v, slab_ref, list(zip(DILATIONS, (v0_ref, v1_ref, v2_ref))))
    gate_ref[...] = jax.nn.sigmoid(_dot(xn, w_ref[:, GATE_OFF:IN_WIDTH])).astype(BF16)


def _proj(x2, g1, w_in, bd, qg, kg, tm):
    T = x2.shape[0]
    row = lambda w: pl.BlockSpec((tm, w), lambda i: (i, 0))
    dil = lambda d: pl.BlockSpec((tm // d, d * KV_WIDTH), lambda i: (i, 0))
    dil_shape = lambda d: jax.ShapeDtypeStruct((T // d, d * KV_WIDTH), BF16)
    once = lambda shape: pl.BlockSpec(shape, lambda i: (0, 0), pipeline_mode=pl.Buffered(1))
    views = [dil(d) for d in DILATIONS]
    view_shapes = [dil_shape(d) for d in DILATIONS]
    return pl.pallas_call(
        _proj_kernel,
        grid=(T // tm,),
        in_specs=[row(D_MODEL), once((1, D_MODEL)), once((D_MODEL, IN_WIDTH)),
                  once((NORM_COLS, NORM_COLS)), once((1, KV_WIDTH)), once((1, KV_WIDTH))],
        out_specs=(row(POOL_WIDTH), *views, *views, *views, row(2 * D_MODEL)),
        out_shape=(jax.ShapeDtypeStruct((T, POOL_WIDTH), F32), *view_shapes, *view_shapes, *view_shapes,
                   jax.ShapeDtypeStruct((T, 2 * D_MODEL), BF16)),
        scratch_shapes=[pltpu.VMEM((KV_SLABS, tm, LANES), F32)],
        compiler_params=_params(("arbitrary",)),
        name="proj",
    )(x2, g1, w_in, bd, qg, kg)


def _alibi_slopes(n):
    def geometric(k):
        start = 2.0 ** (-8.0 / k)
        return [start ** (i + 1) for i in range(k)]
    p = 2 ** int(math.floor(math.log2(n)))
    s = geometric(p) + geometric(2 * p)[0::2][: n - p]
    return np.sort(np.array(s, dtype=np.float32))[::-1].copy()


def _attn_bias(slopes, dilation):
    qi = np.arange(BAND)[:, None]
    ki = np.arange(2 * BAND)[None, :]
    step = qi + BAND - ki
    valid = (step >= 0) & (step <= BAND)
    alibi = -slopes[:, None, None] * (step * dilation).astype(np.float32)[None]
    general = np.where(valid[None], alibi, np.float32(NEG_INF)).astype(np.float32)
    first = np.where((ki >= BAND)[None], general, np.float32(NEG_INF)).astype(np.float32)
    return np.stack([general, first])


HEAD_PAIRS = ATT_HEADS // 2


def _attn_kernel(q_ref, kp_ref, kc_ref, vp_ref, vc_ref, bias_ref, o_ref, lse_ref, s_ref, m_ref):
    q_blocks = q_ref.shape[1] // BAND
    first_step = (pl.program_id(2) == 0).astype(jnp.int32)
    lane = lax.broadcasted_iota(jnp.int32, (BAND, LANES), 1)
    low = lane < HEAD_DIM
    low_b = (lax.broadcasted_iota(jnp.int32, (1, LANES), 1) < HEAD_DIM).astype(BF16)
    lane_sel = (low_b, 1 - low_b)

    for blk in range(q_blocks):
        rows = slice(blk * BAND, (blk + 1) * BAND)
        if blk == 0:
            k_prev, v_prev, first = kp_ref, vp_ref, first_step
            before = slice(0, BAND)
        else:
            k_prev, v_prev, first = kc_ref, vc_ref, 0
            before = slice((blk - 1) * BAND, blk * BAND)

        for h in range(ATT_HEADS):
            cols = slice((h // 2) * LANES, (h // 2 + 1) * LANES)
            qm = q_ref[0, rows, cols] * lane_sel[h % 2]
            sp = _dot_nt(qm, k_prev[0, before, cols]) + bias_ref[first, h, :, 0:BAND]
            sc = _dot_nt(qm, kc_ref[0, rows, cols]) + bias_ref[first, h, :, BAND:2 * BAND]
            s_ref[blk, h, :, 0:BAND] = sp
            s_ref[blk, h, :, BAND:2 * BAND] = sc
            m_ref[blk, h] = jnp.broadcast_to(jnp.max(jnp.maximum(sp, sc), axis=-1, keepdims=True), (BAND, LANES))

        lse_tile = jnp.zeros((BAND, LANES), F32)
        for j in range(HEAD_PAIRS):
            cols = slice(j * LANES, (j + 1) * LANES)
            res = []
            for half in range(2):
                h = 2 * j + half
                own, other = lane_sel[half], lane_sel[1 - half]
                pp = jnp.exp(s_ref[blk, h, :, 0:BAND] - m_ref[blk, h]).astype(BF16)
                pc = jnp.exp(s_ref[blk, h, :, BAND:2 * BAND] - m_ref[blk, h]).astype(BF16)
                res.append(_dot(pp, v_prev[0, before, cols] * own + other)
                           + _dot(pc, vc_ref[0, rows, cols] * own + other))
            num = jnp.where(low, res[0], res[1])
            den = pltpu.roll(jnp.where(low, res[1], res[0]), HEAD_DIM, axis=1)
            o_ref[0, rows, cols] = (num / den).astype(BF16)
            lse = jnp.where(low, m_ref[blk, 2 * j], m_ref[blk, 2 * j + 1]) + jnp.log(den)
            lse_tile = jnp.where((lane == j) | (lane == HEAD_DIM + j), lse, lse_tile)
        lse_ref[0, rows, :] = lse_tile


def _attn_pattern(q, k, v, bias, dilation, B):
    d = dilation
    rows = q.shape[0] // B
    q_blocks = 2 if rows % (2 * BAND) == 0 else 1
    qr = q_blocks * BAND
    view = lambda a: a.reshape(B, rows, d * KV_WIDTH)
    cur = lambda w: pl.BlockSpec((1, qr, w), lambda b, r, n: (b, n, r))
    prev = pl.BlockSpec((1, BAND, KV_WIDTH), lambda b, r, n: (b, jnp.maximum(q_blocks * n - 1, 0), r))
    o, lse = pl.pallas_call(
        _attn_kernel,
        grid=(B, d, rows // qr),
        in_specs=[cur(KV_WIDTH), prev, cur(KV_WIDTH), prev, cur(KV_WIDTH),
                  pl.BlockSpec(bias.shape, lambda b, r, n: (0, 0, 0, 0))],
        out_specs=(cur(KV_WIDTH), cur(LANES)),
        out_shape=(jax.ShapeDtypeStruct((B, rows, d * KV_WIDTH), BF16),
                   jax.ShapeDtypeStruct((B, rows, d * LANES), F32)),
        scratch_shapes=[pltpu.VMEM((q_blocks, ATT_HEADS, BAND, 2 * BAND), F32),
                        pltpu.VMEM((q_blocks, ATT_HEADS, BAND, LANES), F32)],
        compiler_params=_params(("arbitrary", "arbitrary", "arbitrary")),
        name=f"attn_d{d}",
    )(view(q), view(k), view(k), view(v), view(v), bias)
    return o, lse


def _head_expand():
    e = np.zeros((LANES, KV_WIDTH), np.float32)
    for h in range(ATT_HEADS):
        e[(h % 2) * HEAD_DIM + h // 2, h * HEAD_DIM:(h + 1) * HEAD_DIM] = 1.0
    return e


def _mix_kernel(x_ref, a_ref, ah_ref, o0_ref, o1_ref, o2_ref, l0_ref, l1_ref, l2_ref, ex_ref, gate_ref,
                wp_ref, ps_ref, wo_ref, g2_ref, wq_ref, sk_ref,
                h_ref, hn_ref, sc_ref, ext_ref, merged_ref, onat_ref, lnat_ref):
    tm = x_ref.shape[1]
    i = pl.program_id(1)

    def natural_rows(src_ref, d, slabs, dst_ref):
        n = src_ref.shape[1]
        for r in range(d):
            for s in range(slabs):
                col = (r * slabs + s) * LANES
                dst_ref[s, pl.ds(r, n, stride=d), :] = src_ref[0, :, col:col + LANES].astype(F32)

    for p, (o_ref, l_ref) in enumerate(((o1_ref, l1_ref), (o2_ref, l2_ref))):
        natural_rows(o_ref, DILATIONS[p + 1], KV_SLABS, onat_ref.at[p])
        natural_rows(l_ref, DILATIONS[p + 1], 1, lnat_ref.at[p])

    def attn_cols(p, g):
        if p == 0:
            return o0_ref[0, :, g * POOL_OUT:(g + 1) * POOL_OUT].astype(F32)
        return jnp.concatenate([onat_ref[p - 1, 2 * g], onat_ref[p - 1, 2 * g + 1]], axis=1)

    lses = [l0_ref[0], lnat_ref[0, 0], lnat_ref[1, 0]]
    top = jnp.maximum(jnp.maximum(lses[0], lses[1]), lses[2])
    es = [jnp.exp(l - top) for l in lses]
    tot = es[0] + es[1] + es[2]
    w_hi, w_lo = [], []
    for e in es:
        w = e / tot
        hi = w.astype(BF16)
        w_hi.append(hi)
        w_lo.append((w - hi.astype(F32)).astype(BF16))
    ext_ref[0:POOL_HALO, :] = jnp.where(i == 0, 0.0, ah_ref[0])
    ext_ref[POOL_HALO:POOL_HALO + tm, :] = a_ref[0]
    t = i * tm + lax.broadcasted_iota(jnp.int32, (tm, 1), 0)
    for g, w in enumerate(POOL_WINDOWS):
        cols = slice(g * POOL_IN, (g + 1) * POOL_IN)
        wsum = ext_ref[POOL_HALO:POOL_HALO + tm, cols]
        for j in range(1, w):
            wsum = wsum + ext_ref[POOL_HALO - j:POOL_HALO - j + tm, cols]
        count = jnp.minimum(t + 1, w).astype(F32)
        dlt = (wsum / count - a_ref[0, :, cols]).astype(BF16)
        ocol = slice(g * POOL_OUT, (g + 1) * POOL_OUT)
        pool = _dot(dlt, wp_ref[g]) * ps_ref[:, ocol]
        attn = jnp.zeros((tm, POOL_OUT), F32)
        for p, (hi, lo) in enumerate(zip(w_hi, w_lo)):
            weight = _dot(hi, ex_ref[:, ocol]) + _dot(lo, ex_ref[:, ocol])
            attn = attn + weight * attn_cols(p, g)
        merged = (gate_ref[0, :, ocol].astype(F32) * pool
                  + gate_ref[0, :, D_MODEL + g * POOL_OUT:D_MODEL + (g + 1) * POOL_OUT].astype(F32) * attn)
        merged_ref[:, ocol] = merged.astype(BF16)
    h = x_ref[0] + _dot(merged_ref[...], wo_ref[...])
    h_ref[0] = h
    hn = (h * lax.rsqrt(jnp.mean(h * h, axis=-1, keepdims=True) + EPS) * g2_ref[...]).astype(BF16)
    hn_ref[0] = hn
    qt = _dot_nt(wq_ref[...], hn).astype(BF16)
    for g in range(PEER_GROUPS):
        head, half = divmod(g, 2)
        rows = slice(g * PEER_NKEYS, (g + 1) * PEER_NKEYS)
        sc = _dot(sk_ref[g], qt[rows, :])
        for j in range(tm // LANES):
            sc_ref[j, pl.ds(half * PEER_HEADS * PEER_NKEYS + head, PEER_NKEYS, stride=PEER_HEADS), :] = (
                sc[:, j * LANES:(j + 1) * LANES])


def _mix(x, a, attn_outs, attn_lses, expand, gates, w_pool, pool_scale, w_out, g2, wq_t, sub_keys, tm):
    B, S, _ = x.shape
    T = B * S
    nt = S // tm
    halo_per_tile = tm // POOL_HALO
    tile = lambda w: pl.BlockSpec((1, tm, w), lambda b, i: (b, i, 0))
    dil = lambda d, w: pl.BlockSpec((1, tm // d, d * w), lambda b, i: (b, i, 0))
    halo = pl.BlockSpec((1, POOL_HALO, POOL_WIDTH), lambda b, i: (b, jnp.maximum(i * halo_per_tile - 1, 0), 0))
    once = lambda shape: pl.BlockSpec(shape, lambda b, i: (0,) * len(shape), pipeline_mode=pl.Buffered(1))
    return pl.pallas_call(
        _mix_kernel,
        grid=(B, nt),
        in_specs=[tile(D_MODEL), tile(POOL_WIDTH), halo,
                  *[dil(d, KV_WIDTH) for d in DILATIONS], *[dil(d, LANES) for d in DILATIONS],
                  once(expand.shape), tile(2 * D_MODEL),
                  once(w_pool.shape), once((1, D_MODEL)), once((D_MODEL, D_MODEL)), once((1, D_MODEL)),
                  once(wq_t.shape), once(sub_keys.shape)],
        out_specs=(tile(D_MODEL), tile(D_MODEL),
                   pl.BlockSpec((tm // LANES, PEER_GROUPS * PEER_NKEYS, LANES), lambda b, i: (b * nt + i, 0, 0))),
        out_shape=(jax.ShapeDtypeStruct((B, S, D_MODEL), F32),
                   jax.ShapeDtypeStruct((B, S, D_MODEL), BF16),
                   jax.ShapeDtypeStruct((T // LANES, PEER_GROUPS * PEER_NKEYS, LANES), F32)),
        scratch_shapes=[pltpu.VMEM((POOL_HALO + tm, POOL_WIDTH), F32),
                        pltpu.VMEM((tm, D_MODEL), BF16),
                        pltpu.VMEM((N_PATTERNS - 1, KV_SLABS, tm, LANES), F32),
                        pltpu.VMEM((N_PATTERNS - 1, 1, tm, LANES), F32)],
        compiler_params=_params(("arbitrary", "arbitrary")),
        name="mix",
    )(x, a, a, *attn_outs, *attn_lses, expand, gates, w_pool, pool_scale, w_out, g2, wq_t, sub_keys)


_PAIRS = frozenset((a, b) for a in range(PEER_TOPK) for b in range(PEER_TOPK) if (a + 1) * (b + 1) <= PEER_TOPK)


def _oddeven_merge_sort_pairs(n):
    pairs = []

    def merge(lo, hi, r):
        step = r * 2
        if step < hi - lo:
            merge(lo, hi, step)
            merge(lo + r, hi, step)
            pairs.extend((i, i + r) for i in range(lo + r, hi - r, step))
        else:
            pairs.append((lo, lo + r))

    def sort(lo, hi):
        if hi - lo >= 1:
            mid = lo + (hi - lo) // 2
            sort(lo, mid)
            sort(mid + 1, hi)
            merge(lo, hi, 1)

    sort(0, n - 1)
    return tuple(pairs)


_SORT_PAIRS = _oddeven_merge_sort_pairs(PEER_TOPK)
LISTS_PER_HALF = PEER_NKEYS // PEER_TOPK


def _precedes(va, ia, vb, ib):
    if isinstance(ia, float) and isinstance(ib, float):
        return (va >= vb) if ia < ib else (va > vb)
    return (va > vb) | ((va == vb) & (ia < ib))


def _compare_exchange(a, b):
    (va, ia), (vb, ib) = a, b
    a_first = _precedes(va, ia, vb, ib)
    return ((jnp.maximum(va, vb), jnp.where(a_first, ia, ib)),
            (jnp.minimum(va, vb), jnp.where(a_first, ib, ia)))


def _sort_desc(items):
    items = list(items)
    for i, j in _SORT_PAIRS:
        items[i], items[j] = _compare_exchange(items[i], items[j])
    return items


def _bitonic_sort_desc(items):
    items = list(items)
    n = len(items)
    span = n // 2
    while span:
        for k in range(n):
            if not k & span:
                items[k], items[k + span] = _compare_exchange(items[k], items[k + span])
        span //= 2
    return items


def _merge_keep_top(a, b):
    n = len(a)
    out = []
    for k in range(n):
        if n - 1 - k >= len(b):
            out.append(a[k])
            continue
        (va, ia), (vb, ib) = a[k], b[n - 1 - k]
        out.append((jnp.maximum(va, vb), jnp.where(_precedes(va, ia, vb, ib), ia, ib)))
    return _bitonic_sort_desc(out)


def _topk_kernel(sc_ref, isel_ref, jsel_ref, g_ref, av_ref, ai_ref, bv_ref, bi_ref, ts_ref, ti_ref,
                 e_ref, s_ref):
    def sort_group(g, carry):
        base = pl.multiple_of(g * PEER_TOPK * PEER_HEADS, PEER_TOPK * PEER_HEADS)
        items = [(sc_ref[0, pl.ds(base + k * PEER_HEADS, PEER_HEADS), :], float(k)) for k in range(PEER_TOPK)]
        first_key = ((g % LISTS_PER_HALF) * PEER_TOPK).astype(F32)
        for r, (v, i) in enumerate(_sort_desc(items)):
            av_ref[g, r] = v
            ai_ref[g, r] = i + first_key
        return carry

    lax.fori_loop(0, 2 * LISTS_PER_HALF, sort_group, 0)

    def merge_level(src_v, src_i, dst_v, dst_i, n_out):
        def body(m, carry):
            lists = [[(src_v[2 * m + side, r], src_i[2 * m + side, r]) for r in range(PEER_TOPK)]
                     for side in range(2)]
            for r, (v, i) in enumerate(_merge_keep_top(*lists)):
                dst_v[m, r] = v
                dst_i[m, r] = i
            return carry

        lax.fori_loop(0, n_out, body, 0)

    merge_level(av_ref, ai_ref, bv_ref, bi_ref, LISTS_PER_HALF)
    merge_level(bv_ref, bi_ref, av_ref, ai_ref, LISTS_PER_HALF // 2)
    merge_level(av_ref, ai_ref, ts_ref, ti_ref, 2)

    def cand(a, b):
        expert = ti_ref[0, a] * float(PEER_NKEYS) + ti_ref[1, b]
        return ts_ref[0, a] + ts_ref[1, b], expert + float((a * PEER_TOPK + b) * PEER_N)

    by_row = {a: [cand(a, b) for b in range(PEER_TOPK) if (a, b) in _PAIRS] for a in range(PEER_TOPK)}
    single = [by_row[a][0] for a in range(8, PEER_TOPK)]
    second = _bitonic_sort_desc(by_row[1] + single[::-1])
    third = _sort_desc(by_row[2] + by_row[3] + by_row[4] + by_row[5] + by_row[6])
    third = _merge_keep_top(third, by_row[7])
    best = _merge_keep_top(_merge_keep_top(by_row[0], second), third)
    for r, (v, key) in enumerate(best):
        s_ref[r] = v
        e_ref[r] = key - jnp.floor(key * (1.0 / PEER_N)) * float(PEER_N)

    best = s_ref[...]
    ex = jnp.exp(best - best[0:1])
    gate = ex / jnp.sum(ex, axis=0, keepdims=True)
    e = e_ref[...].reshape(PEER_SEL, LANES).T
    i_sel = jnp.floor(e * (1.0 / PEER_NKEYS))
    isel_ref[...] = i_sel
    jsel_ref[...] = e - i_sel * PEER_NKEYS
    g_ref[...] = gate.reshape(PEER_SEL, LANES).T


def _topk(sc_t):
    n_tiles = sc_t.shape[0]
    out = pl.BlockSpec((LANES, PEER_SEL), lambda i: (i, 0))
    shp = jax.ShapeDtypeStruct((n_tiles * LANES, PEER_SEL), F32)
    vreg = (PEER_HEADS, LANES)
    return pl.pallas_call(
        _topk_kernel,
        grid=(n_tiles,),
        in_specs=[pl.BlockSpec((1, PEER_GROUPS * PEER_NKEYS, LANES), lambda i: (i, 0, 0))],
        out_specs=(out, out, out),
        out_shape=(shp, shp, shp),
        scratch_shapes=[pltpu.VMEM((2 * LISTS_PER_HALF, PEER_TOPK) + vreg, F32),
                        pltpu.VMEM((2 * LISTS_PER_HALF, PEER_TOPK) + vreg, F32),
                        pltpu.VMEM((LISTS_PER_HALF, PEER_TOPK) + vreg, F32),
                        pltpu.VMEM((LISTS_PER_HALF, PEER_TOPK) + vreg, F32),
                        pltpu.VMEM((2, PEER_TOPK) + vreg, F32),
                        pltpu.VMEM((2, PEER_TOPK) + vreg, F32),
                        pltpu.VMEM((PEER_TOPK,) + vreg, F32),
                        pltpu.VMEM((PEER_TOPK,) + vreg, F32)],
        compiler_params=_params(("arbitrary",)),
        name="topk",
    )(sc_t)


G_TILE = 128
G_PITCH = G_TILE + SUBLANES


def _gbuild_kernel(isel_ref, jsel_ref, g_ref, out_ref, tile_ref):
    sub = lax.broadcasted_iota(jnp.int32, (PEER_NKEYS, PEER_SEL), 0).astype(F32)

    def body(c, carry):
        irow = isel_ref[pl.ds(c, 1), :]
        jrow = jsel_ref[pl.ds(c, 1), :]
        grow = g_ref[pl.ds(c, 1), :]
        a = jnp.where(sub == irow, grow, 0.0).astype(BF16)
        b = jnp.where(sub == jrow, 1.0, 0.0).astype(BF16)
        tile_ref[pl.ds(c, PEER_NKEYS, stride=G_PITCH), :] = _dot_nt(a, b)
        return carry

    lax.fori_loop(0, G_TILE, body, 0, unroll=True)
    for i in range(PEER_NKEYS):
        out_ref[i] = tile_ref[i * G_PITCH:i * G_PITCH + G_TILE, :].astype(BF16)


def _gbuild(isel, jsel, gates):
    T = isel.shape[0]
    row = pl.BlockSpec((G_TILE, PEER_SEL), lambda t: (t, 0))
    return pl.pallas_call(
        _gbuild_kernel,
        grid=(T // G_TILE,),
        in_specs=[row, row, row],
        out_specs=pl.BlockSpec((PEER_NKEYS, G_TILE, PEER_NKEYS), lambda t: (0, t, 0)),
        out_shape=jax.ShapeDtypeStruct((PEER_NKEYS, T, PEER_NKEYS), BF16),
        scratch_shapes=[pltpu.VMEM((PEER_NKEYS * G_PITCH, PEER_NKEYS), F32)],
        compiler_params=_params(("arbitrary",)),
        name="gbuild",
    )(isel, jsel, gates)


def _gelu_tanh(x):
    return 0.5 * x * (1.0 + jnp.tanh(math.sqrt(2.0 / math.pi) * (x + 0.044715 * (x * x * x))))


def _peer_kernel(hn_ref, u_ref, v_ref, g_ref, h_ref, out_ref, w_ref):
    e = pl.program_id(1)
    n_rows = g_ref.shape[0]

    @pl.when(e == 0)
    def _():
        out_ref[...] = h_ref[...]

    hn = hn_ref[...]
    for i in range(0, n_rows, 2):
        cols = slice(i * PEER_NKEYS, (i + 2) * PEER_NKEYS)
        act = _gelu_tanh(_dot_nt(hn, u_ref[cols, :]))
        gate = jnp.concatenate([g_ref[i], g_ref[i + 1]], axis=1).astype(F32)
        w_ref[:, cols] = (act * gate).astype(BF16)
    out_ref[...] += _dot(w_ref[...], v_ref[...])


def _peer(hn2, h2, u, v, gmat, tt, eb):
    T = hn2.shape[0]
    n_rows = eb // PEER_NKEYS
    tok = lambda dt: pl.BlockSpec((tt, D_MODEL), lambda t, e: (t, 0))
    exp = pl.BlockSpec((eb, D_MODEL), lambda t, e: (e, 0))
    return pl.pallas_call(
        _peer_kernel,
        grid=(T // tt, PEER_N // eb),
        in_specs=[tok(BF16), exp, exp,
                  pl.BlockSpec((n_rows, tt, PEER_NKEYS), lambda t, e: (e, t, 0)),
                  tok(F32)],
        out_specs=pl.BlockSpec((tt, D_MODEL), lambda t, e: (t, 0)),
        out_shape=jax.ShapeDtypeStruct((T, D_MODEL), F32),
        scratch_shapes=[pltpu.VMEM((tt, eb), BF16)],
        compiler_params=_params(("arbitrary", "arbitrary")),
        name="peer",
    )(hn2, u, v, gmat, h2)


def _layer(h, norm1_g, w_in, q_norm_g, k_norm_g, w_pool, pool_scale, w_out, norm2_g,
           w_query, sub_keys, expert_u, expert_v, biases):
    B, S, D = h.shape
    T = B * S
    tm = min(512, S)
    head = np.arange(NORM_COLS) // HEAD_DIM
    bd = jnp.asarray(head[:, None] == head[None, :], BF16)
    qg = (jnp.tile(q_norm_g, ATT_HEADS) * (HEAD_DIM ** -0.5)).reshape(1, KV_WIDTH)
    kg = jnp.tile(k_norm_g, ATT_HEADS).reshape(1, KV_WIDTH)

    res = _proj(h.reshape(T, D), norm1_g.reshape(1, D), w_in.astype(BF16), bd, qg, kg, tm)
    a, qs, ks, vs, gates = res[0], res[1:4], res[4:7], res[7:10], res[10]
    attn_outs, attn_lses = [], []
    for g, dilation in enumerate(DILATIONS):
        o, lse = _attn_pattern(qs[g], ks[g], vs[g], biases[g], dilation, B)
        attn_outs.append(o)
        attn_lses.append(lse)

    wq_t = w_query.T.astype(BF16)
    sk = sub_keys.reshape(PEER_GROUPS, PEER_NKEYS, PEER_HALF).astype(BF16)
    h_new, hn, sc_t = _mix(h, a.reshape(B, S, POOL_WIDTH), attn_outs, attn_lses,
                           jnp.asarray(_head_expand(), BF16), gates.reshape(B, S, 2 * D),
                           w_pool.astype(BF16), pool_scale.reshape(1, D), w_out.astype(BF16),
                           norm2_g.reshape(1, D), wq_t, sk, tm)
    isel, jsel, gsel = _topk(sc_t)
    gmat = _gbuild(isel, jsel, gsel)
    out = _peer(hn.reshape(T, D), h_new.reshape(T, D), expert_u.astype(BF16), expert_v.astype(BF16),
                gmat, min(1024, T), 1024)
    return out.reshape(B, S, D)


def kernel(x, norm1_g, w_in, q_norm_g, k_norm_g, w_pool, pool_scale, w_out, norm2_g, w_query, sub_keys,
           expert_u, expert_v):
    slopes = _alibi_slopes(N_PATTERNS * ATT_HEADS).reshape(N_PATTERNS, ATT_HEADS)
    biases = [jnp.asarray(_attn_bias(slopes[g], dil)) for g, (_, dil) in enumerate(DILATED_PATTERNS)]
    h = x
    for l in range(norm1_g.shape[0]):
        h = _layer(h, norm1_g[l], w_in[l], q_norm_g[l], k_norm_g[l], w_pool[l], pool_scale[l], w_out[l],
                   norm2_g[l], w_query[l], sub_keys[l], expert_u[l], expert_v[l], biases)
    return h
```

```python
import functools
import math

import numpy as np
import jax
import jax.numpy as jnp
from jax import lax
from jax.experimental import pallas as pl
from jax.experimental.pallas import tpu as pltpu

D_MODEL = 1024
EPS = 1e-6
NEG_INF = -1e30

POOL_WINDOWS = (2, 4, 8, 16)
POOL_GROUPS = 4
POOL_IN = 128
POOL_WIDTH = 512
POOL_OUT = 256
POOL_HALO = 16

ATT_HEADS = 16
HEAD_DIM = 64
DILATED_PATTERNS = ((128, 1), (512, 4), (2048, 16))
N_PATTERNS = 3
BAND = 128
KV_WIDTH = 1024
Q_OFF = POOL_WIDTH
K_OFF = Q_OFF + N_PATTERNS * KV_WIDTH
V_OFF = K_OFF + KV_WIDTH
GATE_OFF = V_OFF + KV_WIDTH
IN_WIDTH = GATE_OFF + 2 * D_MODEL

PEER_HEADS = 8
PEER_NKEYS = 128
PEER_N = PEER_NKEYS * PEER_NKEYS
PEER_HALF = 128
PEER_TOPK = 16
PEER_GROUPS = 2 * PEER_HEADS
PEER_SEL = PEER_HEADS * PEER_TOPK

LANES = 128
SUBLANES = 8
VMEM_LIMIT = 56 * 1024 * 1024

BF16 = jnp.bfloat16
F32 = jnp.float32


def _dot(a, b):
    return jnp.dot(a, b, preferred_element_type=F32)


def _dot_nt(a, b):
    return lax.dot_general(a, b, (((1,), (1,)), ((), ())), preferred_element_type=F32)


def _params(semantics):
    return pltpu.CompilerParams(dimension_semantics=semantics, vmem_limit_bytes=VMEM_LIMIT)


def _const_spec(shape):
    n = len(shape)
    return pl.BlockSpec(shape, lambda *_: (0,) * n)


KV_SLABS = KV_WIDTH // LANES
NORM_COLS = 256
DILATIONS = tuple(d for _, d in DILATED_PATTERNS)


def _store_dilated(y, slab_ref, out_refs):
    tm = y.shape[0]
    for d, ref in out_refs:
        if d == 1:
            ref[...] = y.astype(BF16)
    if all(d == 1 for d, _ in out_refs):
        return
    for s in range(KV_SLABS):
        slab_ref[s] = y[:, s * LANES:(s + 1) * LANES]
    for d, ref in out_refs:
        if d == 1:
            continue
        for r in range(d):
            for s in range(KV_SLABS):
                col = r * KV_WIDTH + s * LANES
                ref[:, col:col + LANES] = slab_ref[s, pl.ds(r, tm // d, stride=d), :].astype(BF16)


def _proj_kernel(x_ref, g1_ref, w_ref, bd_ref, qg_ref, kg_ref,
                 a_ref, q0_ref, q1_ref, q2_ref, k0_ref, k1_ref, k2_ref, v0_ref, v1_ref, v2_ref, gate_ref,
                 slab_ref):
    x = x_ref[...]
    inv = lax.rsqrt(jnp.mean(x * x, axis=-1, keepdims=True) + EPS)
    xn = (x * inv * g1_ref[...]).astype(BF16)

    def head_norm(y, gain):
        parts = []
        for c in range(0, KV_WIDTH, NORM_COLS):
            yc = y[:, c:c + NORM_COLS]
            ss = _dot((yc * yc).astype(BF16), bd_ref[...])
            parts.append(yc * lax.rsqrt(ss * (1.0 / HEAD_DIM) + EPS))
        return jnp.concatenate(parts, axis=1) * gain

    a_ref[...] = _dot(xn, w_ref[:, 0:POOL_WIDTH])
    for g, q_ref in enumerate((q0_ref, q1_ref, q2_ref)):
        lo = Q_OFF + g * KV_WIDTH
        q = head_norm(_dot(xn, w_ref[:, lo:lo + KV_WIDTH]), qg_ref[...])
        _store_dilated(q, slab_ref, [(DILATIONS[g], q_ref)])
    k = head_norm(_dot(xn, w_ref[:, K_OFF:K_OFF + KV_WIDTH]), kg_ref[...])
    _store_dilated(k, slab_ref, list(zip(DILATIONS, (k0_ref, k1_ref, k2_ref))))
    v = _dot(xn, w_ref[:, V_OFF:V_OFF + KV_WIDTH])
    _store_dilated(v, slab_ref, list(zip(DILATIONS, (v0_ref, v1_ref, v2_ref))))
    gate_ref[...] = jax.nn.sigmoid(_dot(xn, w_ref[:, GATE_OFF:IN_WIDTH])).astype(BF16)


def _proj(x2, g1, w_in, bd, qg, kg, tm):
    T = x2.shape[0]
    row = lambda w: pl.BlockSpec((tm, w), lambda i: (i, 0))
    dil = lambda d: pl.BlockSpec((tm // d, d * KV_WIDTH), lambda i: (i, 0))
    dil_shape = lambda d: jax.ShapeDtypeStruct((T // d, d * KV_WIDTH), BF16)
    once = lambda shape: pl.BlockSpec(shape, lambda i: (0, 0), pipeline_mode=pl.Buffered(1))
    views = [dil(d) for d in DILATIONS]
    view_shapes = [dil_shape(d) for d in DILATIONS]
    return pl.pallas_call(
        _proj_kernel,
        grid=(T // tm,),
        in_specs=[row(D_MODEL), once((1, D_MODEL)), once((D_MODEL, IN_WIDTH)),
                  once((NORM_COLS, NORM_COLS)), once((1, KV_WIDTH)), once((1, KV_WIDTH))],
        out_specs=(row(POOL_WIDTH), *views, *views, *views, row(2 * D_MODEL)),
        out_shape=(jax.ShapeDtypeStruct((T, POOL_WIDTH), F32), *view_shapes, *view_shapes, *view_shapes,
                   jax.ShapeDtypeStruct((T, 2 * D_MODEL), BF16)),
        scratch_shapes=[pltpu.VMEM((KV_SLABS, tm, LANES), F32)],
        compiler_params=_params(("arbitrary",)),
        name="proj",
    )(x2, g1, w_in, bd, qg, kg)


def _alibi_slopes(n):
    def geometric(k):
        start = 2.0 ** (-8.0 / k)
        return [start ** (i + 1) for i in range(k)]
    p = 2 ** int(math.floor(math.log2(n)))
    s = geometric(p) + geometric(2 * p)[0::2][: n - p]
    return np.sort(np.array(s, dtype=np.float32))[::-1].copy()


def _attn_bias(slopes, dilation):
    qi = np.arange(BAND)[:, None]
    ki = np.arange(2 * BAND)[None, :]
    step = qi + BAND - ki
    valid = (step >= 0) & (step <= BAND)
    alibi = -slopes[:, None, None] * (step * dilation).astype(np.float32)[None]
    general = np.where(valid[None], alibi, np.float32(NEG_INF)).astype(np.float32)
    first = np.where((ki >= BAND)[None], general, np.float32(NEG_INF)).astype(np.float32)
    return np.stack([general, first])


HEAD_PAIRS = ATT_HEADS // 2


def _attn_kernel(q_ref, kp_ref, kc_ref, vp_ref, vc_ref, bias_ref, o_ref, lse_ref, s_ref, m_ref):
    q_blocks = q_ref.shape[1] // BAND
    first_step = (pl.program_id(2) == 0).astype(jnp.int32)
    lane = lax.broadcasted_iota(jnp.int32, (BAND, LANES), 1)
    low = lane < HEAD_DIM
    low_b = (lax.broadcasted_iota(jnp.int32, (1, LANES), 1) < HEAD_DIM).astype(BF16)
    lane_sel = (low_b, 1 - low_b)

    for blk in range(q_blocks):
        rows = slice(blk * BAND, (blk + 1) * BAND)
        if blk == 0:
            k_prev, v_prev, first = kp_ref, vp_ref, first_step
            before = slice(0, BAND)
        else:
            k_prev, v_prev, first = kc_ref, vc_ref, 0
            before = slice((blk - 1) * BAND, blk * BAND)

        for h in range(ATT_HEADS):
            cols = slice((h // 2) * LANES, (h // 2 + 1) * LANES)
            qm = q_ref[0, rows, cols] * lane_sel[h % 2]
            sp = _dot_nt(qm, k_prev[0, before, cols]) + bias_ref[first, h, :, 0:BAND]
            sc = _dot_nt(qm, kc_ref[0, rows, cols]) + bias_ref[first, h, :, BAND:2 * BAND]
            s_ref[blk, h, :, 0:BAND] = sp
            s_ref[blk, h, :, BAND:2 * BAND] = sc
            m_ref[blk, h] = jnp.broadcast_to(jnp.max(jnp.maximum(sp, sc), axis=-1, keepdims=True), (BAND, LANES))

        lse_tile = jnp.zeros((BAND, LANES), F32)
        for j in range(HEAD_PAIRS):
            cols = slice(j * LANES, (j + 1) * LANES)
            res = []
            for half in range(2):
                h = 2 * j + half
                own, other = lane_sel[half], lane_sel[1 - half]
                pp = jnp.exp(s_ref[blk, h, :, 0:BAND] - m_ref[blk, h]).astype(BF16)
                pc = jnp.exp(s_ref[blk, h, :, BAND:2 * BAND] - m_ref[blk, h]).astype(BF16)
                res.append(_dot(pp, v_prev[0, before, cols] * own + other)
                           + _dot(pc, vc_ref[0, rows, cols] * own + other))
            num = jnp.where(low, res[0], res[1])
            den = pltpu.roll(jnp.where(low, res[1], res[0]), HEAD_DIM, axis=1)
            o_ref[0, rows, cols] = (num / den).astype(BF16)
            lse = jnp.where(low, m_ref[blk, 2 * j], m_ref[blk, 2 * j + 1]) + jnp.log(den)
            lse_tile = jnp.where((lane == j) | (lane == HEAD_DIM + j), lse, lse_tile)
        lse_ref[0, rows, :] = lse_tile


def _attn_pattern(q, k, v, bias, dilation, B):
    d = dilation
    rows = q.shape[0] // B
    q_blocks = 2 if rows % (2 * BAND) == 0 else 1
    qr = q_blocks * BAND
    view = lambda a: a.reshape(B, rows, d * KV_WIDTH)
    cur = lambda w: pl.BlockSpec((1, qr, w), lambda b, r, n: (b, n, r))
    prev = pl.BlockSpec((1, BAND, KV_WIDTH), lambda b, r, n: (b, jnp.maximum(q_blocks * n - 1, 0), r))
    o, lse = pl.pallas_call(
        _attn_kernel,
        grid=(B, d, rows // qr),
        in_specs=[cur(KV_WIDTH), prev, cur(KV_WIDTH), prev, cur(KV_WIDTH),
                  pl.BlockSpec(bias.shape, lambda b, r, n: (0, 0, 0, 0))],
        out_specs=(cur(KV_WIDTH), cur(LANES)),
        out_shape=(jax.ShapeDtypeStruct((B, rows, d * KV_WIDTH), BF16),
                   jax.ShapeDtypeStruct((B, rows, d * LANES), F32)),
        scratch_shapes=[pltpu.VMEM((q_blocks, ATT_HEADS, BAND, 2 * BAND), F32),
                        pltpu.VMEM((q_blocks, ATT_HEADS, BAND, LANES), F32)],
        compiler_params=_params(("arbitrary", "arbitrary", "arbitrary")),
        name=f"attn_d{d}",
    )(view(q), view(k), view(k), view(v), view(v), bias)
    return o, lse


def _head_expand():
    e = np.zeros((LANES, KV_WIDTH), np.float32)
    for h in range(ATT_HEADS):
        e[(h % 2) * HEAD_DIM + h // 2, h * HEAD_DIM:(h + 1) * HEAD_DIM] = 1.0
    return e


def _mix_kernel(x_ref, a_ref, ah_ref, o0_ref, o1_ref, o2_ref, l0_ref, l1_ref, l2_ref, ex_ref, gate_ref,
                wp_ref, ps_ref, wo_ref, g2_ref, wq_ref, sk_ref,
                h_ref, hn_ref, sc_ref, ext_ref, merged_ref, onat_ref, lnat_ref):
    tm = x_ref.shape[1]
    i = pl.program_id(1)

    def natural_rows(src_ref, d, slabs, dst_ref):
        n = src_ref.shape[1]
        for r in range(d):
            for s in range(slabs):
                col = (r * slabs + s) * LANES
                dst_ref[s, pl.ds(r, n, stride=d), :] = src_ref[0, :, col:col + LANES].astype(F32)

    for p, (o_ref, l_ref) in enumerate(((o1_ref, l1_ref), (o2_ref, l2_ref))):
        natural_rows(o_ref, DILATIONS[p + 1], KV_SLABS, onat_ref.at[p])
        natural_rows(l_ref, DILATIONS[p + 1], 1, lnat_ref.at[p])

    def attn_cols(p, g):
        if p == 0:
            return o0_ref[0, :, g * POOL_OUT:(g + 1) * POOL_OUT].astype(F32)
        return jnp.concatenate([onat_ref[p - 1, 2 * g], onat_ref[p - 1, 2 * g + 1]], axis=1)

    lses = [l0_ref[0], lnat_ref[0, 0], lnat_ref[1, 0]]
    top = jnp.maximum(jnp.maximum(lses[0], lses[1]), lses[2])
    es = [jnp.exp(l - top) for l in lses]
    tot = es[0] + es[1] + es[2]
    w_hi, w_lo = [], []
    for e in es:
        w = e / tot
        hi = w.astype(BF16)
        w_hi.append(hi)
        w_lo.append((w - hi.astype(F32)).astype(BF16))
    ext_ref[0:POOL_HALO, :] = jnp.where(i == 0, 0.0, ah_ref[0])
    ext_ref[POOL_HALO:POOL_HALO + tm, :] = a_ref[0]
    t = i * tm + lax.broadcasted_iota(jnp.int32, (tm, 1), 0)
    for g, w in enumerate(POOL_WINDOWS):
        cols = slice(g * POOL_IN, (g + 1) * POOL_IN)
        wsum = ext_ref[POOL_HALO:POOL_HALO + tm, cols]
        for j in range(1, w):
            wsum = wsum + ext_ref[POOL_HALO - j:POOL_HALO - j + tm, cols]
        count = jnp.minimum(t + 1, w).astype(F32)
        dlt = (wsum / count - a_ref[0, :, cols]).astype(BF16)
        ocol = slice(g * POOL_OUT, (g + 1) * POOL_OUT)
        pool = _dot(dlt, wp_ref[g]) * ps_ref[:, ocol]
        attn = jnp.zeros((tm, POOL_OUT), F32)
        for p, (hi, lo) in enumerate(zip(w_hi, w_lo)):
            weight = _dot(hi, ex_ref[:, ocol]) + _dot(lo, ex_ref[:, ocol])
            attn = attn + weight * attn_cols(p, g)
        merged = (gate_ref[0, :, ocol].astype(F32) * pool
                  + gate_ref[0, :, D_MODEL + g * POOL_OUT:D_MODEL + (g + 1) * POOL_OUT].astype(F32) * attn)
        merged_ref[:, ocol] = merged.astype(BF16)
    h = x_ref[0] + _dot(merged_ref[...], wo_ref[...])
    h_ref[0] = h
    hn = (h * lax.rsqrt(jnp.mean(h * h, axis=-1, keepdims=True) + EPS) * g2_ref[...]).astype(BF16)
    hn_ref[0] = hn
    qt = _dot_nt(wq_ref[...], hn).astype(BF16)
    for g in range(PEER_GROUPS):
        head, half = divmod(g, 2)
        rows = slice(g * PEER_NKEYS, (g + 1) * PEER_NKEYS)
        sc = _dot(sk_ref[g], qt[rows, :])
        for j in range(tm // LANES):
            sc_ref[j, pl.ds(half * PEER_HEADS * PEER_NKEYS + head, PEER_NKEYS, stride=PEER_HEADS), :] = (
                sc[:, j * LANES:(j + 1) * LANES])


def _mix(x, a, attn_outs, attn_lses, expand, gates, w_pool, pool_scale, w_out, g2, wq_t, sub_keys, tm):
    B, S, _ = x.shape
    T = B * S
    nt = S // tm
    halo_per_tile = tm // POOL_HALO
    tile = lambda w: pl.BlockSpec((1, tm, w), lambda b, i: (b, i, 0))
    dil = lambda d, w: pl.BlockSpec((1, tm // d, d * w), lambda b, i: (b, i, 0))
    halo = pl.BlockSpec((1, POOL_HALO, POOL_WIDTH), lambda b, i: (b, jnp.maximum(i * halo_per_tile - 1, 0), 0))
    once = lambda shape: pl.BlockSpec(shape, lambda b, i: (0,) * len(shape), pipeline_mode=pl.Buffered(1))
    return pl.pallas_call(
        _mix_kernel,
        grid=(B, nt),
        in_specs=[tile(D_MODEL), tile(POOL_WIDTH), halo,
                  *[dil(d, KV_WIDTH) for d in DILATIONS], *[dil(d, LANES) for d in DILATIONS],
                  once(expand.shape), tile(2 * D_MODEL),
                  once(w_pool.shape), once((1, D_MODEL)), once((D_MODEL, D_MODEL)), once((1, D_MODEL)),
                  once(wq_t.shape), once(sub_keys.shape)],
        out_specs=(tile(D_MODEL), tile(D_MODEL),
                   pl.BlockSpec((tm // LANES, PEER_GROUPS * PEER_NKEYS, LANES), lambda b, i: (b * nt + i, 0, 0))),
        out_shape=(jax.ShapeDtypeStruct((B, S, D_MODEL), F32),
                   jax.ShapeDtypeStruct((B, S, D_MODEL), BF16),
                   jax.ShapeDtypeStruct((T // LANES, PEER_GROUPS * PEER_NKEYS, LANES), F32)),
        scratch_shapes=[pltpu.VMEM((POOL_HALO + tm, POOL_WIDTH), F32),
                        pltpu.VMEM((tm, D_MODEL), BF16),
                        pltpu.VMEM((N_PATTERNS - 1, KV_SLABS, tm, LANES), F32),
                        pltpu.VMEM((N_PATTERNS - 1, 1, tm, LANES), F32)],
        compiler_params=_params(("arbitrary", "arbitrary")),
        name="mix",
    )(x, a, a, *attn_outs, *attn_lses, expand, gates, w_pool, pool_scale, w_out, g2, wq_t, sub_keys)


_PAIRS = frozenset((a, b) for a in range(PEER_TOPK) for b in range(PEER_TOPK) if (a + 1) * (b + 1) <= PEER_TOPK)


def _oddeven_merge_sort_pairs(n):
    pairs = []

    def merge(lo, hi, r):
        step = r * 2
        if step < hi - lo:
            merge(lo, hi, step)
            merge(lo + r, hi, step)
            pairs.extend((i, i + r) for i in range(lo + r, hi - r, step))
        else:
            pairs.append((lo, lo + r))

    def sort(lo, hi):
        if hi - lo >= 1:
            mid = lo + (hi - lo) // 2
            sort(lo, mid)
            sort(mid + 1, hi)
            merge(lo, hi, 1)

    sort(0, n - 1)
    return tuple(pairs)


_SORT_PAIRS = _oddeven_merge_sort_pairs(PEER_TOPK)
LISTS_PER_HALF = PEER_NKEYS // PEER_TOPK


def _precedes(va, ia, vb, ib):
    if isinstance(ia, float) and isinstance(ib, float):
        return (va >= vb) if ia < ib else (va > vb)
    return (va > vb) | ((va == vb) & (ia < ib))


def _compare_exchange(a, b):
    (va, ia), (vb, ib) = a, b
    a_first = _precedes(va, ia, vb, ib)
    return ((jnp.maximum(va, vb), jnp.where(a_first, ia, ib)),
            (jnp.minimum(va, vb), jnp.where(a_first, ib, ia)))


def _sort_desc(items):
    items = list(items)
    for i, j in _SORT_PAIRS:
        items[i], items[j] = _compare_exchange(items[i], items[j])
    return items


def _bitonic_sort_desc(items):
    items = list(items)
    n = len(items)
    span = n // 2
    while span:
        for k in range(n):
            if not k & span:
                items[k], items[k + span] = _compare_exchange(items[k], items[k + span])
        span //= 2
    return items


def _merge_keep_top(a, b):
    n = len(a)
    out = []
    for k in range(n):
        if n - 1 - k >= len(b):
            out.append(a[k])
            continue
        (va, ia), (vb, ib) = a[k], b[n - 1 - k]
        out.append((jnp.maximum(va, vb), jnp.where(_precedes(va, ia, vb, ib), ia, ib)))
    return _bitonic_sort_desc(out)


def _topk_kernel(sc_ref, isel_ref, jsel_ref, g_ref, av_ref, ai_ref, bv_ref, bi_ref, ts_ref, ti_ref,
                 e_ref, s_ref):
    def sort_group(g, carry):
        base = pl.multiple_of(g * PEER_TOPK * PEER_HEADS, PEER_TOPK * PEER_HEADS)
        items = [(sc_ref[0, pl.ds(base + k * PEER_HEADS, PEER_HEADS), :], float(k)) for k in range(PEER_TOPK)]
        first_key = ((g % LISTS_PER_HALF) * PEER_TOPK).astype(F32)
        for r, (v, i) in enumerate(_sort_desc(items)):
            av_ref[g, r] = v
            ai_ref[g, r] = i + first_key
        return carry

    lax.fori_loop(0, 2 * LISTS_PER_HALF, sort_group, 0)

    def merge_level(src_v, src_i, dst_v, dst_i, n_out):
        def body(m, carry):
            lists = [[(src_v[2 * m + side, r], src_i[2 * m + side, r]) for r in range(PEER_TOPK)]
                     for side in range(2)]
            for r, (v, i) in enumerate(_merge_keep_top(*lists)):
                dst_v[m, r] = v
                dst_i[m, r] = i
            return carry

        lax.fori_loop(0, n_out, body, 0)

    merge_level(av_ref, ai_ref, bv_ref, bi_ref, LISTS_PER_HALF)
    merge_level(bv_ref, bi_ref, av_ref, ai_ref, LISTS_PER_HALF // 2)
    merge_level(av_ref, ai_ref, ts_ref, ti_ref, 2)

    def cand(a, b):
        expert = ti_ref[0, a] * float(PEER_NKEYS) + ti_ref[1, b]
        return ts_ref[0, a] + ts_ref[1, b], expert + float((a * PEER_TOPK + b) * PEER_N)

    by_row = {a: [cand(a, b) for b in range(PEER_TOPK) if (a, b) in _PAIRS] for a in range(PEER_TOPK)}
    single = [by_row[a][0] for a in range(8, PEER_TOPK)]
    second = _bitonic_sort_desc(by_row[1] + single[::-1])
    third = _sort_desc(by_row[2] + by_row[3] + by_row[4] + by_row[5] + by_row[6])
    third = _merge_keep_top(third, by_row[7])
    best = _merge_keep_top(_merge_keep_top(by_row[0], second), third)
    for r, (v, key) in enumerate(best):
        s_ref[r] = v
        e_ref[r] = key - jnp.floor(key * (1.0 / PEER_N)) * float(PEER_N)

    best = s_ref[...]
    ex = jnp.exp(best - best[0:1])
    gate = ex / jnp.sum(ex, axis=0, keepdims=True)
    e = e_ref[...].reshape(PEER_SEL, LANES).T
    i_sel = jnp.floor(e * (1.0 / PEER_NKEYS))
    isel_ref[...] = i_sel
    jsel_ref[...] = e - i_sel * PEER_NKEYS
    g_ref[...] = gate.reshape(PEER_SEL, LANES).T


def _topk(sc_t):
    n_tiles = sc_t.shape[0]
    out = pl.BlockSpec((LANES, PEER_SEL), lambda i: (i, 0))
    shp = jax.ShapeDtypeStruct((n_tiles * LANES, PEER_SEL), F32)
    vreg = (PEER_HEADS, LANES)
    return pl.pallas_call(
        _topk_kernel,
        grid=(n_tiles,),
        in_specs=[pl.BlockSpec((1, PEER_GROUPS * PEER_NKEYS, LANES), lambda i: (i, 0, 0))],
        out_specs=(out, out, out),
        out_shape=(shp, shp, shp),
        scratch_shapes=[pltpu.VMEM((2 * LISTS_PER_HALF, PEER_TOPK) + vreg, F32),
                        pltpu.VMEM((2 * LISTS_PER_HALF, PEER_TOPK) + vreg, F32),
                        pltpu.VMEM((LISTS_PER_HALF, PEER_TOPK) + vreg, F32),
                        pltpu.VMEM((LISTS_PER_HALF, PEER_TOPK) + vreg, F32),
                        pltpu.VMEM((2, PEER_TOPK) + vreg, F32),
                        pltpu.VMEM((2, PEER_TOPK) + vreg, F32),
                        pltpu.VMEM((PEER_TOPK,) + vreg, F32),
                        pltpu.VMEM((PEER_TOPK,) + vreg, F32)],
        compiler_params=_params(("arbitrary",)),
        name="topk",
    )(sc_t)


G_TILE = 128
G_ROWS = SUBLANES
G_BLOCKS = PEER_NKEYS // G_ROWS


def _gbuild_kernel(isel_ref, jsel_ref, g_ref, out_ref):
    sub = lax.broadcasted_iota(jnp.int32, (PEER_NKEYS, PEER_SEL), 0).astype(F32)
    for c in range(G_TILE):
        irow = isel_ref[c:c + 1, :]
        jrow = jsel_ref[c:c + 1, :]
        grow = g_ref[c:c + 1, :]
        a = jnp.where(sub == irow, grow, 0.0).astype(BF16)
        b = jnp.where(sub == jrow, 1.0, 0.0).astype(BF16)
        tile = _dot_nt(a, b)
        for e in range(G_BLOCKS):
            out_ref[e, c * G_ROWS:(c + 1) * G_ROWS, :] = tile[e * G_ROWS:(e + 1) * G_ROWS, :]


def _gbuild(isel, jsel, gates):
    T = isel.shape[0]
    row = pl.BlockSpec((G_TILE, PEER_SEL), lambda t: (t, 0))
    return pl.pallas_call(
        _gbuild_kernel,
        grid=(T // G_TILE,),
        in_specs=[row, row, row],
        out_specs=pl.BlockSpec((G_BLOCKS, G_TILE * G_ROWS, PEER_NKEYS), lambda t: (0, t, 0)),
        out_shape=jax.ShapeDtypeStruct((G_BLOCKS, T * G_ROWS, PEER_NKEYS), F32),
        compiler_params=_params(("arbitrary",)),
        name="gbuild",
    )(isel, jsel, gates)


def _gelu_tanh(x):
    return 0.5 * x * (1.0 + jnp.tanh(math.sqrt(2.0 / math.pi) * (x + 0.044715 * (x * x * x))))


def _peer_kernel(hn_ref, u_ref, v_ref, g_ref, h_ref, out_ref, w_ref):
    e = pl.program_id(1)
    tt = hn_ref.shape[0]

    @pl.when(e == 0)
    def _():
        out_ref[...] = h_ref[...]

    hn = hn_ref[...]
    for i in range(0, G_ROWS, 2):
        cols = slice(i * PEER_NKEYS, (i + 2) * PEER_NKEYS)
        act = _gelu_tanh(_dot_nt(hn, u_ref[cols, :]))
        gate = jnp.concatenate([g_ref[0, pl.ds(i, tt, stride=G_ROWS), :],
                                g_ref[0, pl.ds(i + 1, tt, stride=G_ROWS), :]], axis=1)
        w_ref[:, cols] = (act * gate).astype(BF16)
    out_ref[...] += _dot(w_ref[...], v_ref[...])


def _peer(hn2, h2, u, v, gmat, tt):
    T = hn2.shape[0]
    eb = G_ROWS * PEER_NKEYS
    tok = lambda dt: pl.BlockSpec((tt, D_MODEL), lambda t, e: (t, 0))
    exp = pl.BlockSpec((eb, D_MODEL), lambda t, e: (e, 0))
    return pl.pallas_call(
        _peer_kernel,
        grid=(T // tt, PEER_N // eb),
        in_specs=[tok(BF16), exp, exp,
                  pl.BlockSpec((1, tt * G_ROWS, PEER_NKEYS), lambda t, e: (e, t, 0)),
                  tok(F32)],
        out_specs=pl.BlockSpec((tt, D_MODEL), lambda t, e: (t, 0)),
        out_shape=jax.ShapeDtypeStruct((T, D_MODEL), F32),
        scratch_shapes=[pltpu.VMEM((tt, eb), BF16)],
        compiler_params=_params(("arbitrary", "arbitrary")),
        name="peer",
    )(hn2, u, v, gmat, h2)


def _layer(h, norm1_g, w_in, q_norm_g, k_norm_g, w_pool, pool_scale, w_out, norm2_g,
           w_query, sub_keys, expert_u, expert_v, biases):
    B, S, D = h.shape
    T = B * S
    tm = min(512, S)
    head = np.arange(NORM_COLS) // HEAD_DIM
    bd = jnp.asarray(head[:, None] == head[None, :], BF16)
    qg = (jnp.tile(q_norm_g, ATT_HEADS) * (HEAD_DIM ** -0.5)).reshape(1, KV_WIDTH)
    kg = jnp.tile(k_norm_g, ATT_HEADS).reshape(1, KV_WIDTH)

    res = _proj(h.reshape(T, D), norm1_g.reshape(1, D), w_in.astype(BF16), bd, qg, kg, tm)
    a, qs, ks, vs, gates = res[0], res[1:4], res[4:7], res[7:10], res[10]
    attn_outs, attn_lses = [], []
    for g, dilation in enumerate(DILATIONS):
        o, lse = _attn_pattern(qs[g], ks[g], vs[g], biases[g], dilation, B)
        attn_outs.append(o)
        attn_lses.append(lse)

    wq_t = w_query.T.astype(BF16)
    sk = sub_keys.reshape(PEER_GROUPS, PEER_NKEYS, PEER_HALF).astype(BF16)
    h_new, hn, sc_t = _mix(h, a.reshape(B, S, POOL_WIDTH), attn_outs, attn_lses,
                           jnp.asarray(_head_expand(), BF16), gates.reshape(B, S, 2 * D),
                           w_pool.astype(BF16), pool_scale.reshape(1, D), w_out.astype(BF16),
                           norm2_g.reshape(1, D), wq_t, sk, tm)
    isel, jsel, gsel = _topk(sc_t)
    gmat = _gbuild(isel, jsel, gsel)
    out = _peer(hn.reshape(T, D), h_new.reshape(T, D), expert_u.astype(BF16), expert_v.astype(BF16),
                gmat, min(1024, T))
    return out.reshape(B, S, D)


def kernel(x, norm1_g, w_in, q_norm_g, k_norm_g, w_pool, pool_scale, w_out, norm2_g, w_query, sub_keys,
           expert_u, expert_v):
    slopes = _alibi_slopes(N_PATTERNS * ATT_HEADS).reshape(N_PATTERNS, ATT_HEADS)
    biases = [jnp.asarray(_attn_bias(slopes[g], dil)) for g, (_, dil) in enumerate(DILATED_PATTERNS)]
    h = x
    for l in range(norm1_g.shape[0]):
        h = _layer(h, norm1_g[l], w_in[l], q_norm_g[l], k_norm_g[l], w_pool[l], pool_scale[l], w_out[l],
                   norm2_g[l], w_query[l], sub_keys[l], expert_u[l], expert_v[l], biases)
    return h
```

```python
import functools
import math

import numpy as np
import jax
import jax.numpy as jnp
from jax import lax
from jax.experimental import pallas as pl
from jax.experimental.pallas import tpu as pltpu

D_MODEL = 1024
EPS = 1e-6
NEG_INF = -1e30

POOL_WINDOWS = (2, 4, 8, 16)
POOL_GROUPS = 4
POOL_IN = 128
POOL_WIDTH = 512
POOL_OUT = 256
POOL_HALO = 16

ATT_HEADS = 16
HEAD_DIM = 64
DILATED_PATTERNS = ((128, 1), (512, 4), (2048, 16))
N_PATTERNS = 3
BAND = 128
KV_WIDTH = 1024
Q_OFF = POOL_WIDTH
K_OFF = Q_OFF + N_PATTERNS * KV_WIDTH
V_OFF = K_OFF + KV_WIDTH
GATE_OFF = V_OFF + KV_WIDTH
IN_WIDTH = GATE_OFF + 2 * D_MODEL

PEER_HEADS = 8
PEER_NKEYS = 128
PEER_N = PEER_NKEYS * PEER_NKEYS
PEER_HALF = 128
PEER_TOPK = 16
PEER_GROUPS = 2 * PEER_HEADS
PEER_SEL = PEER_HEADS * PEER_TOPK

LANES = 128
SUBLANES = 8
VMEM_LIMIT = 56 * 1024 * 1024

BF16 = jnp.bfloat16
F32 = jnp.float32


def _dot(a, b):
    return jnp.dot(a, b, preferred_element_type=F32)


def _dot_nt(a, b):
    return lax.dot_general(a, b, (((1,), (1,)), ((), ())), preferred_element_type=F32)


def _params(semantics):
    return pltpu.CompilerParams(dimension_semantics=semantics, vmem_limit_bytes=VMEM_LIMIT)


def _const_spec(shape):
    n = len(shape)
    return pl.BlockSpec(shape, lambda *_: (0,) * n)


KV_SLABS = KV_WIDTH // LANES
NORM_COLS = 256
DILATIONS = tuple(d for _, d in DILATED_PATTERNS)


def _store_dilated(y, slab_ref, out_refs):
    tm = y.shape[0]
    for d, ref in out_refs:
        if d == 1:
            ref[...] = y.astype(BF16)
    if all(d == 1 for d, _ in out_refs):
        return
    for s in range(KV_SLABS):
        slab_ref[s] = y[:, s * LANES:(s + 1) * LANES]
    for d, ref in out_refs:
        if d == 1:
            continue
        for r in range(d):
            for s in range(KV_SLABS):
                col = r * KV_WIDTH + s * LANES
                ref[:, col:col + LANES] = slab_ref[s, pl.ds(r, tm // d, stride=d), :].astype(BF16)


def _proj_kernel(x_ref, g1_ref, w_ref, bd_ref, qg_ref, kg_ref,
                 a_ref, q0_ref, q1_ref, q2_ref, k0_ref, k1_ref, k2_ref, v0_ref, v1_ref, v2_ref, gate_ref,
                 slab_ref):
    x = x_ref[...]
    inv = lax.rsqrt(jnp.mean(x * x, axis=-1, keepdims=True) + EPS)
    xn = (x * inv * g1_ref[...]).astype(BF16)

    def head_norm(y, gain):
        parts = []
        for c in range(0, KV_WIDTH, NORM_COLS):
            yc = y[:, c:c + NORM_COLS]
            ss = _dot((yc * yc).astype(BF16), bd_ref[...])
            parts.append(yc * lax.rsqrt(ss * (1.0 / HEAD_DIM) + EPS))
        return jnp.concatenate(parts, axis=1) * gain

    a_ref[...] = _dot(xn, w_ref[:, 0:POOL_WIDTH])
    for g, q_ref in enumerate((q0_ref, q1_ref, q2_ref)):
        lo = Q_OFF + g * KV_WIDTH
        q = head_norm(_dot(xn, w_ref[:, lo:lo + KV_WIDTH]), qg_ref[...])
        _store_dilated(q, slab_ref, [(DILATIONS[g], q_ref)])
    k = head_norm(_dot(xn, w_ref[:, K_OFF:K_OFF + KV_WIDTH]), kg_ref[...])
    _store_dilated(k, slab_ref, list(zip(DILATIONS, (k0_ref, k1_ref, k2_ref))))
    v = _dot(xn, w_ref[:, V_OFF:V_OFF + KV_WIDTH])
    _store_dilated(v, slab_ref, list(zip(DILATIONS, (v0_ref, v1_ref, v2_ref))))
    gate_ref[...] = jax.nn.sigmoid(_dot(xn, w_ref[:, GATE_OFF:IN_WIDTH])).astype(BF16)


def _proj(x2, g1, w_in, bd, qg, kg, tm):
    T = x2.shape[0]
    row = lambda w: pl.BlockSpec((tm, w), lambda i: (i, 0))
    dil = lambda d: pl.BlockSpec((tm // d, d * KV_WIDTH), lambda i: (i, 0))
    dil_shape = lambda d: jax.ShapeDtypeStruct((T // d, d * KV_WIDTH), BF16)
    once = lambda shape: pl.BlockSpec(shape, lambda i: (0, 0), pipeline_mode=pl.Buffered(1))
    views = [dil(d) for d in DILATIONS]
    view_shapes = [dil_shape(d) for d in DILATIONS]
    return pl.pallas_call(
        _proj_kernel,
        grid=(T // tm,),
        in_specs=[row(D_MODEL), once((1, D_MODEL)), once((D_MODEL, IN_WIDTH)),
                  once((NORM_COLS, NORM_COLS)), once((1, KV_WIDTH)), once((1, KV_WIDTH))],
        out_specs=(row(POOL_WIDTH), *views, *views, *views, row(2 * D_MODEL)),
        out_shape=(jax.ShapeDtypeStruct((T, POOL_WIDTH), F32), *view_shapes, *view_shapes, *view_shapes,
                   jax.ShapeDtypeStruct((T, 2 * D_MODEL), BF16)),
        scratch_shapes=[pltpu.VMEM((KV_SLABS, tm, LANES), F32)],
        compiler_params=_params(("arbitrary",)),
        name="proj",
    )(x2, g1, w_in, bd, qg, kg)


def _alibi_slopes(n):
    def geometric(k):
        start = 2.0 ** (-8.0 / k)
        return [start ** (i + 1) for i in range(k)]
    p = 2 ** int(math.floor(math.log2(n)))
    s = geometric(p) + geometric(2 * p)[0::2][: n - p]
    return np.sort(np.array(s, dtype=np.float32))[::-1].copy()


def _attn_bias(slopes, dilation):
    qi = np.arange(BAND)[:, None]
    ki = np.arange(2 * BAND)[None, :]
    step = qi + BAND - ki
    valid = (step >= 0) & (step <= BAND)
    alibi = -slopes[:, None, None] * (step * dilation).astype(np.float32)[None]
    general = np.where(valid[None], alibi, np.float32(NEG_INF)).astype(np.float32)
    first = np.where((ki >= BAND)[None], general, np.float32(NEG_INF)).astype(np.float32)
    return np.stack([general, first])


HEAD_PAIRS = ATT_HEADS // 2


def _attn_kernel(q_ref, kp_ref, kc_ref, vp_ref, vc_ref, bias_ref, o_ref, lse_ref, s_ref, m_ref):
    q_blocks = q_ref.shape[1] // BAND
    first_step = (pl.program_id(2) == 0).astype(jnp.int32)
    lane = lax.broadcasted_iota(jnp.int32, (BAND, LANES), 1)
    low = lane < HEAD_DIM
    low_b = (lax.broadcasted_iota(jnp.int32, (1, LANES), 1) < HEAD_DIM).astype(BF16)
    lane_sel = (low_b, 1 - low_b)

    for blk in range(q_blocks):
        rows = slice(blk * BAND, (blk + 1) * BAND)
        if blk == 0:
            k_prev, v_prev, first = kp_ref, vp_ref, first_step
            before = slice(0, BAND)
        else:
            k_prev, v_prev, first = kc_ref, vc_ref, 0
            before = slice((blk - 1) * BAND, blk * BAND)

        for h in range(ATT_HEADS):
            cols = slice((h // 2) * LANES, (h // 2 + 1) * LANES)
            qm = q_ref[0, rows, cols] * lane_sel[h % 2]
            sp = _dot_nt(qm, k_prev[0, before, cols]) + bias_ref[first, h, :, 0:BAND]
            sc = _dot_nt(qm, kc_ref[0, rows, cols]) + bias_ref[first, h, :, BAND:2 * BAND]
            s_ref[blk, h, :, 0:BAND] = sp
            s_ref[blk, h, :, BAND:2 * BAND] = sc
            m_ref[blk, h] = jnp.broadcast_to(jnp.max(jnp.maximum(sp, sc), axis=-1, keepdims=True), (BAND, LANES))

        lse_tile = jnp.zeros((BAND, LANES), F32)
        for j in range(HEAD_PAIRS):
            cols = slice(j * LANES, (j + 1) * LANES)
            res = []
            for half in range(2):
                h = 2 * j + half
                own, other = lane_sel[half], lane_sel[1 - half]
                pp = jnp.exp(s_ref[blk, h, :, 0:BAND] - m_ref[blk, h]).astype(BF16)
                pc = jnp.exp(s_ref[blk, h, :, BAND:2 * BAND] - m_ref[blk, h]).astype(BF16)
                res.append(_dot(pp, v_prev[0, before, cols] * own + other)
                           + _dot(pc, vc_ref[0, rows, cols] * own + other))
            num = jnp.where(low, res[0], res[1])
            den = pltpu.roll(jnp.where(low, res[1], res[0]), HEAD_DIM, axis=1)
            o_ref[0, rows, cols] = (num / den).astype(BF16)
            lse = jnp.where(low, m_ref[blk, 2 * j], m_ref[blk, 2 * j + 1]) + jnp.log(den)
            lse_tile = jnp.where((lane == j) | (lane == HEAD_DIM + j), lse, lse_tile)
        lse_ref[0, rows, :] = lse_tile


def _attn_pattern(q, k, v, bias, dilation, B):
    d = dilation
    rows = q.shape[0] // B
    q_blocks = 2 if rows % (2 * BAND) == 0 else 1
    qr = q_blocks * BAND
    view = lambda a: a.reshape(B, rows, d * KV_WIDTH)
    cur = lambda w: pl.BlockSpec((1, qr, w), lambda b, r, n: (b, n, r))
    prev = pl.BlockSpec((1, BAND, KV_WIDTH), lambda b, r, n: (b, jnp.maximum(q_blocks * n - 1, 0), r))
    o, lse = pl.pallas_call(
        _attn_kernel,
        grid=(B, d, rows // qr),
        in_specs=[cur(KV_WIDTH), prev, cur(KV_WIDTH), prev, cur(KV_WIDTH),
                  pl.BlockSpec(bias.shape, lambda b, r, n: (0, 0, 0, 0))],
        out_specs=(cur(KV_WIDTH), cur(LANES)),
        out_shape=(jax.ShapeDtypeStruct((B, rows, d * KV_WIDTH), BF16),
                   jax.ShapeDtypeStruct((B, rows, d * LANES), F32)),
        scratch_shapes=[pltpu.VMEM((q_blocks, ATT_HEADS, BAND, 2 * BAND), F32),
                        pltpu.VMEM((q_blocks, ATT_HEADS, BAND, LANES), F32)],
        compiler_params=_params(("arbitrary", "arbitrary", "arbitrary")),
        name=f"attn_d{d}",
    )(view(q), view(k), view(k), view(v), view(v), bias)
    return o, lse


def _head_expand():
    e = np.zeros((LANES, KV_WIDTH), np.float32)
    for h in range(ATT_HEADS):
        e[(h % 2) * HEAD_DIM + h // 2, h * HEAD_DIM:(h + 1) * HEAD_DIM] = 1.0
    return e


def _mix_kernel(x_ref, a_ref, ah_ref, o0_ref, o1_ref, o2_ref, l0_ref, l1_ref, l2_ref, ex_ref, gate_ref,
                wp_ref, ps_ref, wo_ref, g2_ref, wq_ref, sk_ref,
                h_ref, hn_ref, sc_ref, ext_ref, merged_ref, onat_ref, lnat_ref):
    tm = x_ref.shape[1]
    i = pl.program_id(1)

    def natural_rows(src_ref, d, slabs, dst_ref):
        n = src_ref.shape[1]
        for r in range(d):
            for s in range(slabs):
                col = (r * slabs + s) * LANES
                dst_ref[s, pl.ds(r, n, stride=d), :] = src_ref[0, :, col:col + LANES].astype(F32)

    for p, (o_ref, l_ref) in enumerate(((o1_ref, l1_ref), (o2_ref, l2_ref))):
        natural_rows(o_ref, DILATIONS[p + 1], KV_SLABS, onat_ref.at[p])
        natural_rows(l_ref, DILATIONS[p + 1], 1, lnat_ref.at[p])

    def attn_cols(p, g):
        if p == 0:
            return o0_ref[0, :, g * POOL_OUT:(g + 1) * POOL_OUT].astype(F32)
        return jnp.concatenate([onat_ref[p - 1, 2 * g], onat_ref[p - 1, 2 * g + 1]], axis=1)

    lses = [l0_ref[0], lnat_ref[0, 0], lnat_ref[1, 0]]
    top = jnp.maximum(jnp.maximum(lses[0], lses[1]), lses[2])
    es = [jnp.exp(l - top) for l in lses]
    tot = es[0] + es[1] + es[2]
    w_hi, w_lo = [], []
    for e in es:
        w = e / tot
        hi = w.astype(BF16)
        w_hi.append(hi)
        w_lo.append((w - hi.astype(F32)).astype(BF16))
    ext_ref[0:POOL_HALO, :] = jnp.where(i == 0, 0.0, ah_ref[0])
    ext_ref[POOL_HALO:POOL_HALO + tm, :] = a_ref[0]
    t = i * tm + lax.broadcasted_iota(jnp.int32, (tm, 1), 0)
    for g, w in enumerate(POOL_WINDOWS):
        cols = slice(g * POOL_IN, (g + 1) * POOL_IN)
        wsum = ext_ref[POOL_HALO:POOL_HALO + tm, cols]
        for j in range(1, w):
            wsum = wsum + ext_ref[POOL_HALO - j:POOL_HALO - j + tm, cols]
        count = jnp.minimum(t + 1, w).astype(F32)
        dlt = (wsum / count - a_ref[0, :, cols]).astype(BF16)
        ocol = slice(g * POOL_OUT, (g + 1) * POOL_OUT)
        pool = _dot(dlt, wp_ref[g]) * ps_ref[:, ocol]
        attn = jnp.zeros((tm, POOL_OUT), F32)
        for p, (hi, lo) in enumerate(zip(w_hi, w_lo)):
            weight = _dot(hi, ex_ref[:, ocol]) + _dot(lo, ex_ref[:, ocol])
            attn = attn + weight * attn_cols(p, g)
        merged = (gate_ref[0, :, ocol].astype(F32) * pool
                  + gate_ref[0, :, D_MODEL + g * POOL_OUT:D_MODEL + (g + 1) * POOL_OUT].astype(F32) * attn)
        merged_ref[:, ocol] = merged.astype(BF16)
    h = x_ref[0] + _dot(merged_ref[...], wo_ref[...])
    h_ref[0] = h
    hn = (h * lax.rsqrt(jnp.mean(h * h, axis=-1, keepdims=True) + EPS) * g2_ref[...]).astype(BF16)
    hn_ref[0] = hn
    qt = _dot_nt(wq_ref[...], hn).astype(BF16)
    for g in range(PEER_GROUPS):
        head, half = divmod(g, 2)
        rows = slice(g * PEER_NKEYS, (g + 1) * PEER_NKEYS)
        sc = _dot(sk_ref[g], qt[rows, :])
        for j in range(tm // LANES):
            sc_ref[j, pl.ds(half * PEER_HEADS * PEER_NKEYS + head, PEER_NKEYS, stride=PEER_HEADS), :] = (
                sc[:, j * LANES:(j + 1) * LANES])


def _mix(x, a, attn_outs, attn_lses, expand, gates, w_pool, pool_scale, w_out, g2, wq_t, sub_keys, tm):
    B, S, _ = x.shape
    T = B * S
    nt = S // tm
    halo_per_tile = tm // POOL_HALO
    tile = lambda w: pl.BlockSpec((1, tm, w), lambda b, i: (b, i, 0))
    dil = lambda d, w: pl.BlockSpec((1, tm // d, d * w), lambda b, i: (b, i, 0))
    halo = pl.BlockSpec((1, POOL_HALO, POOL_WIDTH), lambda b, i: (b, jnp.maximum(i * halo_per_tile - 1, 0), 0))
    once = lambda shape: pl.BlockSpec(shape, lambda b, i: (0,) * len(shape), pipeline_mode=pl.Buffered(1))
    return pl.pallas_call(
        _mix_kernel,
        grid=(B, nt),
        in_specs=[tile(D_MODEL), tile(POOL_WIDTH), halo,
                  *[dil(d, KV_WIDTH) for d in DILATIONS], *[dil(d, LANES) for d in DILATIONS],
                  once(expand.shape), tile(2 * D_MODEL),
                  once(w_pool.shape), once((1, D_MODEL)), once((D_MODEL, D_MODEL)), once((1, D_MODEL)),
                  once(wq_t.shape), once(sub_keys.shape)],
        out_specs=(tile(D_MODEL), tile(D_MODEL),
                   pl.BlockSpec((tm // LANES, PEER_GROUPS * PEER_NKEYS, LANES), lambda b, i: (b * nt + i, 0, 0))),
        out_shape=(jax.ShapeDtypeStruct((B, S, D_MODEL), F32),
                   jax.ShapeDtypeStruct((B, S, D_MODEL), BF16),
                   jax.ShapeDtypeStruct((T // LANES, PEER_GROUPS * PEER_NKEYS, LANES), F32)),
        scratch_shapes=[pltpu.VMEM((POOL_HALO + tm, POOL_WIDTH), F32),
                        pltpu.VMEM((tm, D_MODEL), BF16),
                        pltpu.VMEM((N_PATTERNS - 1, KV_SLABS, tm, LANES), F32),
                        pltpu.VMEM((N_PATTERNS - 1, 1, tm, LANES), F32)],
        compiler_params=_params(("arbitrary", "arbitrary")),
        name="mix",
    )(x, a, a, *attn_outs, *attn_lses, expand, gates, w_pool, pool_scale, w_out, g2, wq_t, sub_keys)


_PAIRS = frozenset((a, b) for a in range(PEER_TOPK) for b in range(PEER_TOPK) if (a + 1) * (b + 1) <= PEER_TOPK)


def _oddeven_merge_sort_pairs(n):
    pairs = []

    def merge(lo, hi, r):
        step = r * 2
        if step < hi - lo:
            merge(lo, hi, step)
            merge(lo + r, hi, step)
            pairs.extend((i, i + r) for i in range(lo + r, hi - r, step))
        else:
            pairs.append((lo, lo + r))

    def sort(lo, hi):
        if hi - lo >= 1:
            mid = lo + (hi - lo) // 2
            sort(lo, mid)
            sort(mid + 1, hi)
            merge(lo, hi, 1)

    sort(0, n - 1)
    return tuple(pairs)


_SORT_PAIRS = _oddeven_merge_sort_pairs(PEER_TOPK)
LISTS_PER_HALF = PEER_NKEYS // PEER_TOPK


def _precedes(va, ia, vb, ib):
    if isinstance(ia, float) and isinstance(ib, float):
        return (va >= vb) if ia < ib else (va > vb)
    return (va > vb) | ((va == vb) & (ia < ib))


def _compare_exchange(a, b):
    (va, ia), (vb, ib) = a, b
    a_first = _precedes(va, ia, vb, ib)
    return ((jnp.maximum(va, vb), jnp.where(a_first, ia, ib)),
            (jnp.minimum(va, vb), jnp.where(a_first, ib, ia)))


def _sort_desc(items):
    items = list(items)
    for i, j in _SORT_PAIRS:
        items[i], items[j] = _compare_exchange(items[i], items[j])
    return items


def _bitonic_sort_desc(items):
    items = list(items)
    n = len(items)
    span = n // 2
    while span:
        for k in range(n):
            if not k & span:
                items[k], items[k + span] = _compare_exchange(items[k], items[k + span])
        span //= 2
    return items


def _merge_keep_top(a, b):
    n = len(a)
    out = []
    for k in range(n):
        if n - 1 - k >= len(b):
            out.append(a[k])
            continue
        (va, ia), (vb, ib) = a[k], b[n - 1 - k]
        out.append((jnp.maximum(va, vb), jnp.where(_precedes(va, ia, vb, ib), ia, ib)))
    return _bitonic_sort_desc(out)


def _topk_kernel(sc_ref, isel_ref, jsel_ref, g_ref, av_ref, ai_ref, bv_ref, bi_ref, ts_ref, ti_ref,
                 e_ref, s_ref):
    def sort_group(g, carry):
        base = pl.multiple_of(g * PEER_TOPK * PEER_HEADS, PEER_TOPK * PEER_HEADS)
        items = [(sc_ref[0, pl.ds(base + k * PEER_HEADS, PEER_HEADS), :], float(k)) for k in range(PEER_TOPK)]
        first_key = ((g % LISTS_PER_HALF) * PEER_TOPK).astype(F32)
        for r, (v, i) in enumerate(_sort_desc(items)):
            av_ref[g, r] = v
            ai_ref[g, r] = i + first_key
        return carry

    lax.fori_loop(0, 2 * LISTS_PER_HALF, sort_group, 0)

    def merge_level(src_v, src_i, dst_v, dst_i, n_out):
        def body(m, carry):
            lists = [[(src_v[2 * m + side, r], src_i[2 * m + side, r]) for r in range(PEER_TOPK)]
                     for side in range(2)]
            for r, (v, i) in enumerate(_merge_keep_top(*lists)):
                dst_v[m, r] = v
                dst_i[m, r] = i
            return carry

        lax.fori_loop(0, n_out, body, 0)

    merge_level(av_ref, ai_ref, bv_ref, bi_ref, LISTS_PER_HALF)
    merge_level(bv_ref, bi_ref, av_ref, ai_ref, LISTS_PER_HALF // 2)
    merge_level(av_ref, ai_ref, ts_ref, ti_ref, 2)

    def cand(a, b):
        expert = ti_ref[0, a] * float(PEER_NKEYS) + ti_ref[1, b]
        return ts_ref[0, a] + ts_ref[1, b], expert + float((a * PEER_TOPK + b) * PEER_N)

    by_row = {a: [cand(a, b) for b in range(PEER_TOPK) if (a, b) in _PAIRS] for a in range(PEER_TOPK)}
    single = [by_row[a][0] for a in range(8, PEER_TOPK)]
    second = _bitonic_sort_desc(by_row[1] + single[::-1])
    third = _sort_desc(by_row[2] + by_row[3] + by_row[4] + by_row[5] + by_row[6])
    third = _merge_keep_top(third, by_row[7])
    best = _merge_keep_top(_merge_keep_top(by_row[0], second), third)
    for r, (v, key) in enumerate(best):
        s_ref[r] = v
        e_ref[r] = key - jnp.floor(key * (1.0 / PEER_N)) * float(PEER_N)

    best = s_ref[...]
    ex = jnp.exp(best - best[0:1])
    gate = ex / jnp.sum(ex, axis=0, keepdims=True)
    e = e_ref[...].reshape(PEER_SEL, LANES).T
    i_sel = jnp.floor(e * (1.0 / PEER_NKEYS))
    isel_ref[...] = i_sel
    jsel_ref[...] = e - i_sel * PEER_NKEYS
    g_ref[...] = gate.reshape(PEER_SEL, LANES).T


def _topk(sc_t):
    n_tiles = sc_t.shape[0]
    out = pl.BlockSpec((LANES, PEER_SEL), lambda i: (i, 0))
    shp = jax.ShapeDtypeStruct((n_tiles * LANES, PEER_SEL), F32)
    vreg = (PEER_HEADS, LANES)
    return pl.pallas_call(
        _topk_kernel,
        grid=(n_tiles,),
        in_specs=[pl.BlockSpec((1, PEER_GROUPS * PEER_NKEYS, LANES), lambda i: (i, 0, 0))],
        out_specs=(out, out, out),
        out_shape=(shp, shp, shp),
        scratch_shapes=[pltpu.VMEM((2 * LISTS_PER_HALF, PEER_TOPK) + vreg, F32),
                        pltpu.VMEM((2 * LISTS_PER_HALF, PEER_TOPK) + vreg, F32),
                        pltpu.VMEM((LISTS_PER_HALF, PEER_TOPK) + vreg, F32),
                        pltpu.VMEM((LISTS_PER_HALF, PEER_TOPK) + vreg, F32),
                        pltpu.VMEM((2, PEER_TOPK) + vreg, F32),
                        pltpu.VMEM((2, PEER_TOPK) + vreg, F32),
                        pltpu.VMEM((PEER_TOPK,) + vreg, F32),
                        pltpu.VMEM((PEER_TOPK,) + vreg, F32)],
        compiler_params=_params(("arbitrary",)),
        name="topk",
    )(sc_t)


G_TILE = 128
G_PITCH = G_TILE + SUBLANES // 2


def _gbuild_kernel(isel_ref, jsel_ref, g_ref, out_ref, tile_ref):
    sub = lax.broadcasted_iota(jnp.int32, (PEER_NKEYS, PEER_SEL), 0).astype(F32)

    def body(c, carry):
        irow = isel_ref[pl.ds(c, 1), :]
        jrow = jsel_ref[pl.ds(c, 1), :]
        grow = g_ref[pl.ds(c, 1), :]
        a = jnp.where(sub == irow, grow, 0.0).astype(BF16)
        b = jnp.where(sub == jrow, 1.0, 0.0).astype(BF16)
        tile_ref[pl.ds(c, PEER_NKEYS, stride=G_PITCH), :] = _dot_nt(a, b)
        return carry

    lax.fori_loop(0, G_TILE, body, 0, unroll=True)
    for i in range(PEER_NKEYS):
        out_ref[i] = tile_ref[i * G_PITCH:i * G_PITCH + G_TILE, :].astype(BF16)


def _gbuild(isel, jsel, gates):
    T = isel.shape[0]
    row = pl.BlockSpec((G_TILE, PEER_SEL), lambda t: (t, 0))
    return pl.pallas_call(
        _gbuild_kernel,
        grid=(T // G_TILE,),
        in_specs=[row, row, row],
        out_specs=pl.BlockSpec((PEER_NKEYS, G_TILE, PEER_NKEYS), lambda t: (0, t, 0)),
        out_shape=jax.ShapeDtypeStruct((PEER_NKEYS, T, PEER_NKEYS), BF16),
        scratch_shapes=[pltpu.VMEM((PEER_NKEYS * G_PITCH, PEER_NKEYS), F32)],
        compiler_params=_params(("arbitrary",)),
        name="gbuild",
    )(isel, jsel, gates)


def _gelu_tanh(x):
    return 0.5 * x * (1.0 + jnp.tanh(math.sqrt(2.0 / math.pi) * (x + 0.044715 * (x * x * x))))


def _peer_kernel(hn_ref, u_ref, v_ref, g_ref, h_ref, out_ref, w_ref):
    e = pl.program_id(1)
    n_rows = g_ref.shape[0]

    @pl.when(e == 0)
    def _():
        out_ref[...] = h_ref[...]

    hn = hn_ref[...]
    for i in range(0, n_rows, 2):
        cols = slice(i * PEER_NKEYS, (i + 2) * PEER_NKEYS)
        act = _gelu_tanh(_dot_nt(hn, u_ref[cols, :]))
        gate = jnp.concatenate([g_ref[i], g_ref[i + 1]], axis=1).astype(F32)
        w_ref[:, cols] = (act * gate).astype(BF16)
    out_ref[...] += _dot(w_ref[...], v_ref[...])


def _peer(hn2, h2, u, v, gmat, tt, eb):
    T = hn2.shape[0]
    n_rows = eb // PEER_NKEYS
    tok = lambda dt: pl.BlockSpec((tt, D_MODEL), lambda t, e: (t, 0))
    exp = pl.BlockSpec((eb, D_MODEL), lambda t, e: (e, 0))
    return pl.pallas_call(
        _peer_kernel,
        grid=(T // tt, PEER_N // eb),
        in_specs=[tok(BF16), exp, exp,
                  pl.BlockSpec((n_rows, tt, PEER_NKEYS), lambda t, e: (e, t, 0)),
                  tok(F32)],
        out_specs=pl.BlockSpec((tt, D_MODEL), lambda t, e: (t, 0)),
        out_shape=jax.ShapeDtypeStruct((T, D_MODEL), F32),
        scratch_shapes=[pltpu.VMEM((tt, eb), BF16)],
        compiler_params=_params(("arbitrary", "arbitrary")),
        name="peer",
    )(hn2, u, v, gmat, h2)


def _layer(h, norm1_g, w_in, q_norm_g, k_norm_g, w_pool, pool_scale, w_out, norm2_g,
           w_query, sub_keys, expert_u, expert_v, biases):
    B, S, D = h.shape
    T = B * S
    tm = min(512, S)
    head = np.arange(NORM_COLS) // HEAD_DIM
    bd = jnp.asarray(head[:, None] == head[None, :], BF16)
    qg = (jnp.tile(q_norm_g, ATT_HEADS) * (HEAD_DIM ** -0.5)).reshape(1, KV_WIDTH)
    kg = jnp.tile(k_norm_g, ATT_HEADS).reshape(1, KV_WIDTH)

    res = _proj(h.reshape(T, D), norm1_g.reshape(1, D), w_in.astype(BF16), bd, qg, kg, tm)
    a, qs, ks, vs, gates = res[0], res[1:4], res[4:7], res[7:10], res[10]
    attn_outs, attn_lses = [], []
    for g, dilation in enumerate(DILATIONS):
        o, lse = _attn_pattern(qs[g], ks[g], vs[g], biases[g], dilation, B)
        attn_outs.append(o)
        attn_lses.append(lse)

    wq_t = w_query.T.astype(BF16)
    sk = sub_keys.reshape(PEER_GROUPS, PEER_NKEYS, PEER_HALF).astype(BF16)
    h_new, hn, sc_t = _mix(h, a.reshape(B, S, POOL_WIDTH), attn_outs, attn_lses,
                           jnp.asarray(_head_expand(), BF16), gates.reshape(B, S, 2 * D),
                           w_pool.astype(BF16), pool_scale.reshape(1, D), w_out.astype(BF16),
                           norm2_g.reshape(1, D), wq_t, sk, tm)
    isel, jsel, gsel = _topk(sc_t)
    gmat = _gbuild(isel, jsel, gsel)
    out = _peer(hn.reshape(T, D), h_new.reshape(T, D), expert_u.astype(BF16), expert_v.astype(BF16),
                gmat, min(1024, T), 2048)
    return out.reshape(B, S, D)


def kernel(x, norm1_g, w_in, q_norm_g, k_norm_g, w_pool, pool_scale, w_out, norm2_g, w_query, sub_keys,
           expert_u, expert_v):
    slopes = _alibi_slopes(N_PATTERNS * ATT_HEADS).reshape(N_PATTERNS, ATT_HEADS)
    biases = [jnp.asarray(_attn_bias(slopes[g], dil)) for g, (_, dil) in enumerate(DILATED_PATTERNS)]
    h = x
    for l in range(norm1_g.shape[0]):
        h = _layer(h, norm1_g[l], w_in[l], q_norm_g[l], k_norm_g[l], w_pool[l], pool_scale[l], w_out[l],
                   norm2_g[l], w_query[l], sub_keys[l], expert_u[l], expert_v[l], biases)
    return h
```

```python
import functools
import math

import numpy as np
import jax
import jax.numpy as jnp
from jax import lax
from jax.experimental import pallas as pl
from jax.experimental.pallas import tpu as pltpu

D_MODEL = 1024
EPS = 1e-6
NEG_INF = -1e30

POOL_WINDOWS = (2, 4, 8, 16)
POOL_GROUPS = 4
POOL_IN = 128
POOL_WIDTH = 512
POOL_OUT = 256
POOL_HALO = 16

ATT_HEADS = 16
HEAD_DIM = 64
DILATED_PATTERNS = ((128, 1), (512, 4), (2048, 16))
N_PATTERNS = 3
BAND = 128
KV_WIDTH = 1024
Q_OFF = POOL_WIDTH
K_OFF = Q_OFF + N_PATTERNS * KV_WIDTH
V_OFF = K_OFF + KV_WIDTH
GATE_OFF = V_OFF + KV_WIDTH
IN_WIDTH = GATE_OFF + 2 * D_MODEL

PEER_HEADS = 8
PEER_NKEYS = 128
PEER_N = PEER_NKEYS * PEER_NKEYS
PEER_HALF = 128
PEER_TOPK = 16
PEER_GROUPS = 2 * PEER_HEADS
PEER_SEL = PEER_HEADS * PEER_TOPK

LANES = 128
SUBLANES = 8
VMEM_LIMIT = 56 * 1024 * 1024

BF16 = jnp.bfloat16
F32 = jnp.float32


def _dot(a, b):
    return jnp.dot(a, b, preferred_element_type=F32)


def _dot_nt(a, b):
    return lax.dot_general(a, b, (((1,), (1,)), ((), ())), preferred_element_type=F32)


def _params(semantics):
    return pltpu.CompilerParams(dimension_semantics=semantics, vmem_limit_bytes=VMEM_LIMIT)


def _const_spec(shape):
    n = len(shape)
    return pl.BlockSpec(shape, lambda *_: (0,) * n)


KV_SLABS = KV_WIDTH // LANES
NORM_COLS = 256
DILATIONS = tuple(d for _, d in DILATED_PATTERNS)


def _store_dilated(y, slab_ref, out_refs):
    tm = y.shape[0]
    for d, ref in out_refs:
        if d == 1:
            ref[...] = y.astype(BF16)
    if all(d == 1 for d, _ in out_refs):
        return
    for s in range(KV_SLABS):
        slab_ref[s] = y[:, s * LANES:(s + 1) * LANES]
    for d, ref in out_refs:
        if d == 1:
            continue
        for r in range(d):
            for s in range(KV_SLABS):
                col = r * KV_WIDTH + s * LANES
                ref[:, col:col + LANES] = slab_ref[s, pl.ds(r, tm // d, stride=d), :].astype(BF16)


def _proj_kernel(x_ref, g1_ref, w_ref, bd_ref, qg_ref, kg_ref,
                 a_ref, q0_ref, q1_ref, q2_ref, k0_ref, k1_ref, k2_ref, v0_ref, v1_ref, v2_ref, gate_ref,
                 slab_ref):
    x = x_ref[...]
    inv = lax.rsqrt(jnp.mean(x * x, axis=-1, keepdims=True) + EPS)
    xn = (x * inv * g1_ref[...]).astype(BF16)

    def head_norm(y, gain):
        parts = []
        for c in range(0, KV_WIDTH, NORM_COLS):
            yc = y[:, c:c + NORM_COLS]
            ss = _dot((yc * yc).astype(BF16), bd_ref[...])
            parts.append(yc * lax.rsqrt(ss * (1.0 / HEAD_DIM) + EPS))
        return jnp.concatenate(parts, axis=1) * gain

    a_ref[...] = _dot(xn, w_ref[:, 0:POOL_WIDTH])
    for g, q_ref in enumerate((q0_ref, q1_ref, q2_ref)):
        lo = Q_OFF + g * KV_WIDTH
        q = head_norm(_dot(xn, w_ref[:, lo:lo + KV_WIDTH]), qg_ref[...])
        _store_dilated(q, slab_ref, [(DILATIONS[g], q_ref)])
    k = head_norm(_dot(xn, w_ref[:, K_OFF:K_OFF + KV_WIDTH]), kg_ref[...])
    _store_dilated(k, slab_ref, list(zip(DILATIONS, (k0_ref, k1_ref, k2_ref))))
    v = _dot(xn, w_ref[:, V_OFF:V_OFF + KV_WIDTH])
    _store_dilated(v, slab_ref, list(zip(DILATIONS, (v0_ref, v1_ref, v2_ref))))
    gate_ref[...] = jax.nn.sigmoid(_dot(xn, w_ref[:, GATE_OFF:IN_WIDTH])).astype(BF16)


def _proj(x2, g1, w_in, bd, qg, kg, tm):
    T = x2.shape[0]
    row = lambda w: pl.BlockSpec((tm, w), lambda i: (i, 0))
    dil = lambda d: pl.BlockSpec((tm // d, d * KV_WIDTH), lambda i: (i, 0))
    dil_shape = lambda d: jax.ShapeDtypeStruct((T // d, d * KV_WIDTH), BF16)
    once = lambda shape: pl.BlockSpec(shape, lambda i: (0, 0), pipeline_mode=pl.Buffered(1))
    views = [dil(d) for d in DILATIONS]
    view_shapes = [dil_shape(d) for d in DILATIONS]
    return pl.pallas_call(
        _proj_kernel,
        grid=(T // tm,),
        in_specs=[row(D_MODEL), once((1, D_MODEL)), once((D_MODEL, IN_WIDTH)),
                  once((NORM_COLS, NORM_COLS)), once((1, KV_WIDTH)), once((1, KV_WIDTH))],
        out_specs=(row(POOL_WIDTH), *views, *views, *views, row(2 * D_MODEL)),
        out_shape=(jax.ShapeDtypeStruct((T, POOL_WIDTH), F32), *view_shapes, *view_shapes, *view_shapes,
                   jax.ShapeDtypeStruct((T, 2 * D_MODEL), BF16)),
        scratch_shapes=[pltpu.VMEM((KV_SLABS, tm, LANES), F32)],
        compiler_params=_params(("arbitrary",)),
        name="proj",
    )(x2, g1, w_in, bd, qg, kg)


def _alibi_slopes(n):
    def geometric(k):
        start = 2.0 ** (-8.0 / k)
        return [start ** (i + 1) for i in range(k)]
    p = 2 ** int(math.floor(math.log2(n)))
    s = geometric(p) + geometric(2 * p)[0::2][: n - p]
    return np.sort(np.array(s, dtype=np.float32))[::-1].copy()


def _attn_bias(slopes, dilation):
    qi = np.arange(BAND)[:, None]
    ki = np.arange(2 * BAND)[None, :]
    step = qi + BAND - ki
    valid = (step >= 0) & (step <= BAND)
    alibi = -slopes[:, None, None] * (step * dilation).astype(np.float32)[None]
    general = np.where(valid[None], alibi, np.float32(NEG_INF)).astype(np.float32)
    first = np.where((ki >= BAND)[None], general, np.float32(NEG_INF)).astype(np.float32)
    return np.stack([general, first])


HEAD_PAIRS = ATT_HEADS // 2


def _attn_kernel(q_ref, kp_ref, kc_ref, vp_ref, vc_ref, bias_ref, o_ref, lse_ref, s_ref, m_ref):
    q_blocks = q_ref.shape[1] // BAND
    first_step = (pl.program_id(2) == 0).astype(jnp.int32)
    lane = lax.broadcasted_iota(jnp.int32, (BAND, LANES), 1)
    low = lane < HEAD_DIM
    low_b = (lax.broadcasted_iota(jnp.int32, (1, LANES), 1) < HEAD_DIM).astype(BF16)
    lane_sel = (low_b, 1 - low_b)

    for blk in range(q_blocks):
        rows = slice(blk * BAND, (blk + 1) * BAND)
        if blk == 0:
            k_prev, v_prev, first = kp_ref, vp_ref, first_step
            before = slice(0, BAND)
        else:
            k_prev, v_prev, first = kc_ref, vc_ref, 0
            before = slice((blk - 1) * BAND, blk * BAND)

        for h in range(ATT_HEADS):
            cols = slice((h // 2) * LANES, (h // 2 + 1) * LANES)
            qm = q_ref[0, rows, cols] * lane_sel[h % 2]
            sp = _dot_nt(qm, k_prev[0, before, cols]) + bias_ref[first, h, :, 0:BAND]
            sc = _dot_nt(qm, kc_ref[0, rows, cols]) + bias_ref[first, h, :, BAND:2 * BAND]
            s_ref[blk, h, :, 0:BAND] = sp
            s_ref[blk, h, :, BAND:2 * BAND] = sc
            m_ref[blk, h] = jnp.broadcast_to(jnp.max(jnp.maximum(sp, sc), axis=-1, keepdims=True), (BAND, LANES))

        lse_tile = jnp.zeros((BAND, LANES), F32)
        for j in range(HEAD_PAIRS):
            cols = slice(j * LANES, (j + 1) * LANES)
            res = []
            for half in range(2):
                h = 2 * j + half
                own, other = lane_sel[half], lane_sel[1 - half]
                pp = jnp.exp(s_ref[blk, h, :, 0:BAND] - m_ref[blk, h]).astype(BF16)
                pc = jnp.exp(s_ref[blk, h, :, BAND:2 * BAND] - m_ref[blk, h]).astype(BF16)
                res.append(_dot(pp, v_prev[0, before, cols] * own + other)
                           + _dot(pc, vc_ref[0, rows, cols] * own + other))
            num = jnp.where(low, res[0], res[1])
            den = pltpu.roll(jnp.where(low, res[1], res[0]), HEAD_DIM, axis=1)
            o_ref[0, rows, cols] = (num / den).astype(BF16)
            lse = jnp.where(low, m_ref[blk, 2 * j], m_ref[blk, 2 * j + 1]) + jnp.log(den)
            lse_tile = jnp.where((lane == j) | (lane == HEAD_DIM + j), lse, lse_tile)
        lse_ref[0, rows, :] = lse_tile


def _attn_pattern(q, k, v, bias, dilation, B):
    d = dilation
    rows = q.shape[0] // B
    q_blocks = max(n for n in (1, 2, 4) if rows % (n * BAND) == 0)
    qr = q_blocks * BAND
    view = lambda a: a.reshape(B, rows, d * KV_WIDTH)
    cur = lambda w: pl.BlockSpec((1, qr, w), lambda b, r, n: (b, n, r))
    prev = pl.BlockSpec((1, BAND, KV_WIDTH), lambda b, r, n: (b, jnp.maximum(q_blocks * n - 1, 0), r))
    o, lse = pl.pallas_call(
        _attn_kernel,
        grid=(B, d, rows // qr),
        in_specs=[cur(KV_WIDTH), prev, cur(KV_WIDTH), prev, cur(KV_WIDTH),
                  pl.BlockSpec(bias.shape, lambda b, r, n: (0, 0, 0, 0))],
        out_specs=(cur(KV_WIDTH), cur(LANES)),
        out_shape=(jax.ShapeDtypeStruct((B, rows, d * KV_WIDTH), BF16),
                   jax.ShapeDtypeStruct((B, rows, d * LANES), F32)),
        scratch_shapes=[pltpu.VMEM((q_blocks, ATT_HEADS, BAND, 2 * BAND), F32),
                        pltpu.VMEM((q_blocks, ATT_HEADS, BAND, LANES), F32)],
        compiler_params=_params(("arbitrary", "arbitrary", "arbitrary")),
        name=f"attn_d{d}",
    )(view(q), view(k), view(k), view(v), view(v), bias)
    return o, lse


def _head_expand():
    e = np.zeros((LANES, KV_WIDTH), np.float32)
    for h in range(ATT_HEADS):
        e[(h % 2) * HEAD_DIM + h // 2, h * HEAD_DIM:(h + 1) * HEAD_DIM] = 1.0
    return np.concatenate([e, e], axis=0)


def _mix_kernel(x_ref, a_ref, ah_ref, o0_ref, o1_ref, o2_ref, l0_ref, l1_ref, l2_ref, ex_ref, gate_ref,
                wp_ref, ps_ref, wo_ref, g2_ref, wq_ref, sk_ref,
                h_ref, hn_ref, sc_ref, ext_ref, merged_ref, onat_ref, lnat_ref):
    tm = x_ref.shape[1]
    i = pl.program_id(1)

    def natural_rows(src_ref, d, slabs, dst_ref):
        n = src_ref.shape[1]
        for r in range(d):
            for s in range(slabs):
                col = (r * slabs + s) * LANES
                dst_ref[s, pl.ds(r, n, stride=d), :] = src_ref[0, :, col:col + LANES].astype(F32)

    for p, (o_ref, l_ref) in enumerate(((o1_ref, l1_ref), (o2_ref, l2_ref))):
        natural_rows(o_ref, DILATIONS[p + 1], KV_SLABS, onat_ref.at[p])
        natural_rows(l_ref, DILATIONS[p + 1], 1, lnat_ref.at[p])

    def attn_cols(p, g):
        if p == 0:
            return o0_ref[0, :, g * POOL_OUT:(g + 1) * POOL_OUT].astype(F32)
        return jnp.concatenate([onat_ref[p - 1, 2 * g], onat_ref[p - 1, 2 * g + 1]], axis=1)

    lses = [l0_ref[0], lnat_ref[0, 0], lnat_ref[1, 0]]
    top = jnp.maximum(jnp.maximum(lses[0], lses[1]), lses[2])
    es = [jnp.exp(l - top) for l in lses]
    tot = es[0] + es[1] + es[2]
    w_split = []
    for e in es:
        w = e / tot
        hi = w.astype(BF16)
        w_split.append(jnp.concatenate([hi, (w - hi.astype(F32)).astype(BF16)], axis=1))
    ext_ref[0:POOL_HALO, :] = jnp.where(i == 0, 0.0, ah_ref[0])
    ext_ref[POOL_HALO:POOL_HALO + tm, :] = a_ref[0]
    t = i * tm + lax.broadcasted_iota(jnp.int32, (tm, 1), 0)
    for g, w in enumerate(POOL_WINDOWS):
        cols = slice(g * POOL_IN, (g + 1) * POOL_IN)
        wsum = ext_ref[POOL_HALO:POOL_HALO + tm, cols]
        for j in range(1, w):
            wsum = wsum + ext_ref[POOL_HALO - j:POOL_HALO - j + tm, cols]
        count = jnp.minimum(t + 1, w).astype(F32)
        dlt = (wsum / count - a_ref[0, :, cols]).astype(BF16)
        ocol = slice(g * POOL_OUT, (g + 1) * POOL_OUT)
        pool = _dot(dlt, wp_ref[g]) * ps_ref[:, ocol]
        attn = jnp.zeros((tm, POOL_OUT), F32)
        for p, w2 in enumerate(w_split):
            attn = attn + _dot(w2, ex_ref[:, ocol]) * attn_cols(p, g)
        merged = (gate_ref[0, :, ocol].astype(F32) * pool
                  + gate_ref[0, :, D_MODEL + g * POOL_OUT:D_MODEL + (g + 1) * POOL_OUT].astype(F32) * attn)
        merged_ref[:, ocol] = merged.astype(BF16)
    h = x_ref[0] + _dot(merged_ref[...], wo_ref[...])
    h_ref[0] = h
    hn = (h * lax.rsqrt(jnp.mean(h * h, axis=-1, keepdims=True) + EPS) * g2_ref[...]).astype(BF16)
    hn_ref[0] = hn
    qt = _dot_nt(wq_ref[...], hn).astype(BF16)
    for g in range(PEER_GROUPS):
        head, half = divmod(g, 2)
        rows = slice(g * PEER_NKEYS, (g + 1) * PEER_NKEYS)
        sc = _dot(sk_ref[g], qt[rows, :])
        for j in range(tm // LANES):
            sc_ref[j, pl.ds(half * PEER_HEADS * PEER_NKEYS + head, PEER_NKEYS, stride=PEER_HEADS), :] = (
                sc[:, j * LANES:(j + 1) * LANES])


def _mix(x, a, attn_outs, attn_lses, expand, gates, w_pool, pool_scale, w_out, g2, wq_t, sub_keys, tm):
    B, S, _ = x.shape
    T = B * S
    nt = S // tm
    halo_per_tile = tm // POOL_HALO
    tile = lambda w: pl.BlockSpec((1, tm, w), lambda b, i: (b, i, 0))
    dil = lambda d, w: pl.BlockSpec((1, tm // d, d * w), lambda b, i: (b, i, 0))
    halo = pl.BlockSpec((1, POOL_HALO, POOL_WIDTH), lambda b, i: (b, jnp.maximum(i * halo_per_tile - 1, 0), 0))
    once = lambda shape: pl.BlockSpec(shape, lambda b, i: (0,) * len(shape), pipeline_mode=pl.Buffered(1))
    return pl.pallas_call(
        _mix_kernel,
        grid=(B, nt),
        in_specs=[tile(D_MODEL), tile(POOL_WIDTH), halo,
                  *[dil(d, KV_WIDTH) for d in DILATIONS], *[dil(d, LANES) for d in DILATIONS],
                  once(expand.shape), tile(2 * D_MODEL),
                  once(w_pool.shape), once((1, D_MODEL)), once((D_MODEL, D_MODEL)), once((1, D_MODEL)),
                  once(wq_t.shape), once(sub_keys.shape)],
        out_specs=(tile(D_MODEL), tile(D_MODEL),
                   pl.BlockSpec((tm // LANES, PEER_GROUPS * PEER_NKEYS, LANES), lambda b, i: (b * nt + i, 0, 0))),
        out_shape=(jax.ShapeDtypeStruct((B, S, D_MODEL), F32),
                   jax.ShapeDtypeStruct((B, S, D_MODEL), BF16),
                   jax.ShapeDtypeStruct((T // LANES, PEER_GROUPS * PEER_NKEYS, LANES), F32)),
        scratch_shapes=[pltpu.VMEM((POOL_HALO + tm, POOL_WIDTH), F32),
                        pltpu.VMEM((tm, D_MODEL), BF16),
                        pltpu.VMEM((N_PATTERNS - 1, KV_SLABS, tm, LANES), F32),
                        pltpu.VMEM((N_PATTERNS - 1, 1, tm, LANES), F32)],
        compiler_params=_params(("arbitrary", "arbitrary")),
        name="mix",
    )(x, a, a, *attn_outs, *attn_lses, expand, gates, w_pool, pool_scale, w_out, g2, wq_t, sub_keys)


_PAIRS = frozenset((a, b) for a in range(PEER_TOPK) for b in range(PEER_TOPK) if (a + 1) * (b + 1) <= PEER_TOPK)


def _oddeven_merge_sort_pairs(n):
    pairs = []

    def merge(lo, hi, r):
        step = r * 2
        if step < hi - lo:
            merge(lo, hi, step)
            merge(lo + r, hi, step)
            pairs.extend((i, i + r) for i in range(lo + r, hi - r, step))
        else:
            pairs.append((lo, lo + r))

    def sort(lo, hi):
        if hi - lo >= 1:
            mid = lo + (hi - lo) // 2
            sort(lo, mid)
            sort(mid + 1, hi)
            merge(lo, hi, 1)

    sort(0, n - 1)
    return tuple(pairs)


_SORT_PAIRS = _oddeven_merge_sort_pairs(PEER_TOPK)
LISTS_PER_HALF = PEER_NKEYS // PEER_TOPK


def _precedes(va, ia, vb, ib):
    if isinstance(ia, float) and isinstance(ib, float):
        return (va >= vb) if ia < ib else (va > vb)
    return (va > vb) | ((va == vb) & (ia < ib))


def _compare_exchange(a, b):
    (va, ia), (vb, ib) = a, b
    a_first = _precedes(va, ia, vb, ib)
    return ((jnp.maximum(va, vb), jnp.where(a_first, ia, ib)),
            (jnp.minimum(va, vb), jnp.where(a_first, ib, ia)))


def _sort_desc(items):
    items = list(items)
    for i, j in _SORT_PAIRS:
        items[i], items[j] = _compare_exchange(items[i], items[j])
    return items


def _bitonic_sort_desc(items):
    items = list(items)
    n = len(items)
    span = n // 2
    while span:
        for k in range(n):
            if not k & span:
                items[k], items[k + span] = _compare_exchange(items[k], items[k + span])
        span //= 2
    return items


def _merge_keep_top(a, b):
    n = len(a)
    out = []
    for k in range(n):
        if n - 1 - k >= len(b):
            out.append(a[k])
            continue
        (va, ia), (vb, ib) = a[k], b[n - 1 - k]
        out.append((jnp.maximum(va, vb), jnp.where(_precedes(va, ia, vb, ib), ia, ib)))
    return _bitonic_sort_desc(out)


def _topk_kernel(sc_ref, isel_ref, jsel_ref, g_ref, av_ref, ai_ref, bv_ref, bi_ref, ts_ref, ti_ref,
                 e_ref, s_ref):
    def sort_group(g, carry):
        base = pl.multiple_of(g * PEER_TOPK * PEER_HEADS, PEER_TOPK * PEER_HEADS)
        items = [(sc_ref[0, pl.ds(base + k * PEER_HEADS, PEER_HEADS), :], float(k)) for k in range(PEER_TOPK)]
        first_key = ((g % LISTS_PER_HALF) * PEER_TOPK).astype(F32)
        for r, (v, i) in enumerate(_sort_desc(items)):
            av_ref[g, r] = v
            ai_ref[g, r] = i + first_key
        return carry

    lax.fori_loop(0, 2 * LISTS_PER_HALF, sort_group, 0)

    def merge_level(src_v, src_i, dst_v, dst_i, n_out):
        def body(m, carry):
            lists = [[(src_v[2 * m + side, r], src_i[2 * m + side, r]) for r in range(PEER_TOPK)]
                     for side in range(2)]
            for r, (v, i) in enumerate(_merge_keep_top(*lists)):
                dst_v[m, r] = v
                dst_i[m, r] = i
            return carry

        lax.fori_loop(0, n_out, body, 0)

    merge_level(av_ref, ai_ref, bv_ref, bi_ref, LISTS_PER_HALF)
    merge_level(bv_ref, bi_ref, av_ref, ai_ref, LISTS_PER_HALF // 2)
    merge_level(av_ref, ai_ref, ts_ref, ti_ref, 2)

    def cand(a, b):
        expert = ti_ref[0, a] * float(PEER_NKEYS) + ti_ref[1, b]
        return ts_ref[0, a] + ts_ref[1, b], expert + float((a * PEER_TOPK + b) * PEER_N)

    by_row = {a: [cand(a, b) for b in range(PEER_TOPK) if (a, b) in _PAIRS] for a in range(PEER_TOPK)}
    single = [by_row[a][0] for a in range(8, PEER_TOPK)]
    second = _bitonic_sort_desc(by_row[1] + single[::-1])
    third = _sort_desc(by_row[2] + by_row[3] + by_row[4] + by_row[5] + by_row[6])
    third = _merge_keep_top(third, by_row[7])
    best = _merge_keep_top(_merge_keep_top(by_row[0], second), third)
    for r, (v, key) in enumerate(best):
        s_ref[r] = v
        e_ref[r] = key - jnp.floor(key * (1.0 / PEER_N)) * float(PEER_N)

    best = s_ref[...]
    ex = jnp.exp(best - best[0:1])
    gate = ex / jnp.sum(ex, axis=0, keepdims=True)
    e = e_ref[...].reshape(PEER_SEL, LANES).T
    i_sel = jnp.floor(e * (1.0 / PEER_NKEYS))
    isel_ref[...] = i_sel
    jsel_ref[...] = e - i_sel * PEER_NKEYS
    g_ref[...] = gate.reshape(PEER_SEL, LANES).T


def _topk(sc_t):
    n_tiles = sc_t.shape[0]
    out = pl.BlockSpec((LANES, PEER_SEL), lambda i: (i, 0))
    shp = jax.ShapeDtypeStruct((n_tiles * LANES, PEER_SEL), F32)
    vreg = (PEER_HEADS, LANES)
    return pl.pallas_call(
        _topk_kernel,
        grid=(n_tiles,),
        in_specs=[pl.BlockSpec((1, PEER_GROUPS * PEER_NKEYS, LANES), lambda i: (i, 0, 0))],
        out_specs=(out, out, out),
        out_shape=(shp, shp, shp),
        scratch_shapes=[pltpu.VMEM((2 * LISTS_PER_HALF, PEER_TOPK) + vreg, F32),
                        pltpu.VMEM((2 * LISTS_PER_HALF, PEER_TOPK) + vreg, F32),
                        pltpu.VMEM((LISTS_PER_HALF, PEER_TOPK) + vreg, F32),
                        pltpu.VMEM((LISTS_PER_HALF, PEER_TOPK) + vreg, F32),
                        pltpu.VMEM((2, PEER_TOPK) + vreg, F32),
                        pltpu.VMEM((2, PEER_TOPK) + vreg, F32),
                        pltpu.VMEM((PEER_TOPK,) + vreg, F32),
                        pltpu.VMEM((PEER_TOPK,) + vreg, F32)],
        compiler_params=_params(("arbitrary",)),
        name="topk",
    )(sc_t)


G_TILE = 128
G_PITCH = G_TILE + SUBLANES // 2


def _gbuild_kernel(isel_ref, jsel_ref, g_ref, out_ref, tile_ref):
    sub = lax.broadcasted_iota(jnp.int32, (PEER_NKEYS, PEER_SEL), 0).astype(F32)

    def body(c, carry):
        irow = isel_ref[pl.ds(c, 1), :]
        jrow = jsel_ref[pl.ds(c, 1), :]
        grow = g_ref[pl.ds(c, 1), :]
        a = jnp.where(sub == irow, grow, 0.0).astype(BF16)
        b = jnp.where(sub == jrow, 1.0, 0.0).astype(BF16)
        tile_ref[pl.ds(c, PEER_NKEYS, stride=G_PITCH), :] = _dot_nt(a, b)
        return carry

    lax.fori_loop(0, G_TILE, body, 0, unroll=True)
    for i in range(PEER_NKEYS):
        out_ref[i] = tile_ref[i * G_PITCH:i * G_PITCH + G_TILE, :].astype(BF16)


def _gbuild(isel, jsel, gates):
    T = isel.shape[0]
    row = pl.BlockSpec((G_TILE, PEER_SEL), lambda t: (t, 0))
    return pl.pallas_call(
        _gbuild_kernel,
        grid=(T // G_TILE,),
        in_specs=[row, row, row],
        out_specs=pl.BlockSpec((PEER_NKEYS, G_TILE, PEER_NKEYS), lambda t: (0, t, 0)),
        out_shape=jax.ShapeDtypeStruct((PEER_NKEYS, T, PEER_NKEYS), BF16),
        scratch_shapes=[pltpu.VMEM((PEER_NKEYS * G_PITCH, PEER_NKEYS), F32)],
        compiler_params=_params(("arbitrary",)),
        name="gbuild",
    )(isel, jsel, gates)


def _gelu_tanh(x):
    return 0.5 * x * (1.0 + jnp.tanh(math.sqrt(2.0 / math.pi) * (x + 0.044715 * (x * x * x))))


def _peer_kernel(hn_ref, u_ref, v_ref, g_ref, h_ref, out_ref, w_ref):
    e = pl.program_id(1)
    n_rows = g_ref.shape[0]

    @pl.when(e == 0)
    def _():
        out_ref[...] = h_ref[...]

    hn = hn_ref[...]
    for i in range(0, n_rows, 2):
        cols = slice(i * PEER_NKEYS, (i + 2) * PEER_NKEYS)
        act = _gelu_tanh(_dot_nt(hn, u_ref[cols, :]))
        gate = jnp.concatenate([g_ref[i], g_ref[i + 1]], axis=1).astype(F32)
        w_ref[:, cols] = (act * gate).astype(BF16)
    out_ref[...] += _dot(w_ref[...], v_ref[...])


def _peer(hn2, h2, u, v, gmat, tt, eb):
    T = hn2.shape[0]
    n_rows = eb // PEER_NKEYS
    tok = lambda dt: pl.BlockSpec((tt, D_MODEL), lambda t, e: (t, 0))
    exp = pl.BlockSpec((eb, D_MODEL), lambda t, e: (e, 0))
    return pl.pallas_call(
        _peer_kernel,
        grid=(T // tt, PEER_N // eb),
        in_specs=[tok(BF16), exp, exp,
                  pl.BlockSpec((n_rows, tt, PEER_NKEYS), lambda t, e: (e, t, 0)),
                  tok(F32)],
        out_specs=pl.BlockSpec((tt, D_MODEL), lambda t, e: (t, 0)),
        out_shape=jax.ShapeDtypeStruct((T, D_MODEL), F32),
        scratch_shapes=[pltpu.VMEM((tt, eb), BF16)],
        compiler_params=_params(("arbitrary", "arbitrary")),
        name="peer",
    )(hn2, u, v, gmat, h2)


def _layer(h, norm1_g, w_in, q_norm_g, k_norm_g, w_pool, pool_scale, w_out, norm2_g,
           w_query, sub_keys, expert_u, expert_v, biases):
    B, S, D = h.shape
    T = B * S
    tm = min(512, S)
    head = np.arange(NORM_COLS) // HEAD_DIM
    bd = jnp.asarray(head[:, None] == head[None, :], BF16)
    qg = (jnp.tile(q_norm_g, ATT_HEADS) * (HEAD_DIM ** -0.5)).reshape(1, KV_WIDTH)
    kg = jnp.tile(k_norm_g, ATT_HEADS).reshape(1, KV_WIDTH)

    res = _proj(h.reshape(T, D), norm1_g.reshape(1, D), w_in.astype(BF16), bd, qg, kg, tm)
    a, qs, ks, vs, gates = res[0], res[1:4], res[4:7], res[7:10], res[10]
    attn_outs, attn_lses = [], []
    for g, dilation in enumerate(DILATIONS):
        o, lse = _attn_pattern(qs[g], ks[g], vs[g], biases[g], dilation, B)
        attn_outs.append(o)
        attn_lses.append(lse)

    wq_t = w_query.T.astype(BF16)
    sk = sub_keys.reshape(PEER_GROUPS, PEER_NKEYS, PEER_HALF).astype(BF16)
    h_new, hn, sc_t = _mix(h, a.reshape(B, S, POOL_WIDTH), attn_outs, attn_lses,
                           jnp.asarray(_head_expand(), BF16), gates.reshape(B, S, 2 * D),
                           w_pool.astype(BF16), pool_scale.reshape(1, D), w_out.astype(BF16),
                           norm2_g.reshape(1, D), wq_t, sk, tm)
    isel, jsel, gsel = _topk(sc_t)
    gmat = _gbuild(isel, jsel, gsel)
    out = _peer(hn.reshape(T, D), h_new.reshape(T, D), expert_u.astype(BF16), expert_v.astype(BF16),
                gmat, min(1024, T), 2048)
    return out.reshape(B, S, D)


def kernel(x, norm1_g, w_in, q_norm_g, k_norm_g, w_pool, pool_scale, w_out, norm2_g, w_query, sub_keys,
           expert_u, expert_v):
    slopes = _alibi_slopes(N_PATTERNS * ATT_HEADS).reshape(N_PATTERNS, ATT_HEADS)
    biases = [jnp.asarray(_attn_bias(slopes[g], dil)) for g, (_, dil) in enumerate(DILATED_PATTERNS)]
    h = x
    for l in range(norm1_g.shape[0]):
        h = _layer(h, norm1_g[l], w_in[l], q_norm_g[l], k_norm_g[l], w_pool[l], pool_scale[l], w_out[l],
                   norm2_g[l], w_query[l], sub_keys[l], expert_u[l], expert_v[l], biases)
    return h
```

```python
import functools
import math

import numpy as np
import jax
import jax.numpy as jnp
from jax import lax
from jax.experimental import pallas as pl
from jax.experimental.pallas import tpu as pltpu

D_MODEL = 1024
EPS = 1e-6
NEG_INF = -1e30

POOL_WINDOWS = (2, 4, 8, 16)
POOL_GROUPS = 4
POOL_IN = 128
POOL_WIDTH = 512
POOL_OUT = 256
POOL_HALO = 16

ATT_HEADS = 16
HEAD_DIM = 64
DILATED_PATTERNS = ((128, 1), (512, 4), (2048, 16))
N_PATTERNS = 3
BAND = 128
KV_WIDTH = 1024
Q_OFF = POOL_WIDTH
K_OFF = Q_OFF + N_PATTERNS * KV_WIDTH
V_OFF = K_OFF + KV_WIDTH
GATE_OFF = V_OFF + KV_WIDTH
IN_WIDTH = GATE_OFF + 2 * D_MODEL

PEER_HEADS = 8
PEER_NKEYS = 128
PEER_N = PEER_NKEYS * PEER_NKEYS
PEER_HALF = 128
PEER_TOPK = 16
PEER_GROUPS = 2 * PEER_HEADS
PEER_SEL = PEER_HEADS * PEER_TOPK

LANES = 128
SUBLANES = 8
VMEM_LIMIT = 56 * 1024 * 1024

BF16 = jnp.bfloat16
F32 = jnp.float32


def _dot(a, b):
    return jnp.dot(a, b, preferred_element_type=F32)


def _dot_nt(a, b):
    return lax.dot_general(a, b, (((1,), (1,)), ((), ())), preferred_element_type=F32)


def _params(semantics):
    return pltpu.CompilerParams(dimension_semantics=semantics, vmem_limit_bytes=VMEM_LIMIT)


def _const_spec(shape):
    n = len(shape)
    return pl.BlockSpec(shape, lambda *_: (0,) * n)


KV_SLABS = KV_WIDTH // LANES
NORM_COLS = 256
DILATIONS = tuple(d for _, d in DILATED_PATTERNS)


def _store_dilated(y, slab_ref, out_refs):
    tm = y.shape[0]
    for d, ref in out_refs:
        if d == 1:
            ref[...] = y.astype(BF16)
    if all(d == 1 for d, _ in out_refs):
        return
    for s in range(KV_SLABS):
        slab_ref[s] = y[:, s * LANES:(s + 1) * LANES]
    for d, ref in out_refs:
        if d == 1:
            continue
        for r in range(d):
            for s in range(KV_SLABS):
                col = r * KV_WIDTH + s * LANES
                ref[:, col:col + LANES] = slab_ref[s, pl.ds(r, tm // d, stride=d), :].astype(BF16)


def _proj_kernel(x_ref, g1_ref, w_ref, bd_ref, qg_ref, kg_ref,
                 a_ref, q0_ref, q1_ref, q2_ref, k0_ref, k1_ref, k2_ref, v0_ref, v1_ref, v2_ref, gate_ref,
                 slab_ref):
    x = x_ref[...]
    inv = lax.rsqrt(jnp.mean(x * x, axis=-1, keepdims=True) + EPS)
    xn = (x * inv * g1_ref[...]).astype(BF16)

    def head_norm(y, gain):
        parts = []
        for c in range(0, KV_WIDTH, NORM_COLS):
            yc = y[:, c:c + NORM_COLS]
            ss = _dot((yc * yc).astype(BF16), bd_ref[...])
            parts.append(yc * lax.rsqrt(ss * (1.0 / HEAD_DIM) + EPS))
        return jnp.concatenate(parts, axis=1) * gain

    a_ref[...] = _dot(xn, w_ref[:, 0:POOL_WIDTH])
    for g, q_ref in enumerate((q0_ref, q1_ref, q2_ref)):
        lo = Q_OFF + g * KV_WIDTH
        q = head_norm(_dot(xn, w_ref[:, lo:lo + KV_WIDTH]), qg_ref[...])
        _store_dilated(q, slab_ref, [(DILATIONS[g], q_ref)])
    k = head_norm(_dot(xn, w_ref[:, K_OFF:K_OFF + KV_WIDTH]), kg_ref[...])
    _store_dilated(k, slab_ref, list(zip(DILATIONS, (k0_ref, k1_ref, k2_ref))))
    v = _dot(xn, w_ref[:, V_OFF:V_OFF + KV_WIDTH])
    _store_dilated(v, slab_ref, list(zip(DILATIONS, (v0_ref, v1_ref, v2_ref))))
    gate_ref[...] = jax.nn.sigmoid(_dot(xn, w_ref[:, GATE_OFF:IN_WIDTH])).astype(BF16)


def _proj(x2, g1, w_in, bd, qg, kg, tm):
    T = x2.shape[0]
    row = lambda w: pl.BlockSpec((tm, w), lambda i: (i, 0))
    dil = lambda d: pl.BlockSpec((tm // d, d * KV_WIDTH), lambda i: (i, 0))
    dil_shape = lambda d: jax.ShapeDtypeStruct((T // d, d * KV_WIDTH), BF16)
    once = lambda shape: pl.BlockSpec(shape, lambda i: (0, 0), pipeline_mode=pl.Buffered(1))
    views = [dil(d) for d in DILATIONS]
    view_shapes = [dil_shape(d) for d in DILATIONS]
    return pl.pallas_call(
        _proj_kernel,
        grid=(T // tm,),
        in_specs=[row(D_MODEL), once((1, D_MODEL)), once((D_MODEL, IN_WIDTH)),
                  once((NORM_COLS, NORM_COLS)), once((1, KV_WIDTH)), once((1, KV_WIDTH))],
        out_specs=(row(POOL_WIDTH), *views, *views, *views, row(2 * D_MODEL)),
        out_shape=(jax.ShapeDtypeStruct((T, POOL_WIDTH), F32), *view_shapes, *view_shapes, *view_shapes,
                   jax.ShapeDtypeStruct((T, 2 * D_MODEL), BF16)),
        scratch_shapes=[pltpu.VMEM((KV_SLABS, tm, LANES), F32)],
        compiler_params=_params(("arbitrary",)),
        name="proj",
    )(x2, g1, w_in, bd, qg, kg)


def _alibi_slopes(n):
    def geometric(k):
        start = 2.0 ** (-8.0 / k)
        return [start ** (i + 1) for i in range(k)]
    p = 2 ** int(math.floor(math.log2(n)))
    s = geometric(p) + geometric(2 * p)[0::2][: n - p]
    return np.sort(np.array(s, dtype=np.float32))[::-1].copy()


def _attn_bias(slopes, dilation):
    qi = np.arange(BAND)[:, None]
    ki = np.arange(2 * BAND)[None, :]
    step = qi + BAND - ki
    valid = (step >= 0) & (step <= BAND)
    alibi = -slopes[:, None, None] * (step * dilation).astype(np.float32)[None]
    general = np.where(valid[None], alibi, np.float32(NEG_INF)).astype(np.float32)
    first = np.where((ki >= BAND)[None], general, np.float32(NEG_INF)).astype(np.float32)
    return np.stack([general, first])


HEAD_PAIRS = ATT_HEADS // 2


def _attn_kernel(q_ref, kp_ref, kc_ref, vp_ref, vc_ref, bias_ref, o_ref, lse_ref, s_ref, m_ref):
    q_blocks = q_ref.shape[1] // BAND
    first_step = (pl.program_id(2) == 0).astype(jnp.int32)
    lane = lax.broadcasted_iota(jnp.int32, (BAND, LANES), 1)
    low = lane < HEAD_DIM
    low_b = (lax.broadcasted_iota(jnp.int32, (1, LANES), 1) < HEAD_DIM).astype(BF16)
    lane_sel = (low_b, 1 - low_b)

    for blk in range(q_blocks):
        rows = slice(blk * BAND, (blk + 1) * BAND)
        if blk == 0:
            k_prev, v_prev, first = kp_ref, vp_ref, first_step
            before = slice(0, BAND)
        else:
            k_prev, v_prev, first = kc_ref, vc_ref, 0
            before = slice((blk - 1) * BAND, blk * BAND)

        for h in range(ATT_HEADS):
            cols = slice((h // 2) * LANES, (h // 2 + 1) * LANES)
            qm = q_ref[0, rows, cols] * lane_sel[h % 2]
            sp = _dot_nt(qm, k_prev[0, before, cols]) + bias_ref[first, h, :, 0:BAND]
            sc = _dot_nt(qm, kc_ref[0, rows, cols]) + bias_ref[first, h, :, BAND:2 * BAND]
            s_ref[blk, h, :, 0:BAND] = sp
            s_ref[blk, h, :, BAND:2 * BAND] = sc
            m_ref[blk, h] = jnp.broadcast_to(jnp.max(jnp.maximum(sp, sc), axis=-1, keepdims=True), (BAND, LANES))

        lse_tile = jnp.zeros((BAND, LANES), F32)
        for j in range(HEAD_PAIRS):
            cols = slice(j * LANES, (j + 1) * LANES)
            res = []
            for half in range(2):
                h = 2 * j + half
                own, other = lane_sel[half], lane_sel[1 - half]
                pp = jnp.exp(s_ref[blk, h, :, 0:BAND] - m_ref[blk, h]).astype(BF16)
                pc = jnp.exp(s_ref[blk, h, :, BAND:2 * BAND] - m_ref[blk, h]).astype(BF16)
                res.append(_dot(pp, v_prev[0, before, cols] * own + other)
                           + _dot(pc, vc_ref[0, rows, cols] * own + other))
            num = jnp.where(low, res[0], res[1])
            den = pltpu.roll(jnp.where(low, res[1], res[0]), HEAD_DIM, axis=1)
            o_ref[0, rows, cols] = (num / den).astype(BF16)
            lse = jnp.where(low, m_ref[blk, 2 * j], m_ref[blk, 2 * j + 1]) + jnp.log(den)
            lse_tile = jnp.where((lane == j) | (lane == HEAD_DIM + j), lse, lse_tile)
        lse_ref[0, rows, :] = lse_tile


def _attn_pattern(q, k, v, bias, dilation, B):
    d = dilation
    rows = q.shape[0] // B
    q_blocks = max(n for n in (1, 2, 4) if rows % (n * BAND) == 0)
    qr = q_blocks * BAND
    view = lambda a: a.reshape(B, rows, d * KV_WIDTH)
    cur = lambda w: pl.BlockSpec((1, qr, w), lambda b, r, n: (b, n, r))
    prev = pl.BlockSpec((1, BAND, KV_WIDTH), lambda b, r, n: (b, jnp.maximum(q_blocks * n - 1, 0), r))
    o, lse = pl.pallas_call(
        _attn_kernel,
        grid=(B, d, rows // qr),
        in_specs=[cur(KV_WIDTH), prev, cur(KV_WIDTH), prev, cur(KV_WIDTH),
                  pl.BlockSpec(bias.shape, lambda b, r, n: (0, 0, 0, 0))],
        out_specs=(cur(KV_WIDTH), cur(LANES)),
        out_shape=(jax.ShapeDtypeStruct((B, rows, d * KV_WIDTH), BF16),
                   jax.ShapeDtypeStruct((B, rows, d * LANES), F32)),
        scratch_shapes=[pltpu.VMEM((q_blocks, ATT_HEADS, BAND, 2 * BAND), F32),
                        pltpu.VMEM((q_blocks, ATT_HEADS, BAND, LANES), F32)],
        compiler_params=_params(("arbitrary", "arbitrary", "arbitrary")),
        name=f"attn_d{d}",
    )(view(q), view(k), view(k), view(v), view(v), bias)
    return o, lse


def _head_expand():
    e = np.zeros((LANES, KV_WIDTH), np.float32)
    for h in range(ATT_HEADS):
        e[(h % 2) * HEAD_DIM + h // 2, h * HEAD_DIM:(h + 1) * HEAD_DIM] = 1.0
    return np.concatenate([e, e], axis=0)


def _mix_kernel(x_ref, a_ref, ah_ref, o0_ref, o1_ref, o2_ref, l0_ref, l1_ref, l2_ref, ex_ref, gate_ref,
                wp_ref, ps_ref, wo_ref, g2_ref, wq_ref, sk_ref,
                h_ref, hn_ref, sc_ref, ext_ref, merged_ref, onat_ref, lnat_ref):
    tm = x_ref.shape[1]
    i = pl.program_id(1)

    def natural_rows(src_ref, d, slabs, dst_ref):
        n = src_ref.shape[1]
        for r in range(d):
            for s in range(slabs):
                col = (r * slabs + s) * LANES
                dst_ref[s, pl.ds(r, n, stride=d), :] = src_ref[0, :, col:col + LANES].astype(F32)

    for p, (o_ref, l_ref) in enumerate(((o1_ref, l1_ref), (o2_ref, l2_ref))):
        natural_rows(o_ref, DILATIONS[p + 1], KV_SLABS, onat_ref.at[p])
        natural_rows(l_ref, DILATIONS[p + 1], 1, lnat_ref.at[p])

    def attn_cols(p, g):
        if p == 0:
            return o0_ref[0, :, g * POOL_OUT:(g + 1) * POOL_OUT].astype(F32)
        return jnp.concatenate([onat_ref[p - 1, 2 * g], onat_ref[p - 1, 2 * g + 1]], axis=1)

    lses = [l0_ref[0], lnat_ref[0, 0], lnat_ref[1, 0]]
    top = jnp.maximum(jnp.maximum(lses[0], lses[1]), lses[2])
    es = [jnp.exp(l - top) for l in lses]
    tot = es[0] + es[1] + es[2]
    w_split = []
    for e in es:
        w = e / tot
        hi = w.astype(BF16)
        w_split.append(jnp.concatenate([hi, (w - hi.astype(F32)).astype(BF16)], axis=1))
    ext_ref[0:POOL_HALO, :] = jnp.where(i == 0, 0.0, ah_ref[0])
    ext_ref[POOL_HALO:POOL_HALO + tm, :] = a_ref[0]
    t = i * tm + lax.broadcasted_iota(jnp.int32, (tm, 1), 0)
    for g, w in enumerate(POOL_WINDOWS):
        cols = slice(g * POOL_IN, (g + 1) * POOL_IN)
        wsum = ext_ref[POOL_HALO:POOL_HALO + tm, cols]
        for j in range(1, w):
            wsum = wsum + ext_ref[POOL_HALO - j:POOL_HALO - j + tm, cols]
        count = jnp.minimum(t + 1, w).astype(F32)
        dlt = (wsum / count - a_ref[0, :, cols]).astype(BF16)
        ocol = slice(g * POOL_OUT, (g + 1) * POOL_OUT)
        pool = _dot(dlt, wp_ref[g]) * ps_ref[:, ocol]
        attn = jnp.zeros((tm, POOL_OUT), F32)
        for p, w2 in enumerate(w_split):
            attn = attn + _dot(w2, ex_ref[:, ocol]) * attn_cols(p, g)
        merged = (gate_ref[0, :, ocol].astype(F32) * pool
                  + gate_ref[0, :, D_MODEL + g * POOL_OUT:D_MODEL + (g + 1) * POOL_OUT].astype(F32) * attn)
        merged_ref[:, ocol] = merged.astype(BF16)
    h = x_ref[0] + _dot(merged_ref[...], wo_ref[...])
    h_ref[0] = h
    hn = (h * lax.rsqrt(jnp.mean(h * h, axis=-1, keepdims=True) + EPS) * g2_ref[...]).astype(BF16)
    hn_ref[0] = hn
    qt = _dot_nt(wq_ref[...], hn).astype(BF16)
    for g in range(PEER_GROUPS):
        head, half = divmod(g, 2)
        rows = slice(g * PEER_NKEYS, (g + 1) * PEER_NKEYS)
        sc = _dot(sk_ref[g], qt[rows, :])
        for j in range(tm // LANES):
            sc_ref[j, pl.ds(half * PEER_HEADS * PEER_NKEYS + head, PEER_NKEYS, stride=PEER_HEADS), :] = (
                sc[:, j * LANES:(j + 1) * LANES])


def _mix(x, a, attn_outs, attn_lses, expand, gates, w_pool, pool_scale, w_out, g2, wq_t, sub_keys, tm):
    B, S, _ = x.shape
    T = B * S
    nt = S // tm
    halo_per_tile = tm // POOL_HALO
    tile = lambda w: pl.BlockSpec((1, tm, w), lambda b, i: (b, i, 0))
    dil = lambda d, w: pl.BlockSpec((1, tm // d, d * w), lambda b, i: (b, i, 0))
    halo = pl.BlockSpec((1, POOL_HALO, POOL_WIDTH), lambda b, i: (b, jnp.maximum(i * halo_per_tile - 1, 0), 0))
    once = lambda shape: pl.BlockSpec(shape, lambda b, i: (0,) * len(shape), pipeline_mode=pl.Buffered(1))
    return pl.pallas_call(
        _mix_kernel,
        grid=(B, nt),
        in_specs=[tile(D_MODEL), tile(POOL_WIDTH), halo,
                  *[dil(d, KV_WIDTH) for d in DILATIONS], *[dil(d, LANES) for d in DILATIONS],
                  once(expand.shape), tile(2 * D_MODEL),
                  once(w_pool.shape), once((1, D_MODEL)), once((D_MODEL, D_MODEL)), once((1, D_MODEL)),
                  once(wq_t.shape), once(sub_keys.shape)],
        out_specs=(tile(D_MODEL), tile(D_MODEL),
                   pl.BlockSpec((tm // LANES, PEER_GROUPS * PEER_NKEYS, LANES), lambda b, i: (b * nt + i, 0, 0))),
        out_shape=(jax.ShapeDtypeStruct((B, S, D_MODEL), F32),
                   jax.ShapeDtypeStruct((B, S, D_MODEL), BF16),
                   jax.ShapeDtypeStruct((T // LANES, PEER_GROUPS * PEER_NKEYS, LANES), F32)),
        scratch_shapes=[pltpu.VMEM((POOL_HALO + tm, POOL_WIDTH), F32),
                        pltpu.VMEM((tm, D_MODEL), BF16),
                        pltpu.VMEM((N_PATTERNS - 1, KV_SLABS, tm, LANES), F32),
                        pltpu.VMEM((N_PATTERNS - 1, 1, tm, LANES), F32)],
        compiler_params=_params(("arbitrary", "arbitrary")),
        name="mix",
    )(x, a, a, *attn_outs, *attn_lses, expand, gates, w_pool, pool_scale, w_out, g2, wq_t, sub_keys)


_PAIRS = frozenset((a, b) for a in range(PEER_TOPK) for b in range(PEER_TOPK) if (a + 1) * (b + 1) <= PEER_TOPK)


def _oddeven_merge_sort_pairs(n):
    pairs = []

    def merge(lo, hi, r):
        step = r * 2
        if step < hi - lo:
            merge(lo, hi, step)
            merge(lo + r, hi, step)
            pairs.extend((i, i + r) for i in range(lo + r, hi - r, step))
        else:
            pairs.append((lo, lo + r))

    def sort(lo, hi):
        if hi - lo >= 1:
            mid = lo + (hi - lo) // 2
            sort(lo, mid)
            sort(mid + 1, hi)
            merge(lo, hi, 1)

    sort(0, n - 1)
    return tuple(pairs)


_SORT_PAIRS = _oddeven_merge_sort_pairs(PEER_TOPK)
LISTS_PER_HALF = PEER_NKEYS // PEER_TOPK


def _precedes(va, ia, vb, ib):
    if isinstance(ia, float) and isinstance(ib, float):
        return (va >= vb) if ia < ib else (va > vb)
    return (va > vb) | ((va == vb) & (ia < ib))


def _compare_exchange(a, b):
    (va, ia), (vb, ib) = a, b
    a_first = _precedes(va, ia, vb, ib)
    return ((jnp.maximum(va, vb), jnp.where(a_first, ia, ib)),
            (jnp.minimum(va, vb), jnp.where(a_first, ib, ia)))


def _sort_desc(items):
    items = list(items)
    for i, j in _SORT_PAIRS:
        items[i], items[j] = _compare_exchange(items[i], items[j])
    return items


def _bitonic_sort_desc(items):
    items = list(items)
    n = len(items)
    span = n // 2
    while span:
        for k in range(n):
            if not k & span:
                items[k], items[k + span] = _compare_exchange(items[k], items[k + span])
        span //= 2
    return items


def _merge_keep_top(a, b):
    n = len(a)
    out = []
    for k in range(n):
        if n - 1 - k >= len(b):
            out.append(a[k])
            continue
        (va, ia), (vb, ib) = a[k], b[n - 1 - k]
        out.append((jnp.maximum(va, vb), jnp.where(_precedes(va, ia, vb, ib), ia, ib)))
    return _bitonic_sort_desc(out)


def _topk_kernel(sc_ref, isel_ref, jsel_ref, g_ref, av_ref, ai_ref, bv_ref, bi_ref, ts_ref, ti_ref,
                 e_ref, s_ref):
    def sort_group(g, carry):
        base = pl.multiple_of(g * PEER_TOPK * PEER_HEADS, PEER_TOPK * PEER_HEADS)
        items = [(sc_ref[0, pl.ds(base + k * PEER_HEADS, PEER_HEADS), :], float(k)) for k in range(PEER_TOPK)]
        first_key = ((g % LISTS_PER_HALF) * PEER_TOPK).astype(F32)
        for r, (v, i) in enumerate(_sort_desc(items)):
            av_ref[g, r] = v
            ai_ref[g, r] = i + first_key
        return carry

    lax.fori_loop(0, 2 * LISTS_PER_HALF, sort_group, 0)

    def merge_level(src_v, src_i, dst_v, dst_i, n_out):
        def body(m, carry):
            lists = [[(src_v[2 * m + side, r], src_i[2 * m + side, r]) for r in range(PEER_TOPK)]
                     for side in range(2)]
            for r, (v, i) in enumerate(_merge_keep_top(*lists)):
                dst_v[m, r] = v
                dst_i[m, r] = i
            return carry

        lax.fori_loop(0, n_out, body, 0)

    merge_level(av_ref, ai_ref, bv_ref, bi_ref, LISTS_PER_HALF)
    merge_level(bv_ref, bi_ref, av_ref, ai_ref, LISTS_PER_HALF // 2)
    merge_level(av_ref, ai_ref, ts_ref, ti_ref, 2)

    def cand(a, b):
        expert = ti_ref[0, a] * float(PEER_NKEYS) + ti_ref[1, b]
        return ts_ref[0, a] + ts_ref[1, b], expert + float((a * PEER_TOPK + b) * PEER_N)

    by_row = {a: [cand(a, b) for b in range(PEER_TOPK) if (a, b) in _PAIRS] for a in range(PEER_TOPK)}
    single = [by_row[a][0] for a in range(8, PEER_TOPK)]
    second = _bitonic_sort_desc(by_row[1] + single[::-1])
    third = _sort_desc(by_row[2] + by_row[3] + by_row[4] + by_row[5] + by_row[6])
    third = _merge_keep_top(third, by_row[7])
    best = _merge_keep_top(_merge_keep_top(by_row[0], second), third)
    for r, (v, key) in enumerate(best):
        s_ref[r] = v
        e_ref[r] = key - jnp.floor(key * (1.0 / PEER_N)) * float(PEER_N)

    best = s_ref[...]
    ex = jnp.exp(best - best[0:1])
    gate = ex / jnp.sum(ex, axis=0, keepdims=True)
    e = e_ref[...].reshape(PEER_SEL, LANES)
    i_sel = jnp.floor(e * (1.0 / PEER_NKEYS))
    isel_ref[...] = i_sel.T
    jsel_ref[0] = e - i_sel * PEER_NKEYS
    g_ref[...] = gate.reshape(PEER_SEL, LANES).T


def _topk(sc_t):
    n_tiles = sc_t.shape[0]
    out = pl.BlockSpec((LANES, PEER_SEL), lambda i: (i, 0))
    shp = jax.ShapeDtypeStruct((n_tiles * LANES, PEER_SEL), F32)
    out_t = pl.BlockSpec((1, PEER_SEL, LANES), lambda i: (i, 0, 0))
    shp_t = jax.ShapeDtypeStruct((n_tiles, PEER_SEL, LANES), F32)
    vreg = (PEER_HEADS, LANES)
    return pl.pallas_call(
        _topk_kernel,
        grid=(n_tiles,),
        in_specs=[pl.BlockSpec((1, PEER_GROUPS * PEER_NKEYS, LANES), lambda i: (i, 0, 0))],
        out_specs=(out, out_t, out),
        out_shape=(shp, shp_t, shp),
        scratch_shapes=[pltpu.VMEM((2 * LISTS_PER_HALF, PEER_TOPK) + vreg, F32),
                        pltpu.VMEM((2 * LISTS_PER_HALF, PEER_TOPK) + vreg, F32),
                        pltpu.VMEM((LISTS_PER_HALF, PEER_TOPK) + vreg, F32),
                        pltpu.VMEM((LISTS_PER_HALF, PEER_TOPK) + vreg, F32),
                        pltpu.VMEM((2, PEER_TOPK) + vreg, F32),
                        pltpu.VMEM((2, PEER_TOPK) + vreg, F32),
                        pltpu.VMEM((PEER_TOPK,) + vreg, F32),
                        pltpu.VMEM((PEER_TOPK,) + vreg, F32)],
        compiler_params=_params(("arbitrary",)),
        name="topk",
    )(sc_t)


G_TILE = 128
G_PITCH = G_TILE + SUBLANES // 2


def _gbuild_kernel(isel_ref, jsel_ref, g_ref, out_ref, tile_ref):
    sub = lax.broadcasted_iota(jnp.int32, (PEER_NKEYS, PEER_SEL), 0).astype(F32)
    lane = lax.broadcasted_iota(jnp.int32, (PEER_SEL, PEER_NKEYS), 1).astype(F32).astype(BF16)
    one, zero = jnp.ones((), BF16), jnp.zeros((), BF16)
    for c in range(G_TILE):
        irow = isel_ref[c:c + 1, :]
        grow = g_ref[c:c + 1, :]
        jcol = jsel_ref[0, :, c:c + 1].astype(BF16)
        a = jnp.where(sub == irow, grow, 0.0).astype(BF16)
        b = jnp.where(lane == jcol, one, zero)
        tile_ref[pl.ds(c, PEER_NKEYS, stride=G_PITCH), :] = _dot(a, b)
    for i in range(PEER_NKEYS):
        out_ref[i] = tile_ref[i * G_PITCH:i * G_PITCH + G_TILE, :].astype(BF16)


def _gbuild(isel, jsel, gates):
    T = isel.shape[0]
    row = pl.BlockSpec((G_TILE, PEER_SEL), lambda t: (t, 0))
    return pl.pallas_call(
        _gbuild_kernel,
        grid=(T // G_TILE,),
        in_specs=[row, pl.BlockSpec((1, PEER_SEL, G_TILE), lambda t: (t, 0, 0)), row],
        out_specs=pl.BlockSpec((PEER_NKEYS, G_TILE, PEER_NKEYS), lambda t: (0, t, 0)),
        out_shape=jax.ShapeDtypeStruct((PEER_NKEYS, T, PEER_NKEYS), BF16),
        scratch_shapes=[pltpu.VMEM((PEER_NKEYS * G_PITCH, PEER_NKEYS), F32)],
        compiler_params=_params(("arbitrary",)),
        name="gbuild",
    )(isel, jsel, gates)


def _gelu_tanh(x):
    return 0.5 * x * (1.0 + jnp.tanh(math.sqrt(2.0 / math.pi) * (x + 0.044715 * (x * x * x))))


def _peer_kernel(hn_ref, u_ref, v_ref, g_ref, h_ref, out_ref, w_ref):
    e = pl.program_id(1)
    n_rows = g_ref.shape[0]

    @pl.when(e == 0)
    def _():
        out_ref[...] = h_ref[...]

    hn = hn_ref[...]
    for i in range(0, n_rows, 2):
        cols = slice(i * PEER_NKEYS, (i + 2) * PEER_NKEYS)
        act = _gelu_tanh(_dot_nt(hn, u_ref[cols, :]))
        gate = jnp.concatenate([g_ref[i], g_ref[i + 1]], axis=1).astype(F32)
        w_ref[:, cols] = (act * gate).astype(BF16)
    out_ref[...] += _dot(w_ref[...], v_ref[...])


def _peer(hn2, h2, u, v, gmat, tt, eb):
    T = hn2.shape[0]
    n_rows = eb // PEER_NKEYS
    tok = lambda dt: pl.BlockSpec((tt, D_MODEL), lambda t, e: (t, 0))
    exp = pl.BlockSpec((eb, D_MODEL), lambda t, e: (e, 0))
    return pl.pallas_call(
        _peer_kernel,
        grid=(T // tt, PEER_N // eb),
        in_specs=[tok(BF16), exp, exp,
                  pl.BlockSpec((n_rows, tt, PEER_NKEYS), lambda t, e: (e, t, 0)),
                  tok(F32)],
        out_specs=pl.BlockSpec((tt, D_MODEL), lambda t, e: (t, 0)),
        out_shape=jax.ShapeDtypeStruct((T, D_MODEL), F32),
        scratch_shapes=[pltpu.VMEM((tt, eb), BF16)],
        compiler_params=_params(("arbitrary", "arbitrary")),
        name="peer",
    )(hn2, u, v, gmat, h2)


def _layer(h, norm1_g, w_in, q_norm_g, k_norm_g, w_pool, pool_scale, w_out, norm2_g,
           w_query, sub_keys, expert_u, expert_v, biases):
    B, S, D = h.shape
    T = B * S
    tm = min(512, S)
    head = np.arange(NORM_COLS) // HEAD_DIM
    bd = jnp.asarray(head[:, None] == head[None, :], BF16)
    qg = (jnp.tile(q_norm_g, ATT_HEADS) * (HEAD_DIM ** -0.5)).reshape(1, KV_WIDTH)
    kg = jnp.tile(k_norm_g, ATT_HEADS).reshape(1, KV_WIDTH)

    res = _proj(h.reshape(T, D), norm1_g.reshape(1, D), w_in.astype(BF16), bd, qg, kg, tm)
    a, qs, ks, vs, gates = res[0], res[1:4], res[4:7], res[7:10], res[10]
    attn_outs, attn_lses = [], []
    for g, dilation in enumerate(DILATIONS):
        o, lse = _attn_pattern(qs[g], ks[g], vs[g], biases[g], dilation, B)
        attn_outs.append(o)
        attn_lses.append(lse)

    wq_t = w_query.T.astype(BF16)
    sk = sub_keys.reshape(PEER_GROUPS, PEER_NKEYS, PEER_HALF).astype(BF16)
    h_new, hn, sc_t = _mix(h, a.reshape(B, S, POOL_WIDTH), attn_outs, attn_lses,
                           jnp.asarray(_head_expand(), BF16), gates.reshape(B, S, 2 * D),
                           w_pool.astype(BF16), pool_scale.reshape(1, D), w_out.astype(BF16),
                           norm2_g.reshape(1, D), wq_t, sk, tm)
    isel, jsel, gsel = _topk(sc_t)
    gmat = _gbuild(isel, jsel, gsel)
    out = _peer(hn.reshape(T, D), h_new.reshape(T, D), expert_u.astype(BF16), expert_v.astype(BF16),
                gmat, min(1024, T), 2048)
    return out.reshape(B, S, D)


def kernel(x, norm1_g, w_in, q_norm_g, k_norm_g, w_pool, pool_scale, w_out, norm2_g, w_query, sub_keys,
           expert_u, expert_v):
    slopes = _alibi_slopes(N_PATTERNS * ATT_HEADS).reshape(N_PATTERNS, ATT_HEADS)
    biases = [jnp.asarray(_attn_bias(slopes[g], dil)) for g, (_, dil) in enumerate(DILATED_PATTERNS)]
    h = x
    for l in range(norm1_g.shape[0]):
        h = _layer(h, norm1_g[l], w_in[l], q_norm_g[l], k_norm_g[l], w_pool[l], pool_scale[l], w_out[l],
                   norm2_g[l], w_query[l], sub_keys[l], expert_u[l], expert_v[l], biases)
    return h
```

```python
import math

import numpy as np
import jax
import jax.numpy as jnp
from jax import lax
from jax.experimental import pallas as pl
from jax.experimental.pallas import tpu as pltpu

D_MODEL = 1024
EPS = 1e-6
NEG_INF = -1e30

POOL_WINDOWS = (2, 4, 8, 16)
POOL_GROUPS = 4
POOL_IN = 128
POOL_WIDTH = 512
POOL_OUT = 256
POOL_HALO = 16

ATT_HEADS = 16
HEAD_DIM = 64
DILATED_PATTERNS = ((128, 1), (512, 4), (2048, 16))
N_PATTERNS = 3
BAND = 128
KV_WIDTH = 1024
Q_OFF = POOL_WIDTH
K_OFF = Q_OFF + N_PATTERNS * KV_WIDTH
V_OFF = K_OFF + KV_WIDTH
GATE_OFF = V_OFF + KV_WIDTH
IN_WIDTH = GATE_OFF + 2 * D_MODEL

PEER_HEADS = 8
PEER_NKEYS = 128
PEER_N = PEER_NKEYS * PEER_NKEYS
PEER_HALF = 128
PEER_TOPK = 16
PEER_GROUPS = 2 * PEER_HEADS
PEER_SEL = PEER_HEADS * PEER_TOPK

LANES = 128
SUBLANES = 8
VMEM_LIMIT = 56 * 1024 * 1024

TOKEN_TILE = 512
PEER_TOKEN_TILE = 1024
PEER_EXPERT_BLOCK = 2048


def _tile_sizes(seq_len, n_tokens):
    return min(TOKEN_TILE, seq_len), min(PEER_TOKEN_TILE, n_tokens)

BF16 = jnp.bfloat16
F32 = jnp.float32


def _dot(a, b):
    return jnp.dot(a, b, preferred_element_type=F32)


def _dot_nt(a, b):
    return lax.dot_general(a, b, (((1,), (1,)), ((), ())), preferred_element_type=F32)


def _params(semantics):
    return pltpu.CompilerParams(dimension_semantics=semantics, vmem_limit_bytes=VMEM_LIMIT)


def _const_spec(shape):
    n = len(shape)
    return pl.BlockSpec(shape, lambda *_: (0,) * n)


KV_SLABS = KV_WIDTH // LANES
NORM_COLS = 256
DILATIONS = tuple(d for _, d in DILATED_PATTERNS)


def _store_dilated(y, slab_ref, out_refs):
    tm = y.shape[0]
    for d, ref in out_refs:
        if d == 1:
            ref[...] = y.astype(BF16)
    if all(d == 1 for d, _ in out_refs):
        return
    for s in range(KV_SLABS):
        slab_ref[s] = y[:, s * LANES:(s + 1) * LANES]
    for d, ref in out_refs:
        if d == 1:
            continue
        for r in range(d):
            for s in range(KV_SLABS):
                col = r * KV_WIDTH + s * LANES
                ref[:, col:col + LANES] = slab_ref[s, pl.ds(r, tm // d, stride=d), :].astype(BF16)


def _proj_kernel(x_ref, g1_ref, w_ref, bd_ref, qg_ref, kg_ref,
                 a_ref, q0_ref, q1_ref, q2_ref, k0_ref, k1_ref, k2_ref, v0_ref, v1_ref, v2_ref, gate_ref,
                 slab_ref):
    x = x_ref[...]
    inv = lax.rsqrt(jnp.mean(x * x, axis=-1, keepdims=True) + EPS)
    xn = (x * inv * g1_ref[...]).astype(BF16)

    def head_norm(y, gain):
        parts = []
        for c in range(0, KV_WIDTH, NORM_COLS):
            yc = y[:, c:c + NORM_COLS]
            ss = _dot((yc * yc).astype(BF16), bd_ref[...])
            parts.append(yc * lax.rsqrt(ss * (1.0 / HEAD_DIM) + EPS))
        return jnp.concatenate(parts, axis=1) * gain

    a_ref[...] = _dot(xn, w_ref[:, 0:POOL_WIDTH])
    for g, q_ref in enumerate((q0_ref, q1_ref, q2_ref)):
        lo = Q_OFF + g * KV_WIDTH
        q = head_norm(_dot(xn, w_ref[:, lo:lo + KV_WIDTH]), qg_ref[...])
        _store_dilated(q, slab_ref, [(DILATIONS[g], q_ref)])
    k = head_norm(_dot(xn, w_ref[:, K_OFF:K_OFF + KV_WIDTH]), kg_ref[...])
    _store_dilated(k, slab_ref, list(zip(DILATIONS, (k0_ref, k1_ref, k2_ref))))
    v = _dot(xn, w_ref[:, V_OFF:V_OFF + KV_WIDTH])
    _store_dilated(v, slab_ref, list(zip(DILATIONS, (v0_ref, v1_ref, v2_ref))))
    gate_ref[...] = jax.nn.sigmoid(_dot(xn, w_ref[:, GATE_OFF:IN_WIDTH])).astype(BF16)


def _proj(x2, g1, w_in, bd, qg, kg, tm):
    T = x2.shape[0]
    row = lambda w: pl.BlockSpec((tm, w), lambda i: (i, 0))
    dil = lambda d: pl.BlockSpec((tm // d, d * KV_WIDTH), lambda i: (i, 0))
    dil_shape = lambda d: jax.ShapeDtypeStruct((T // d, d * KV_WIDTH), BF16)
    once = lambda shape: pl.BlockSpec(shape, lambda i: (0, 0), pipeline_mode=pl.Buffered(1))
    views = [dil(d) for d in DILATIONS]
    view_shapes = [dil_shape(d) for d in DILATIONS]
    return pl.pallas_call(
        _proj_kernel,
        grid=(T // tm,),
        in_specs=[row(D_MODEL), once((1, D_MODEL)), once((D_MODEL, IN_WIDTH)),
                  once((NORM_COLS, NORM_COLS)), once((1, KV_WIDTH)), once((1, KV_WIDTH))],
        out_specs=(row(POOL_WIDTH), *views, *views, *views, row(2 * D_MODEL)),
        out_shape=(jax.ShapeDtypeStruct((T, POOL_WIDTH), F32), *view_shapes, *view_shapes, *view_shapes,
                   jax.ShapeDtypeStruct((T, 2 * D_MODEL), BF16)),
        scratch_shapes=[pltpu.VMEM((KV_SLABS, tm, LANES), F32)],
        compiler_params=_params(("arbitrary",)),
        name="proj",
    )(x2, g1, w_in, bd, qg, kg)


def _alibi_slopes(n):
    def geometric(k):
        start = 2.0 ** (-8.0 / k)
        return [start ** (i + 1) for i in range(k)]
    p = 2 ** int(math.floor(math.log2(n)))
    s = geometric(p) + geometric(2 * p)[0::2][: n - p]
    return np.sort(np.array(s, dtype=np.float32))[::-1].copy()


def _attn_bias(slopes, dilation):
    qi = np.arange(BAND)[:, None]
    ki = np.arange(2 * BAND)[None, :]
    step = qi + BAND - ki
    valid = (step >= 0) & (step <= BAND)
    alibi = -slopes[:, None, None] * (step * dilation).astype(np.float32)[None]
    general = np.where(valid[None], alibi, np.float32(NEG_INF)).astype(np.float32)
    first = np.where((ki >= BAND)[None], general, np.float32(NEG_INF)).astype(np.float32)
    return np.stack([general, first])


HEAD_PAIRS = ATT_HEADS // 2


def _attn_kernel(q_ref, kp_ref, kc_ref, vp_ref, vc_ref, bias_ref, o_ref, lse_ref, s_ref, m_ref, kt_ref):
    q_blocks = q_ref.shape[1] // BAND
    first_step = (pl.program_id(2) == 0).astype(jnp.int32)
    lane = lax.broadcasted_iota(jnp.int32, (BAND, LANES), 1)
    low = lane < HEAD_DIM
    low_b = (lax.broadcasted_iota(jnp.int32, (1, LANES), 1) < HEAD_DIM).astype(BF16)
    lane_sel = (low_b, 1 - low_b)

    for j in range(HEAD_PAIRS):
        cols = slice(j * LANES, (j + 1) * LANES)
        kt_ref[0, cols, :] = kp_ref[0, :, cols].T
        for blk in range(q_blocks):
            kt_ref[blk + 1, cols, :] = kc_ref[0, blk * BAND:(blk + 1) * BAND, cols].T

    for blk in range(q_blocks):
        rows = slice(blk * BAND, (blk + 1) * BAND)
        if blk == 0:
            v_prev, first = vp_ref, first_step
            before = slice(0, BAND)
        else:
            v_prev, first = vc_ref, 0
            before = slice((blk - 1) * BAND, blk * BAND)

        for h in range(ATT_HEADS):
            cols = slice((h // 2) * LANES, (h // 2 + 1) * LANES)
            qm = q_ref[0, rows, cols] * lane_sel[h % 2]
            sp = _dot(qm, kt_ref[blk, cols, :]) + bias_ref[first, h, :, 0:BAND]
            sc = _dot(qm, kt_ref[blk + 1, cols, :]) + bias_ref[first, h, :, BAND:2 * BAND]
            s_ref[blk, h, :, 0:BAND] = sp
            s_ref[blk, h, :, BAND:2 * BAND] = sc
            m_ref[blk, h] = jnp.broadcast_to(jnp.max(jnp.maximum(sp, sc), axis=-1, keepdims=True), (BAND, LANES))

        lse_tile = jnp.zeros((BAND, LANES), F32)
        for j in range(HEAD_PAIRS):
            cols = slice(j * LANES, (j + 1) * LANES)
            res = []
            for half in range(2):
                h = 2 * j + half
                own, other = lane_sel[half], lane_sel[1 - half]
                pp = jnp.exp(s_ref[blk, h, :, 0:BAND] - m_ref[blk, h]).astype(BF16)
                pc = jnp.exp(s_ref[blk, h, :, BAND:2 * BAND] - m_ref[blk, h]).astype(BF16)
                res.append(_dot(pp, v_prev[0, before, cols] * own + other)
                           + _dot(pc, vc_ref[0, rows, cols] * own + other))
            num = jnp.where(low, res[0], res[1])
            den = pltpu.roll(jnp.where(low, res[1], res[0]), HEAD_DIM, axis=1)
            o_ref[0, rows, cols] = (num / den).astype(BF16)
            lse = jnp.where(low, m_ref[blk, 2 * j], m_ref[blk, 2 * j + 1]) + jnp.log(den)
            lse_tile = jnp.where((lane == j) | (lane == HEAD_DIM + j), lse, lse_tile)
        lse_ref[0, rows, :] = lse_tile


def _attn_pattern(q, k, v, bias, dilation, B):
    d = dilation
    rows = q.shape[0] // B
    q_blocks = max(n for n in (1, 2, 4) if rows % (n * BAND) == 0)
    qr = q_blocks * BAND
    view = lambda a: a.reshape(B, rows, d * KV_WIDTH)
    cur = lambda w: pl.BlockSpec((1, qr, w), lambda b, r, n: (b, n, r))
    prev = pl.BlockSpec((1, BAND, KV_WIDTH), lambda b, r, n: (b, jnp.maximum(q_blocks * n - 1, 0), r))
    o, lse = pl.pallas_call(
        _attn_kernel,
        grid=(B, d, rows // qr),
        in_specs=[cur(KV_WIDTH), prev, cur(KV_WIDTH), prev, cur(KV_WIDTH),
                  pl.BlockSpec(bias.shape, lambda b, r, n: (0, 0, 0, 0))],
        out_specs=(cur(KV_WIDTH), cur(LANES)),
        out_shape=(jax.ShapeDtypeStruct((B, rows, d * KV_WIDTH), BF16),
                   jax.ShapeDtypeStruct((B, rows, d * LANES), F32)),
        scratch_shapes=[pltpu.VMEM((q_blocks, ATT_HEADS, BAND, 2 * BAND), F32),
                        pltpu.VMEM((q_blocks, ATT_HEADS, BAND, LANES), F32),
                        pltpu.VMEM((q_blocks + 1, KV_WIDTH, BAND), BF16)],
        compiler_params=_params(("arbitrary", "arbitrary", "arbitrary")),
        name=f"attn_d{d}",
    )(view(q), view(k), view(k), view(v), view(v), bias)
    return o, lse


def _head_expand():
    e = np.zeros((LANES, KV_WIDTH), np.float32)
    for h in range(ATT_HEADS):
        e[(h % 2) * HEAD_DIM + h // 2, h * HEAD_DIM:(h + 1) * HEAD_DIM] = 1.0
    return np.concatenate([e, e], axis=0)


def _mix_kernel(x_ref, a_ref, ah_ref, o0_ref, o1_ref, o2_ref, l0_ref, l1_ref, l2_ref, ex_ref, gate_ref,
                wp_ref, ps_ref, wo_ref, g2_ref, wq_ref, sk_ref,
                h_ref, hn_ref, sc_ref, ext_ref, merged_ref, onat_ref, lnat_ref):
    tm = x_ref.shape[1]
    i = pl.program_id(1)

    def natural_rows(src_ref, d, slabs, dst_ref):
        n = src_ref.shape[1]
        for r in range(d):
            for s in range(slabs):
                col = (r * slabs + s) * LANES
                dst_ref[s, pl.ds(r, n, stride=d), :] = src_ref[0, :, col:col + LANES].astype(F32)

    for p, (o_ref, l_ref) in enumerate(((o1_ref, l1_ref), (o2_ref, l2_ref))):
        natural_rows(o_ref, DILATIONS[p + 1], KV_SLABS, onat_ref.at[p])
        natural_rows(l_ref, DILATIONS[p + 1], 1, lnat_ref.at[p])

    def attn_cols(p, g):
        if p == 0:
            return o0_ref[0, :, g * POOL_OUT:(g + 1) * POOL_OUT].astype(F32)
        return jnp.concatenate([onat_ref[p - 1, 2 * g], onat_ref[p - 1, 2 * g + 1]], axis=1)

    lses = [l0_ref[0], lnat_ref[0, 0], lnat_ref[1, 0]]
    top = jnp.maximum(jnp.maximum(lses[0], lses[1]), lses[2])
    es = [jnp.exp(l - top) for l in lses]
    tot = es[0] + es[1] + es[2]
    w_split = []
    for e in es:
        w = e / tot
        hi = w.astype(BF16)
        w_split.append(jnp.concatenate([hi, (w - hi.astype(F32)).astype(BF16)], axis=1))
    ext_ref[0:POOL_HALO, :] = jnp.where(i == 0, 0.0, ah_ref[0])
    ext_ref[POOL_HALO:POOL_HALO + tm, :] = a_ref[0]
    t = i * tm + lax.broadcasted_iota(jnp.int32, (tm, 1), 0)
    for g, w in enumerate(POOL_WINDOWS):
        cols = slice(g * POOL_IN, (g + 1) * POOL_IN)
        wsum = ext_ref[POOL_HALO:POOL_HALO + tm, cols]
        for j in range(1, w):
            wsum = wsum + ext_ref[POOL_HALO - j:POOL_HALO - j + tm, cols]
        count = jnp.minimum(t + 1, w).astype(F32)
        dlt = (wsum / count - a_ref[0, :, cols]).astype(BF16)
        ocol = slice(g * POOL_OUT, (g + 1) * POOL_OUT)
        pool = _dot(dlt, wp_ref[g]) * ps_ref[:, ocol]
        attn = jnp.zeros((tm, POOL_OUT), F32)
        for p, w2 in enumerate(w_split):
            attn = attn + _dot(w2, ex_ref[:, ocol]) * attn_cols(p, g)
        merged = (gate_ref[0, :, ocol].astype(F32) * pool
                  + gate_ref[0, :, D_MODEL + g * POOL_OUT:D_MODEL + (g + 1) * POOL_OUT].astype(F32) * attn)
        merged_ref[:, ocol] = merged.astype(BF16)
    h = x_ref[0] + _dot(merged_ref[...], wo_ref[...])
    h_ref[0] = h
    hn = (h * lax.rsqrt(jnp.mean(h * h, axis=-1, keepdims=True) + EPS) * g2_ref[...]).astype(BF16)
    hn_ref[0] = hn
    qt = _dot_nt(wq_ref[...], hn).astype(BF16)
    for g in range(PEER_GROUPS):
        head, half = divmod(g, 2)
        rows = slice(g * PEER_NKEYS, (g + 1) * PEER_NKEYS)
        sc = _dot(sk_ref[g], qt[rows, :])
        for j in range(tm // LANES):
            sc_ref[j, pl.ds(half * PEER_HEADS * PEER_NKEYS + head, PEER_NKEYS, stride=PEER_HEADS), :] = (
                sc[:, j * LANES:(j + 1) * LANES])


def _mix(x, a, attn_outs, attn_lses, expand, gates, w_pool, pool_scale, w_out, g2, wq_t, sub_keys, tm):
    B, S, _ = x.shape
    T = B * S
    nt = S // tm
    halo_per_tile = tm // POOL_HALO
    tile = lambda w: pl.BlockSpec((1, tm, w), lambda b, i: (b, i, 0))
    dil = lambda d, w: pl.BlockSpec((1, tm // d, d * w), lambda b, i: (b, i, 0))
    halo = pl.BlockSpec((1, POOL_HALO, POOL_WIDTH), lambda b, i: (b, jnp.maximum(i * halo_per_tile - 1, 0), 0))
    once = lambda shape: pl.BlockSpec(shape, lambda b, i: (0,) * len(shape), pipeline_mode=pl.Buffered(1))
    return pl.pallas_call(
        _mix_kernel,
        grid=(B, nt),
        in_specs=[tile(D_MODEL), tile(POOL_WIDTH), halo,
                  *[dil(d, KV_WIDTH) for d in DILATIONS], *[dil(d, LANES) for d in DILATIONS],
                  once(expand.shape), tile(2 * D_MODEL),
                  once(w_pool.shape), once((1, D_MODEL)), once((D_MODEL, D_MODEL)), once((1, D_MODEL)),
                  once(wq_t.shape), once(sub_keys.shape)],
        out_specs=(tile(D_MODEL), tile(D_MODEL),
                   pl.BlockSpec((tm // LANES, PEER_GROUPS * PEER_NKEYS, LANES), lambda b, i: (b * nt + i, 0, 0))),
        out_shape=(jax.ShapeDtypeStruct((B, S, D_MODEL), F32),
                   jax.ShapeDtypeStruct((B, S, D_MODEL), BF16),
                   jax.ShapeDtypeStruct((T // LANES, PEER_GROUPS * PEER_NKEYS, LANES), F32)),
        scratch_shapes=[pltpu.VMEM((POOL_HALO + tm, POOL_WIDTH), F32),
                        pltpu.VMEM((tm, D_MODEL), BF16),
                        pltpu.VMEM((N_PATTERNS - 1, KV_SLABS, tm, LANES), F32),
                        pltpu.VMEM((N_PATTERNS - 1, 1, tm, LANES), F32)],
        compiler_params=_params(("arbitrary", "arbitrary")),
        name="mix",
    )(x, a, a, *attn_outs, *attn_lses, expand, gates, w_pool, pool_scale, w_out, g2, wq_t, sub_keys)


_PAIRS = frozenset((a, b) for a in range(PEER_TOPK) for b in range(PEER_TOPK) if (a + 1) * (b + 1) <= PEER_TOPK)


def _oddeven_merge_sort_pairs(n):
    pairs = []

    def merge(lo, hi, r):
        step = r * 2
        if step < hi - lo:
            merge(lo, hi, step)
            merge(lo + r, hi, step)
            pairs.extend((i, i + r) for i in range(lo + r, hi - r, step))
        else:
            pairs.append((lo, lo + r))

    def sort(lo, hi):
        if hi - lo >= 1:
            mid = lo + (hi - lo) // 2
            sort(lo, mid)
            sort(mid + 1, hi)
            merge(lo, hi, 1)

    sort(0, n - 1)
    return tuple(pairs)


_SORT_PAIRS = _oddeven_merge_sort_pairs(PEER_TOPK)
LISTS_PER_HALF = PEER_NKEYS // PEER_TOPK


def _precedes(va, ia, vb, ib):
    if isinstance(ia, float) and isinstance(ib, float):
        return (va >= vb) if ia < ib else (va > vb)
    return (va > vb) | ((va == vb) & (ia < ib))


def _compare_exchange(a, b):
    (va, ia), (vb, ib) = a, b
    a_first = _precedes(va, ia, vb, ib)
    return ((jnp.maximum(va, vb), jnp.where(a_first, ia, ib)),
            (jnp.minimum(va, vb), jnp.where(a_first, ib, ia)))


def _sort_desc(items):
    items = list(items)
    for i, j in _SORT_PAIRS:
        items[i], items[j] = _compare_exchange(items[i], items[j])
    return items


def _bitonic_sort_desc(items):
    items = list(items)
    n = len(items)
    span = n // 2
    while span:
        for k in range(n):
            if not k & span:
                items[k], items[k + span] = _compare_exchange(items[k], items[k + span])
        span //= 2
    return items


def _merge_keep_top(a, b):
    n = len(a)
    out = []
    for k in range(n):
        if n - 1 - k >= len(b):
            out.append(a[k])
            continue
        (va, ia), (vb, ib) = a[k], b[n - 1 - k]
        out.append((jnp.maximum(va, vb), jnp.where(_precedes(va, ia, vb, ib), ia, ib)))
    return _bitonic_sort_desc(out)


def _topk_kernel(sc_ref, isel_ref, jsel_ref, g_ref, av_ref, ai_ref, bv_ref, bi_ref, ts_ref, ti_ref,
                 e_ref, s_ref):
    def sort_group(g, carry):
        base = pl.multiple_of(g * PEER_TOPK * PEER_HEADS, PEER_TOPK * PEER_HEADS)
        items = [(sc_ref[0, pl.ds(base + k * PEER_HEADS, PEER_HEADS), :], float(k)) for k in range(PEER_TOPK)]
        first_key = ((g % LISTS_PER_HALF) * PEER_TOPK).astype(F32)
        for r, (v, i) in enumerate(_sort_desc(items)):
            av_ref[g, r] = v
            ai_ref[g, r] = i + first_key
        return carry

    lax.fori_loop(0, 2 * LISTS_PER_HALF, sort_group, 0)

    def merge_level(src_v, src_i, dst_v, dst_i, n_out):
        def body(m, carry):
            lists = [[(src_v[2 * m + side, r], src_i[2 * m + side, r]) for r in range(PEER_TOPK)]
                     for side in range(2)]
            for r, (v, i) in enumerate(_merge_keep_top(*lists)):
                dst_v[m, r] = v
                dst_i[m, r] = i
            return carry

        lax.fori_loop(0, n_out, body, 0)

    merge_level(av_ref, ai_ref, bv_ref, bi_ref, LISTS_PER_HALF)
    merge_level(bv_ref, bi_ref, av_ref, ai_ref, LISTS_PER_HALF // 2)
    merge_level(av_ref, ai_ref, ts_ref, ti_ref, 2)

    def cand(a, b):
        expert = ti_ref[0, a] * float(PEER_NKEYS) + ti_ref[1, b]
        return ts_ref[0, a] + ts_ref[1, b], expert + float((a * PEER_TOPK + b) * PEER_N)

    by_row = {a: [cand(a, b) for b in range(PEER_TOPK) if (a, b) in _PAIRS] for a in range(PEER_TOPK)}
    single = [by_row[a][0] for a in range(8, PEER_TOPK)]
    second = _bitonic_sort_desc(by_row[1] + single[::-1])
    third = _sort_desc(by_row[2] + by_row[3] + by_row[4] + by_row[5] + by_row[6])
    third = _merge_keep_top(third, by_row[7])
    best = _merge_keep_top(_merge_keep_top(by_row[0], second), third)
    for r, (v, key) in enumerate(best):
        s_ref[r] = v
        e_ref[r] = key - jnp.floor(key * (1.0 / PEER_N)) * float(PEER_N)

    best = s_ref[...]
    ex = jnp.exp(best - best[0:1])
    gate = ex / jnp.sum(ex, axis=0, keepdims=True)
    e = e_ref[...].reshape(PEER_SEL, LANES)
    i_sel = jnp.floor(e * (1.0 / PEER_NKEYS))
    isel_ref[...] = i_sel.T
    jsel_ref[0] = e - i_sel * PEER_NKEYS
    g_ref[...] = gate.reshape(PEER_SEL, LANES).T


def _topk(sc_t):
    n_tiles = sc_t.shape[0]
    out = pl.BlockSpec((LANES, PEER_SEL), lambda i: (i, 0))
    shp = jax.ShapeDtypeStruct((n_tiles * LANES, PEER_SEL), F32)
    out_t = pl.BlockSpec((1, PEER_SEL, LANES), lambda i: (i, 0, 0))
    shp_t = jax.ShapeDtypeStruct((n_tiles, PEER_SEL, LANES), F32)
    vreg = (PEER_HEADS, LANES)
    return pl.pallas_call(
        _topk_kernel,
        grid=(n_tiles,),
        in_specs=[pl.BlockSpec((1, PEER_GROUPS * PEER_NKEYS, LANES), lambda i: (i, 0, 0))],
        out_specs=(out, out_t, out),
        out_shape=(shp, shp_t, shp),
        scratch_shapes=[pltpu.VMEM((2 * LISTS_PER_HALF, PEER_TOPK) + vreg, F32),
                        pltpu.VMEM((2 * LISTS_PER_HALF, PEER_TOPK) + vreg, F32),
                        pltpu.VMEM((LISTS_PER_HALF, PEER_TOPK) + vreg, F32),
                        pltpu.VMEM((LISTS_PER_HALF, PEER_TOPK) + vreg, F32),
                        pltpu.VMEM((2, PEER_TOPK) + vreg, F32),
                        pltpu.VMEM((2, PEER_TOPK) + vreg, F32),
                        pltpu.VMEM((PEER_TOPK,) + vreg, F32),
                        pltpu.VMEM((PEER_TOPK,) + vreg, F32)],
        compiler_params=_params(("arbitrary",)),
        name="topk",
    )(sc_t)


G_TILE = 128
G_PITCH = G_TILE + SUBLANES // 2


def _gbuild_kernel(isel_ref, jsel_ref, g_ref, out_ref, tile_ref):
    sub = lax.broadcasted_iota(jnp.int32, (PEER_NKEYS, PEER_SEL), 0).astype(F32)
    lane = lax.broadcasted_iota(jnp.int32, (PEER_SEL, PEER_NKEYS), 1).astype(F32).astype(BF16)
    one, zero = jnp.ones((), BF16), jnp.zeros((), BF16)
    for c in range(G_TILE):
        irow = isel_ref[c:c + 1, :]
        grow = g_ref[c:c + 1, :]
        jcol = jsel_ref[0, :, c:c + 1].astype(BF16)
        a = jnp.where(sub == irow, grow, 0.0).astype(BF16)
        b = jnp.where(lane == jcol, one, zero)
        tile_ref[pl.ds(c, PEER_NKEYS, stride=G_PITCH), :] = _dot(a, b)
    for i in range(PEER_NKEYS):
        out_ref[i] = tile_ref[i * G_PITCH:i * G_PITCH + G_TILE, :].astype(BF16)


def _gbuild(isel, jsel, gates):
    T = isel.shape[0]
    row = pl.BlockSpec((G_TILE, PEER_SEL), lambda t: (t, 0))
    return pl.pallas_call(
        _gbuild_kernel,
        grid=(T // G_TILE,),
        in_specs=[row, pl.BlockSpec((1, PEER_SEL, G_TILE), lambda t: (t, 0, 0)), row],
        out_specs=pl.BlockSpec((PEER_NKEYS, G_TILE, PEER_NKEYS), lambda t: (0, t, 0)),
        out_shape=jax.ShapeDtypeStruct((PEER_NKEYS, T, PEER_NKEYS), BF16),
        scratch_shapes=[pltpu.VMEM((PEER_NKEYS * G_PITCH, PEER_NKEYS), F32)],
        compiler_params=_params(("arbitrary",)),
        name="gbuild",
    )(isel, jsel, gates)


def _gelu_tanh(x):
    return 0.5 * x * (1.0 + jnp.tanh(math.sqrt(2.0 / math.pi) * (x + 0.044715 * (x * x * x))))


def _peer_kernel(hn_ref, u_ref, v_ref, g_ref, h_ref, out_ref, w_ref):
    e = pl.program_id(1)
    n_rows = g_ref.shape[0]

    @pl.when(e == 0)
    def _():
        out_ref[...] = h_ref[...]

    hn = hn_ref[...]
    for i in range(0, n_rows, 2):
        cols = slice(i * PEER_NKEYS, (i + 2) * PEER_NKEYS)
        act = _gelu_tanh(_dot(hn, u_ref[:, cols]))
        gate = jnp.concatenate([g_ref[i], g_ref[i + 1]], axis=1).astype(F32)
        w_ref[:, cols] = (act * gate).astype(BF16)
    out_ref[...] += _dot(w_ref[...], v_ref[...])


def _peer(hn2, h2, u_t, v, gmat, tt, eb):
    T = hn2.shape[0]
    n_rows = eb // PEER_NKEYS
    tok = lambda dt: pl.BlockSpec((tt, D_MODEL), lambda t, e: (t, 0))
    exp = pl.BlockSpec((eb, D_MODEL), lambda t, e: (e, 0))
    exp_t = pl.BlockSpec((D_MODEL, eb), lambda t, e: (0, e))
    return pl.pallas_call(
        _peer_kernel,
        grid=(T // tt, PEER_N // eb),
        in_specs=[tok(BF16), exp_t, exp,
                  pl.BlockSpec((n_rows, tt, PEER_NKEYS), lambda t, e: (e, t, 0)),
                  tok(F32)],
        out_specs=pl.BlockSpec((tt, D_MODEL), lambda t, e: (t, 0)),
        out_shape=jax.ShapeDtypeStruct((T, D_MODEL), F32),
        scratch_shapes=[pltpu.VMEM((tt, eb), BF16)],
        compiler_params=_params(("arbitrary", "arbitrary")),
        name="peer",
    )(hn2, u_t, v, gmat, h2)


def _layer(h, norm1_g, w_in, q_norm_g, k_norm_g, w_pool, pool_scale, w_out, norm2_g,
           w_query, sub_keys, expert_u, expert_v, biases):
    B, S, D = h.shape
    T = B * S
    tm, tt = _tile_sizes(S, T)
    assert D == D_MODEL and w_in.shape == (D_MODEL, IN_WIDTH), (h.shape, w_in.shape)
    assert S % (BAND * max(DILATIONS)) == 0, "every dilation pattern needs whole 128-key blocks"
    assert S % tm == 0 and tm % (2 * SUBLANES * max(DILATIONS)) == 0 and T % tt == 0 and tt % LANES == 0
    head = np.arange(NORM_COLS) // HEAD_DIM
    bd = jnp.asarray(head[:, None] == head[None, :], BF16)
    qg = (jnp.tile(q_norm_g, ATT_HEADS) * (HEAD_DIM ** -0.5)).reshape(1, KV_WIDTH)
    kg = jnp.tile(k_norm_g, ATT_HEADS).reshape(1, KV_WIDTH)

    res = _proj(h.reshape(T, D), norm1_g.reshape(1, D), w_in.astype(BF16), bd, qg, kg, tm)
    a, qs, ks, vs, gates = res[0], res[1:4], res[4:7], res[7:10], res[10]
    attn_outs, attn_lses = [], []
    for g, dilation in enumerate(DILATIONS):
        o, lse = _attn_pattern(qs[g], ks[g], vs[g], biases[g], dilation, B)
        attn_outs.append(o)
        attn_lses.append(lse)

    wq_t = w_query.T.astype(BF16)
    sk = sub_keys.reshape(PEER_GROUPS, PEER_NKEYS, PEER_HALF).astype(BF16)
    h_new, hn, sc_t = _mix(h, a.reshape(B, S, POOL_WIDTH), attn_outs, attn_lses,
                           jnp.asarray(_head_expand(), BF16), gates.reshape(B, S, 2 * D),
                           w_pool.astype(BF16), pool_scale.reshape(1, D), w_out.astype(BF16),
                           norm2_g.reshape(1, D), wq_t, sk, tm)
    isel, jsel, gsel = _topk(sc_t)
    gmat = _gbuild(isel, jsel, gsel)
    out = _peer(hn.reshape(T, D), h_new.reshape(T, D), expert_u.T.astype(BF16), expert_v.astype(BF16),
                gmat, tt, PEER_EXPERT_BLOCK)
    return out.reshape(B, S, D)


def kernel(x, norm1_g, w_in, q_norm_g, k_norm_g, w_pool, pool_scale, w_out, norm2_g, w_query, sub_keys,
           expert_u, expert_v):
    slopes = _alibi_slopes(N_PATTERNS * ATT_HEADS).reshape(N_PATTERNS, ATT_HEADS)
    biases = [jnp.asarray(_attn_bias(slopes[g], dil)) for g, (_, dil) in enumerate(DILATED_PATTERNS)]
    h = x
    for l in range(norm1_g.shape[0]):
        h = _layer(h, norm1_g[l], w_in[l], q_norm_g[l], k_norm_g[l], w_pool[l], pool_scale[l], w_out[l],
                   norm2_g[l], w_query[l], sub_keys[l], expert_u[l], expert_v[l], biases)
    return h
```

```python
import math

import numpy as np
import jax
import jax.numpy as jnp
from jax import lax
from jax.experimental import pallas as pl
from jax.experimental.pallas import tpu as pltpu

D_MODEL = 1024
EPS = 1e-6
NEG_INF = -1e30

POOL_WINDOWS = (2, 4, 8, 16)
POOL_GROUPS = 4
POOL_IN = 128
POOL_WIDTH = 512
POOL_OUT = 256
POOL_HALO = 16

ATT_HEADS = 16
HEAD_DIM = 64
DILATED_PATTERNS = ((128, 1), (512, 4), (2048, 16))
N_PATTERNS = 3
BAND = 128
KV_WIDTH = 1024
Q_OFF = POOL_WIDTH
K_OFF = Q_OFF + N_PATTERNS * KV_WIDTH
V_OFF = K_OFF + KV_WIDTH
GATE_OFF = V_OFF + KV_WIDTH
IN_WIDTH = GATE_OFF + 2 * D_MODEL

PEER_HEADS = 8
PEER_NKEYS = 128
PEER_N = PEER_NKEYS * PEER_NKEYS
PEER_HALF = 128
PEER_TOPK = 16
PEER_GROUPS = 2 * PEER_HEADS
PEER_SEL = PEER_HEADS * PEER_TOPK

LANES = 128
SUBLANES = 8
VMEM_LIMIT = 56 * 1024 * 1024

TOKEN_TILE = 512
PEER_TOKEN_TILE = 1024
PEER_EXPERT_BLOCK = 2048
TOPK_SLABS = 4


def _tile_sizes(seq_len, n_tokens):
    return min(TOKEN_TILE, seq_len), min(PEER_TOKEN_TILE, n_tokens)

BF16 = jnp.bfloat16
F32 = jnp.float32


def _dot(a, b):
    return jnp.dot(a, b, preferred_element_type=F32)


def _dot_nt(a, b):
    return lax.dot_general(a, b, (((1,), (1,)), ((), ())), preferred_element_type=F32)


def _params(semantics):
    return pltpu.CompilerParams(dimension_semantics=semantics, vmem_limit_bytes=VMEM_LIMIT)


def _const_spec(shape):
    n = len(shape)
    return pl.BlockSpec(shape, lambda *_: (0,) * n)


KV_SLABS = KV_WIDTH // LANES
NORM_COLS = 256
DILATIONS = tuple(d for _, d in DILATED_PATTERNS)


def _store_dilated(y, slab_ref, out_refs):
    tm = y.shape[0]
    for d, ref in out_refs:
        if d == 1:
            ref[...] = y.astype(BF16)
    if all(d == 1 for d, _ in out_refs):
        return
    for s in range(KV_SLABS):
        slab_ref[s] = y[:, s * LANES:(s + 1) * LANES]
    for d, ref in out_refs:
        if d == 1:
            continue
        for r in range(d):
            for s in range(KV_SLABS):
                col = r * KV_WIDTH + s * LANES
                ref[:, col:col + LANES] = slab_ref[s, pl.ds(r, tm // d, stride=d), :].astype(BF16)


def _proj_kernel(x_ref, g1_ref, w_ref, bd_ref, qg_ref, kg_ref,
                 a_ref, q0_ref, q1_ref, q2_ref, k0_ref, k1_ref, k2_ref, v0_ref, v1_ref, v2_ref, gate_ref,
                 slab_ref):
    x = x_ref[...]
    inv = lax.rsqrt(jnp.mean(x * x, axis=-1, keepdims=True) + EPS)
    xn = (x * inv * g1_ref[...]).astype(BF16)

    def head_norm(y, gain):
        parts = []
        for c in range(0, KV_WIDTH, NORM_COLS):
            yc = y[:, c:c + NORM_COLS]
            ss = _dot((yc * yc).astype(BF16), bd_ref[...])
            parts.append(yc * lax.rsqrt(ss * (1.0 / HEAD_DIM) + EPS))
        return jnp.concatenate(parts, axis=1) * gain

    a_ref[...] = _dot(xn, w_ref[:, 0:POOL_WIDTH])
    for g, q_ref in enumerate((q0_ref, q1_ref, q2_ref)):
        lo = Q_OFF + g * KV_WIDTH
        q = head_norm(_dot(xn, w_ref[:, lo:lo + KV_WIDTH]), qg_ref[...])
        _store_dilated(q, slab_ref, [(DILATIONS[g], q_ref)])
    k = head_norm(_dot(xn, w_ref[:, K_OFF:K_OFF + KV_WIDTH]), kg_ref[...])
    _store_dilated(k, slab_ref, list(zip(DILATIONS, (k0_ref, k1_ref, k2_ref))))
    v = _dot(xn, w_ref[:, V_OFF:V_OFF + KV_WIDTH])
    _store_dilated(v, slab_ref, list(zip(DILATIONS, (v0_ref, v1_ref, v2_ref))))
    gate_ref[...] = jax.nn.sigmoid(_dot(xn, w_ref[:, GATE_OFF:IN_WIDTH])).astype(BF16)


def _proj(x2, g1, w_in, bd, qg, kg, tm):
    T = x2.shape[0]
    row = lambda w: pl.BlockSpec((tm, w), lambda i: (i, 0))
    dil = lambda d: pl.BlockSpec((tm // d, d * KV_WIDTH), lambda i: (i, 0))
    dil_shape = lambda d: jax.ShapeDtypeStruct((T // d, d * KV_WIDTH), BF16)
    once = lambda shape: pl.BlockSpec(shape, lambda i: (0, 0), pipeline_mode=pl.Buffered(1))
    views = [dil(d) for d in DILATIONS]
    view_shapes = [dil_shape(d) for d in DILATIONS]
    return pl.pallas_call(
        _proj_kernel,
        grid=(T // tm,),
        in_specs=[row(D_MODEL), once((1, D_MODEL)), once((D_MODEL, IN_WIDTH)),
                  once((NORM_COLS, NORM_COLS)), once((1, KV_WIDTH)), once((1, KV_WIDTH))],
        out_specs=(row(POOL_WIDTH), *views, *views, *views, row(2 * D_MODEL)),
        out_shape=(jax.ShapeDtypeStruct((T, POOL_WIDTH), F32), *view_shapes, *view_shapes, *view_shapes,
                   jax.ShapeDtypeStruct((T, 2 * D_MODEL), BF16)),
        scratch_shapes=[pltpu.VMEM((KV_SLABS, tm, LANES), F32)],
        compiler_params=_params(("arbitrary",)),
        name="proj",
    )(x2, g1, w_in, bd, qg, kg)


def _alibi_slopes(n):
    def geometric(k):
        start = 2.0 ** (-8.0 / k)
        return [start ** (i + 1) for i in range(k)]
    p = 2 ** int(math.floor(math.log2(n)))
    s = geometric(p) + geometric(2 * p)[0::2][: n - p]
    return np.sort(np.array(s, dtype=np.float32))[::-1].copy()


def _attn_bias(slopes, dilation):
    qi = np.arange(BAND)[:, None]
    ki = np.arange(2 * BAND)[None, :]
    step = qi + BAND - ki
    valid = (step >= 0) & (step <= BAND)
    alibi = -slopes[:, None, None] * (step * dilation).astype(np.float32)[None]
    general = np.where(valid[None], alibi, np.float32(NEG_INF)).astype(np.float32)
    first = np.where((ki >= BAND)[None], general, np.float32(NEG_INF)).astype(np.float32)
    return np.stack([general, first])


HEAD_PAIRS = ATT_HEADS // 2


def _attn_kernel(q_ref, kp_ref, kc_ref, vp_ref, vc_ref, bias_ref, o_ref, lse_ref, s_ref, m_ref, kt_ref):
    q_blocks = q_ref.shape[1] // BAND
    first_step = (pl.program_id(2) == 0).astype(jnp.int32)
    lane = lax.broadcasted_iota(jnp.int32, (BAND, LANES), 1)
    low = lane < HEAD_DIM
    low_b = (lax.broadcasted_iota(jnp.int32, (1, LANES), 1) < HEAD_DIM).astype(BF16)
    lane_sel = (low_b, 1 - low_b)

    for j in range(HEAD_PAIRS):
        cols = slice(j * LANES, (j + 1) * LANES)
        kt_ref[0, cols, :] = kp_ref[0, :, cols].T
        for blk in range(q_blocks):
            kt_ref[blk + 1, cols, :] = kc_ref[0, blk * BAND:(blk + 1) * BAND, cols].T

    for blk in range(q_blocks):
        rows = slice(blk * BAND, (blk + 1) * BAND)
        if blk == 0:
            v_prev, first = vp_ref, first_step
            before = slice(0, BAND)
        else:
            v_prev, first = vc_ref, 0
            before = slice((blk - 1) * BAND, blk * BAND)

        for h in range(ATT_HEADS):
            cols = slice((h // 2) * LANES, (h // 2 + 1) * LANES)
            qm = q_ref[0, rows, cols] * lane_sel[h % 2]
            sp = _dot(qm, kt_ref[blk, cols, :]) + bias_ref[first, h, :, 0:BAND]
            sc = _dot(qm, kt_ref[blk + 1, cols, :]) + bias_ref[first, h, :, BAND:2 * BAND]
            s_ref[blk, h, :, 0:BAND] = sp
            s_ref[blk, h, :, BAND:2 * BAND] = sc
            m_ref[blk, h] = jnp.broadcast_to(jnp.max(jnp.maximum(sp, sc), axis=-1, keepdims=True), (BAND, LANES))

        lse_tile = jnp.zeros((BAND, LANES), F32)
        for j in range(HEAD_PAIRS):
            cols = slice(j * LANES, (j + 1) * LANES)
            res = []
            for half in range(2):
                h = 2 * j + half
                own, other = lane_sel[half], lane_sel[1 - half]
                pp = jnp.exp(s_ref[blk, h, :, 0:BAND] - m_ref[blk, h]).astype(BF16)
                pc = jnp.exp(s_ref[blk, h, :, BAND:2 * BAND] - m_ref[blk, h]).astype(BF16)
                res.append(_dot(pp, v_prev[0, before, cols] * own + other)
                           + _dot(pc, vc_ref[0, rows, cols] * own + other))
            num = jnp.where(low, res[0], res[1])
            den = pltpu.roll(jnp.where(low, res[1], res[0]), HEAD_DIM, axis=1)
            o_ref[0, rows, cols] = (num / den).astype(BF16)
            lse = jnp.where(low, m_ref[blk, 2 * j], m_ref[blk, 2 * j + 1]) + jnp.log(den)
            lse_tile = jnp.where((lane == j) | (lane == HEAD_DIM + j), lse, lse_tile)
        lse_ref[0, rows, :] = lse_tile


def _attn_pattern(q, k, v, bias, dilation, B):
    d = dilation
    rows = q.shape[0] // B
    q_blocks = max(n for n in (1, 2, 4) if rows % (n * BAND) == 0)
    qr = q_blocks * BAND
    view = lambda a: a.reshape(B, rows, d * KV_WIDTH)
    cur = lambda w: pl.BlockSpec((1, qr, w), lambda b, r, n: (b, n, r))
    prev = pl.BlockSpec((1, BAND, KV_WIDTH), lambda b, r, n: (b, jnp.maximum(q_blocks * n - 1, 0), r))
    o, lse = pl.pallas_call(
        _attn_kernel,
        grid=(B, d, rows // qr),
        in_specs=[cur(KV_WIDTH), prev, cur(KV_WIDTH), prev, cur(KV_WIDTH),
                  pl.BlockSpec(bias.shape, lambda b, r, n: (0, 0, 0, 0))],
        out_specs=(cur(KV_WIDTH), cur(LANES)),
        out_shape=(jax.ShapeDtypeStruct((B, rows, d * KV_WIDTH), BF16),
                   jax.ShapeDtypeStruct((B, rows, d * LANES), F32)),
        scratch_shapes=[pltpu.VMEM((q_blocks, ATT_HEADS, BAND, 2 * BAND), F32),
                        pltpu.VMEM((q_blocks, ATT_HEADS, BAND, LANES), F32),
                        pltpu.VMEM((q_blocks + 1, KV_WIDTH, BAND), BF16)],
        compiler_params=_params(("arbitrary", "arbitrary", "arbitrary")),
        name=f"attn_d{d}",
    )(view(q), view(k), view(k), view(v), view(v), bias)
    return o, lse


def _head_expand():
    e = np.zeros((LANES, KV_WIDTH), np.float32)
    for h in range(ATT_HEADS):
        e[(h % 2) * HEAD_DIM + h // 2, h * HEAD_DIM:(h + 1) * HEAD_DIM] = 1.0
    return np.concatenate([e, e], axis=0)


def _mix_kernel(x_ref, a_ref, ah_ref, o0_ref, o1_ref, o2_ref, l0_ref, l1_ref, l2_ref, ex_ref, gate_ref,
                wp_ref, ps_ref, wo_ref, g2_ref, wq_ref, sk_ref,
                h_ref, hn_ref, sc_ref, ext_ref, merged_ref, onat_ref, lnat_ref):
    tm = x_ref.shape[1]
    i = pl.program_id(1)

    def natural_rows(src_ref, d, slabs, dst_ref):
        n = src_ref.shape[1]
        for r in range(d):
            for s in range(slabs):
                col = (r * slabs + s) * LANES
                dst_ref[s, pl.ds(r, n, stride=d), :] = src_ref[0, :, col:col + LANES].astype(F32)

    for p, (o_ref, l_ref) in enumerate(((o1_ref, l1_ref), (o2_ref, l2_ref))):
        natural_rows(o_ref, DILATIONS[p + 1], KV_SLABS, onat_ref.at[p])
        natural_rows(l_ref, DILATIONS[p + 1], 1, lnat_ref.at[p])

    def attn_cols(p, g):
        if p == 0:
            return o0_ref[0, :, g * POOL_OUT:(g + 1) * POOL_OUT].astype(F32)
        return jnp.concatenate([onat_ref[p - 1, 2 * g], onat_ref[p - 1, 2 * g + 1]], axis=1)

    lses = [l0_ref[0], lnat_ref[0, 0], lnat_ref[1, 0]]
    top = jnp.maximum(jnp.maximum(lses[0], lses[1]), lses[2])
    es = [jnp.exp(l - top) for l in lses]
    tot = es[0] + es[1] + es[2]
    w_split = []
    for e in es:
        w = e / tot
        hi = w.astype(BF16)
        w_split.append(jnp.concatenate([hi, (w - hi.astype(F32)).astype(BF16)], axis=1))
    ext_ref[0:POOL_HALO, :] = jnp.where(i == 0, 0.0, ah_ref[0])
    ext_ref[POOL_HALO:POOL_HALO + tm, :] = a_ref[0]
    t = i * tm + lax.broadcasted_iota(jnp.int32, (tm, 1), 0)
    for g, w in enumerate(POOL_WINDOWS):
        cols = slice(g * POOL_IN, (g + 1) * POOL_IN)
        wsum = ext_ref[POOL_HALO:POOL_HALO + tm, cols]
        for j in range(1, w):
            wsum = wsum + ext_ref[POOL_HALO - j:POOL_HALO - j + tm, cols]
        count = jnp.minimum(t + 1, w).astype(F32)
        dlt = (wsum / count - a_ref[0, :, cols]).astype(BF16)
        ocol = slice(g * POOL_OUT, (g + 1) * POOL_OUT)
        pool = _dot(dlt, wp_ref[g]) * ps_ref[:, ocol]
        attn = jnp.zeros((tm, POOL_OUT), F32)
        for p, w2 in enumerate(w_split):
            attn = attn + _dot(w2, ex_ref[:, ocol]) * attn_cols(p, g)
        merged = (gate_ref[0, :, ocol].astype(F32) * pool
                  + gate_ref[0, :, D_MODEL + g * POOL_OUT:D_MODEL + (g + 1) * POOL_OUT].astype(F32) * attn)
        merged_ref[:, ocol] = merged.astype(BF16)
    h = x_ref[0] + _dot(merged_ref[...], wo_ref[...])
    h_ref[0] = h
    hn = (h * lax.rsqrt(jnp.mean(h * h, axis=-1, keepdims=True) + EPS) * g2_ref[...]).astype(BF16)
    hn_ref[0] = hn
    qt = _dot_nt(wq_ref[...], hn).astype(BF16)
    for g in range(PEER_GROUPS):
        head, half = divmod(g, 2)
        rows = slice(g * PEER_NKEYS, (g + 1) * PEER_NKEYS)
        sc = _dot(sk_ref[g], qt[rows, :])
        for j in range(tm // LANES):
            sc_ref[j, pl.ds(half * PEER_HEADS * PEER_NKEYS + head, PEER_NKEYS, stride=PEER_HEADS), :] = (
                sc[:, j * LANES:(j + 1) * LANES])


def _mix(x, a, attn_outs, attn_lses, expand, gates, w_pool, pool_scale, w_out, g2, wq_t, sub_keys, tm):
    B, S, _ = x.shape
    T = B * S
    nt = S // tm
    halo_per_tile = tm // POOL_HALO
    tile = lambda w: pl.BlockSpec((1, tm, w), lambda b, i: (b, i, 0))
    dil = lambda d, w: pl.BlockSpec((1, tm // d, d * w), lambda b, i: (b, i, 0))
    halo = pl.BlockSpec((1, POOL_HALO, POOL_WIDTH), lambda b, i: (b, jnp.maximum(i * halo_per_tile - 1, 0), 0))
    once = lambda shape: pl.BlockSpec(shape, lambda b, i: (0,) * len(shape), pipeline_mode=pl.Buffered(1))
    return pl.pallas_call(
        _mix_kernel,
        grid=(B, nt),
        in_specs=[tile(D_MODEL), tile(POOL_WIDTH), halo,
                  *[dil(d, KV_WIDTH) for d in DILATIONS], *[dil(d, LANES) for d in DILATIONS],
                  once(expand.shape), tile(2 * D_MODEL),
                  once(w_pool.shape), once((1, D_MODEL)), once((D_MODEL, D_MODEL)), once((1, D_MODEL)),
                  once(wq_t.shape), once(sub_keys.shape)],
        out_specs=(tile(D_MODEL), tile(D_MODEL),
                   pl.BlockSpec((tm // LANES, PEER_GROUPS * PEER_NKEYS, LANES), lambda b, i: (b * nt + i, 0, 0))),
        out_shape=(jax.ShapeDtypeStruct((B, S, D_MODEL), F32),
                   jax.ShapeDtypeStruct((B, S, D_MODEL), BF16),
                   jax.ShapeDtypeStruct((T // LANES, PEER_GROUPS * PEER_NKEYS, LANES), F32)),
        scratch_shapes=[pltpu.VMEM((POOL_HALO + tm, POOL_WIDTH), F32),
                        pltpu.VMEM((tm, D_MODEL), BF16),
                        pltpu.VMEM((N_PATTERNS - 1, KV_SLABS, tm, LANES), F32),
                        pltpu.VMEM((N_PATTERNS - 1, 1, tm, LANES), F32)],
        compiler_params=_params(("arbitrary", "arbitrary")),
        name="mix",
    )(x, a, a, *attn_outs, *attn_lses, expand, gates, w_pool, pool_scale, w_out, g2, wq_t, sub_keys)


_PAIRS = frozenset((a, b) for a in range(PEER_TOPK) for b in range(PEER_TOPK) if (a + 1) * (b + 1) <= PEER_TOPK)


def _oddeven_merge_sort_pairs(n):
    pairs = []

    def merge(lo, hi, r):
        step = r * 2
        if step < hi - lo:
            merge(lo, hi, step)
            merge(lo + r, hi, step)
            pairs.extend((i, i + r) for i in range(lo + r, hi - r, step))
        else:
            pairs.append((lo, lo + r))

    def sort(lo, hi):
        if hi - lo >= 1:
            mid = lo + (hi - lo) // 2
            sort(lo, mid)
            sort(mid + 1, hi)
            merge(lo, hi, 1)

    sort(0, n - 1)
    return tuple(pairs)


_SORT_PAIRS = _oddeven_merge_sort_pairs(PEER_TOPK)
LISTS_PER_HALF = PEER_NKEYS // PEER_TOPK


def _precedes(va, ia, vb, ib):
    if isinstance(ia, float) and isinstance(ib, float):
        return (va >= vb) if ia < ib else (va > vb)
    return (va > vb) | ((va == vb) & (ia < ib))


def _compare_exchange(a, b):
    (va, ia), (vb, ib) = a, b
    a_first = _precedes(va, ia, vb, ib)
    return ((jnp.maximum(va, vb), jnp.where(a_first, ia, ib)),
            (jnp.minimum(va, vb), jnp.where(a_first, ib, ia)))


def _sort_desc(items):
    items = list(items)
    for i, j in _SORT_PAIRS:
        items[i], items[j] = _compare_exchange(items[i], items[j])
    return items


def _bitonic_sort_desc(items):
    items = list(items)
    n = len(items)
    span = n // 2
    while span:
        for k in range(n):
            if not k & span:
                items[k], items[k + span] = _compare_exchange(items[k], items[k + span])
        span //= 2
    return items


def _merge_keep_top(a, b):
    n = len(a)
    out = []
    for k in range(n):
        if n - 1 - k >= len(b):
            out.append(a[k])
            continue
        (va, ia), (vb, ib) = a[k], b[n - 1 - k]
        out.append((jnp.maximum(va, vb), jnp.where(_precedes(va, ia, vb, ib), ia, ib)))
    return _bitonic_sort_desc(out)


def _topk_kernel(sc_ref, isel_ref, jsel_ref, g_ref, av_ref, ai_ref, bv_ref, bi_ref, ts_ref, ti_ref,
                 e_ref, s_ref):
    refs = (sc_ref, isel_ref, jsel_ref, g_ref, av_ref, ai_ref, bv_ref, bi_ref, ts_ref, ti_ref, e_ref, s_ref)
    lax.fori_loop(0, sc_ref.shape[0], lambda t, carry: _topk_tile(*refs, t, carry), 0)


def _topk_tile(sc_ref, isel_ref, jsel_ref, g_ref, av_ref, ai_ref, bv_ref, bi_ref, ts_ref, ti_ref,
               e_ref, s_ref, t, carry):
    def sort_group(g, carry):
        base = pl.multiple_of(g * PEER_TOPK * PEER_HEADS, PEER_TOPK * PEER_HEADS)
        items = [(sc_ref[t, pl.ds(base + k * PEER_HEADS, PEER_HEADS), :], float(k)) for k in range(PEER_TOPK)]
        first_key = ((g % LISTS_PER_HALF) * PEER_TOPK).astype(F32)
        for r, (v, i) in enumerate(_sort_desc(items)):
            av_ref[g, r] = v
            ai_ref[g, r] = i + first_key
        return carry

    lax.fori_loop(0, 2 * LISTS_PER_HALF, sort_group, 0)

    def merge_level(src_v, src_i, dst_v, dst_i, n_out):
        def body(m, carry):
            lists = [[(src_v[2 * m + side, r], src_i[2 * m + side, r]) for r in range(PEER_TOPK)]
                     for side in range(2)]
            for r, (v, i) in enumerate(_merge_keep_top(*lists)):
                dst_v[m, r] = v
                dst_i[m, r] = i
            return carry

        lax.fori_loop(0, n_out, body, 0)

    merge_level(av_ref, ai_ref, bv_ref, bi_ref, LISTS_PER_HALF)
    merge_level(bv_ref, bi_ref, av_ref, ai_ref, LISTS_PER_HALF // 2)
    merge_level(av_ref, ai_ref, ts_ref, ti_ref, 2)

    def cand(a, b):
        expert = ti_ref[0, a] * float(PEER_NKEYS) + ti_ref[1, b]
        return ts_ref[0, a] + ts_ref[1, b], expert + float((a * PEER_TOPK + b) * PEER_N)

    by_row = {a: [cand(a, b) for b in range(PEER_TOPK) if (a, b) in _PAIRS] for a in range(PEER_TOPK)}
    single = [by_row[a][0] for a in range(8, PEER_TOPK)]
    second = _bitonic_sort_desc(by_row[1] + single[::-1])
    third = _sort_desc(by_row[2] + by_row[3] + by_row[4] + by_row[5] + by_row[6])
    third = _merge_keep_top(third, by_row[7])
    best = _merge_keep_top(_merge_keep_top(by_row[0], second), third)
    for r, (v, key) in enumerate(best):
        s_ref[r] = v
        e_ref[r] = key - jnp.floor(key * (1.0 / PEER_N)) * float(PEER_N)

    best = s_ref[...]
    ex = jnp.exp(best - best[0:1])
    gate = ex / jnp.sum(ex, axis=0, keepdims=True)
    e = e_ref[...].reshape(PEER_SEL, LANES)
    i_sel = jnp.floor(e * (1.0 / PEER_NKEYS))
    rows = pl.ds(pl.multiple_of(t * LANES, LANES), LANES)
    isel_ref[rows, :] = i_sel.T
    jsel_ref[t] = e - i_sel * PEER_NKEYS
    g_ref[rows, :] = gate.reshape(PEER_SEL, LANES).T
    return carry


def _topk(sc_t):
    n_tiles = sc_t.shape[0]
    per_step = math.gcd(n_tiles, TOPK_SLABS)
    out = pl.BlockSpec((per_step * LANES, PEER_SEL), lambda i: (i, 0))
    shp = jax.ShapeDtypeStruct((n_tiles * LANES, PEER_SEL), F32)
    out_t = pl.BlockSpec((per_step, PEER_SEL, LANES), lambda i: (i, 0, 0))
    shp_t = jax.ShapeDtypeStruct((n_tiles, PEER_SEL, LANES), F32)
    vreg = (PEER_HEADS, LANES)
    return pl.pallas_call(
        _topk_kernel,
        grid=(n_tiles // per_step,),
        in_specs=[pl.BlockSpec((per_step, PEER_GROUPS * PEER_NKEYS, LANES), lambda i: (i, 0, 0))],
        out_specs=(out, out_t, out),
        out_shape=(shp, shp_t, shp),
        scratch_shapes=[pltpu.VMEM((2 * LISTS_PER_HALF, PEER_TOPK) + vreg, F32),
                        pltpu.VMEM((2 * LISTS_PER_HALF, PEER_TOPK) + vreg, F32),
                        pltpu.VMEM((LISTS_PER_HALF, PEER_TOPK) + vreg, F32),
                        pltpu.VMEM((LISTS_PER_HALF, PEER_TOPK) + vreg, F32),
                        pltpu.VMEM((2, PEER_TOPK) + vreg, F32),
                        pltpu.VMEM((2, PEER_TOPK) + vreg, F32),
                        pltpu.VMEM((PEER_TOPK,) + vreg, F32),
                        pltpu.VMEM((PEER_TOPK,) + vreg, F32)],
        compiler_params=_params(("arbitrary",)),
        name="topk",
    )(sc_t)


G_TILE = 128
G_PITCH = G_TILE + SUBLANES // 2


def _gbuild_kernel(isel_ref, jsel_ref, g_ref, out_ref, tile_ref):
    sub = lax.broadcasted_iota(jnp.int32, (PEER_NKEYS, PEER_SEL), 0).astype(F32)
    lane = lax.broadcasted_iota(jnp.int32, (PEER_SEL, PEER_NKEYS), 1).astype(F32).astype(BF16)
    one, zero = jnp.ones((), BF16), jnp.zeros((), BF16)
    for c in range(G_TILE):
        irow = isel_ref[c:c + 1, :]
        grow = g_ref[c:c + 1, :]
        jcol = jsel_ref[0, :, c:c + 1].astype(BF16)
        a = jnp.where(sub == irow, grow, 0.0).astype(BF16)
        b = jnp.where(lane == jcol, one, zero)
        tile_ref[pl.ds(c, PEER_NKEYS, stride=G_PITCH), :] = _dot(a, b)
    for i in range(PEER_NKEYS):
        out_ref[i] = tile_ref[i * G_PITCH:i * G_PITCH + G_TILE, :].astype(BF16)


def _gbuild(isel, jsel, gates):
    T = isel.shape[0]
    row = pl.BlockSpec((G_TILE, PEER_SEL), lambda t: (t, 0))
    return pl.pallas_call(
        _gbuild_kernel,
        grid=(T // G_TILE,),
        in_specs=[row, pl.BlockSpec((1, PEER_SEL, G_TILE), lambda t: (t, 0, 0)), row],
        out_specs=pl.BlockSpec((PEER_NKEYS, G_TILE, PEER_NKEYS), lambda t: (0, t, 0)),
        out_shape=jax.ShapeDtypeStruct((PEER_NKEYS, T, PEER_NKEYS), BF16),
        scratch_shapes=[pltpu.VMEM((PEER_NKEYS * G_PITCH, PEER_NKEYS), F32)],
        compiler_params=_params(("arbitrary",)),
        name="gbuild",
    )(isel, jsel, gates)


def _gelu_tanh(x):
    return 0.5 * x * (1.0 + jnp.tanh(math.sqrt(2.0 / math.pi) * (x + 0.044715 * (x * x * x))))


def _peer_kernel(hn_ref, u_ref, v_ref, g_ref, h_ref, out_ref, w_ref):
    e = pl.program_id(1)
    n_rows = g_ref.shape[0]

    @pl.when(e == 0)
    def _():
        out_ref[...] = h_ref[...]

    hn = hn_ref[...]
    for i in range(0, n_rows, 2):
        cols = slice(i * PEER_NKEYS, (i + 2) * PEER_NKEYS)
        act = _gelu_tanh(_dot(hn, u_ref[:, cols]))
        gate = jnp.concatenate([g_ref[i], g_ref[i + 1]], axis=1).astype(F32)
        w_ref[:, cols] = (act * gate).astype(BF16)
    out_ref[...] += _dot(w_ref[...], v_ref[...])


def _peer(hn2, h2, u_t, v, gmat, tt, eb):
    T = hn2.shape[0]
    n_rows = eb // PEER_NKEYS
    tok = lambda dt: pl.BlockSpec((tt, D_MODEL), lambda t, e: (t, 0))
    exp = pl.BlockSpec((eb, D_MODEL), lambda t, e: (e, 0))
    exp_t = pl.BlockSpec((D_MODEL, eb), lambda t, e: (0, e))
    return pl.pallas_call(
        _peer_kernel,
        grid=(T // tt, PEER_N // eb),
        in_specs=[tok(BF16), exp_t, exp,
                  pl.BlockSpec((n_rows, tt, PEER_NKEYS), lambda t, e: (e, t, 0)),
                  tok(F32)],
        out_specs=pl.BlockSpec((tt, D_MODEL), lambda t, e: (t, 0)),
        out_shape=jax.ShapeDtypeStruct((T, D_MODEL), F32),
        scratch_shapes=[pltpu.VMEM((tt, eb), BF16)],
        compiler_params=_params(("arbitrary", "arbitrary")),
        name="peer",
    )(hn2, u_t, v, gmat, h2)


def _layer(h, norm1_g, w_in, q_norm_g, k_norm_g, w_pool, pool_scale, w_out, norm2_g,
           w_query, sub_keys, expert_u, expert_v, biases):
    B, S, D = h.shape
    T = B * S
    tm, tt = _tile_sizes(S, T)
    assert D == D_MODEL and w_in.shape == (D_MODEL, IN_WIDTH), (h.shape, w_in.shape)
    assert S % (BAND * max(DILATIONS)) == 0, "every dilation pattern needs whole 128-key blocks"
    assert S % tm == 0 and tm % (2 * SUBLANES * max(DILATIONS)) == 0 and T % tt == 0 and tt % LANES == 0
    head = np.arange(NORM_COLS) // HEAD_DIM
    bd = jnp.asarray(head[:, None] == head[None, :], BF16)
    qg = (jnp.tile(q_norm_g, ATT_HEADS) * (HEAD_DIM ** -0.5)).reshape(1, KV_WIDTH)
    kg = jnp.tile(k_norm_g, ATT_HEADS).reshape(1, KV_WIDTH)

    res = _proj(h.reshape(T, D), norm1_g.reshape(1, D), w_in.astype(BF16), bd, qg, kg, tm)
    a, qs, ks, vs, gates = res[0], res[1:4], res[4:7], res[7:10], res[10]
    attn_outs, attn_lses = [], []
    for g, dilation in enumerate(DILATIONS):
        o, lse = _attn_pattern(qs[g], ks[g], vs[g], biases[g], dilation, B)
        attn_outs.append(o)
        attn_lses.append(lse)

    wq_t = w_query.T.astype(BF16)
    sk = sub_keys.reshape(PEER_GROUPS, PEER_NKEYS, PEER_HALF).astype(BF16)
    h_new, hn, sc_t = _mix(h, a.reshape(B, S, POOL_WIDTH), attn_outs, attn_lses,
                           jnp.asarray(_head_expand(), BF16), gates.reshape(B, S, 2 * D),
                           w_pool.astype(BF16), pool_scale.reshape(1, D), w_out.astype(BF16),
                           norm2_g.reshape(1, D), wq_t, sk, tm)
    isel, jsel, gsel = _topk(sc_t)
    gmat = _gbuild(isel, jsel, gsel)
    out = _peer(hn.reshape(T, D), h_new.reshape(T, D), expert_u.T.astype(BF16), expert_v.astype(BF16),
                gmat, tt, PEER_EXPERT_BLOCK)
    return out.reshape(B, S, D)


def kernel(x, norm1_g, w_in, q_norm_g, k_norm_g, w_pool, pool_scale, w_out, norm2_g, w_query, sub_keys,
           expert_u, expert_v):
    slopes = _alibi_slopes(N_PATTERNS * ATT_HEADS).reshape(N_PATTERNS, ATT_HEADS)
    biases = [jnp.asarray(_attn_bias(slopes[g], dil)) for g, (_, dil) in enumerate(DILATED_PATTERNS)]
    h = x
    for l in range(norm1_g.shape[0]):
        h = _layer(h, norm1_g[l], w_in[l], q_norm_g[l], k_norm_g[l], w_pool[l], pool_scale[l], w_out[l],
                   norm2_g[l], w_query[l], sub_keys[l], expert_u[l], expert_v[l], biases)
    return h
```

```python
import math

import numpy as np
import jax
import jax.numpy as jnp
from jax import lax
from jax.experimental import pallas as pl
from jax.experimental.pallas import tpu as pltpu

D_MODEL = 1024
EPS = 1e-6
NEG_INF = -1e30

POOL_WINDOWS = (2, 4, 8, 16)
POOL_GROUPS = 4
POOL_IN = 128
POOL_WIDTH = 512
POOL_OUT = 256
POOL_HALO = 16

ATT_HEADS = 16
HEAD_DIM = 64
DILATED_PATTERNS = ((128, 1), (512, 4), (2048, 16))
N_PATTERNS = 3
BAND = 128
KV_WIDTH = 1024
Q_OFF = POOL_WIDTH
K_OFF = Q_OFF + N_PATTERNS * KV_WIDTH
V_OFF = K_OFF + KV_WIDTH
GATE_OFF = V_OFF + KV_WIDTH
IN_WIDTH = GATE_OFF + 2 * D_MODEL

PEER_HEADS = 8
PEER_NKEYS = 128
PEER_N = PEER_NKEYS * PEER_NKEYS
PEER_HALF = 128
PEER_TOPK = 16
PEER_GROUPS = 2 * PEER_HEADS
PEER_SEL = PEER_HEADS * PEER_TOPK

LANES = 128
SUBLANES = 8
VMEM_LIMIT = 56 * 1024 * 1024

TOKEN_TILE = 512
PEER_TOKEN_TILE = 1024
PEER_EXPERT_BLOCK = 2048


def _tile_sizes(seq_len, n_tokens):
    return min(TOKEN_TILE, seq_len), min(PEER_TOKEN_TILE, n_tokens)

BF16 = jnp.bfloat16
F32 = jnp.float32


def _dot(a, b):
    return jnp.dot(a, b, preferred_element_type=F32)


def _dot_nt(a, b):
    return lax.dot_general(a, b, (((1,), (1,)), ((), ())), preferred_element_type=F32)


def _params(semantics):
    return pltpu.CompilerParams(dimension_semantics=semantics, vmem_limit_bytes=VMEM_LIMIT)


def _const_spec(shape):
    n = len(shape)
    return pl.BlockSpec(shape, lambda *_: (0,) * n)


KV_SLABS = KV_WIDTH // LANES
NORM_COLS = 256
DILATIONS = tuple(d for _, d in DILATED_PATTERNS)


def _store_dilated(y, slab_ref, out_refs):
    tm = y.shape[0]
    for d, ref in out_refs:
        if d == 1:
            ref[...] = y.astype(BF16)
    if all(d == 1 for d, _ in out_refs):
        return
    for s in range(KV_SLABS):
        slab_ref[s] = y[:, s * LANES:(s + 1) * LANES]
    for d, ref in out_refs:
        if d == 1:
            continue
        for r in range(d):
            for s in range(KV_SLABS):
                col = r * KV_WIDTH + s * LANES
                ref[:, col:col + LANES] = slab_ref[s, pl.ds(r, tm // d, stride=d), :].astype(BF16)


def _proj_kernel(x_ref, g1_ref, w_ref, bd_ref, qg_ref, kg_ref,
                 a_ref, q0_ref, q1_ref, q2_ref, k0_ref, k1_ref, k2_ref, v0_ref, v1_ref, v2_ref, gate_ref,
                 slab_ref):
    x = x_ref[...]
    inv = lax.rsqrt(jnp.mean(x * x, axis=-1, keepdims=True) + EPS)
    xn = (x * inv * g1_ref[...]).astype(BF16)

    def head_norm(y, gain):
        parts = []
        for c in range(0, KV_WIDTH, NORM_COLS):
            yc = y[:, c:c + NORM_COLS]
            ss = _dot((yc * yc).astype(BF16), bd_ref[...])
            parts.append(yc * lax.rsqrt(ss * (1.0 / HEAD_DIM) + EPS))
        return jnp.concatenate(parts, axis=1) * gain

    a_ref[...] = _dot(xn, w_ref[:, 0:POOL_WIDTH])
    for g, q_ref in enumerate((q0_ref, q1_ref, q2_ref)):
        lo = Q_OFF + g * KV_WIDTH
        q = head_norm(_dot(xn, w_ref[:, lo:lo + KV_WIDTH]), qg_ref[...])
        _store_dilated(q, slab_ref, [(DILATIONS[g], q_ref)])
    k = head_norm(_dot(xn, w_ref[:, K_OFF:K_OFF + KV_WIDTH]), kg_ref[...])
    _store_dilated(k, slab_ref, list(zip(DILATIONS, (k0_ref, k1_ref, k2_ref))))
    v = _dot(xn, w_ref[:, V_OFF:V_OFF + KV_WIDTH])
    _store_dilated(v, slab_ref, list(zip(DILATIONS, (v0_ref, v1_ref, v2_ref))))
    gate_ref[...] = jax.nn.sigmoid(_dot(xn, w_ref[:, GATE_OFF:IN_WIDTH])).astype(BF16)


def _proj(x2, g1, w_in, bd, qg, kg, tm):
    T = x2.shape[0]
    row = lambda w: pl.BlockSpec((tm, w), lambda i: (i, 0))
    dil = lambda d: pl.BlockSpec((tm // d, d * KV_WIDTH), lambda i: (i, 0))
    dil_shape = lambda d: jax.ShapeDtypeStruct((T // d, d * KV_WIDTH), BF16)
    once = lambda shape: pl.BlockSpec(shape, lambda i: (0, 0), pipeline_mode=pl.Buffered(1))
    views = [dil(d) for d in DILATIONS]
    view_shapes = [dil_shape(d) for d in DILATIONS]
    return pl.pallas_call(
        _proj_kernel,
        grid=(T // tm,),
        in_specs=[row(D_MODEL), once((1, D_MODEL)), once((D_MODEL, IN_WIDTH)),
                  once((NORM_COLS, NORM_COLS)), once((1, KV_WIDTH)), once((1, KV_WIDTH))],
        out_specs=(row(POOL_WIDTH), *views, *views, *views, row(2 * D_MODEL)),
        out_shape=(jax.ShapeDtypeStruct((T, POOL_WIDTH), F32), *view_shapes, *view_shapes, *view_shapes,
                   jax.ShapeDtypeStruct((T, 2 * D_MODEL), BF16)),
        scratch_shapes=[pltpu.VMEM((KV_SLABS, tm, LANES), F32)],
        compiler_params=_params(("arbitrary",)),
        name="proj",
    )(x2, g1, w_in, bd, qg, kg)


def _alibi_slopes(n):
    def geometric(k):
        start = 2.0 ** (-8.0 / k)
        return [start ** (i + 1) for i in range(k)]
    p = 2 ** int(math.floor(math.log2(n)))
    s = geometric(p) + geometric(2 * p)[0::2][: n - p]
    return np.sort(np.array(s, dtype=np.float32))[::-1].copy()


def _attn_bias(slopes, dilation):
    qi = np.arange(BAND)[:, None]
    ki = np.arange(2 * BAND)[None, :]
    step = qi + BAND - ki
    valid = (step >= 0) & (step <= BAND)
    alibi = -slopes[:, None, None] * (step * dilation).astype(np.float32)[None]
    general = np.where(valid[None], alibi, np.float32(NEG_INF)).astype(np.float32)
    first = np.where((ki >= BAND)[None], general, np.float32(NEG_INF)).astype(np.float32)
    return np.stack([general, first])


HEAD_PAIRS = ATT_HEADS // 2


def _attn_kernel(q_ref, kp_ref, kc_ref, vp_ref, vc_ref, bias_ref, o_ref, lse_ref, s_ref, m_ref, kt_ref):
    q_blocks = q_ref.shape[1] // BAND
    first_step = (pl.program_id(2) == 0).astype(jnp.int32)
    lane = lax.broadcasted_iota(jnp.int32, (BAND, LANES), 1)
    low = lane < HEAD_DIM
    low_b = (lax.broadcasted_iota(jnp.int32, (1, LANES), 1) < HEAD_DIM).astype(BF16)
    lane_sel = (low_b, 1 - low_b)

    for j in range(HEAD_PAIRS):
        cols = slice(j * LANES, (j + 1) * LANES)
        kt_ref[0, cols, :] = kp_ref[0, :, cols].T
        for blk in range(q_blocks):
            kt_ref[blk + 1, cols, :] = kc_ref[0, blk * BAND:(blk + 1) * BAND, cols].T

    for blk in range(q_blocks):
        rows = slice(blk * BAND, (blk + 1) * BAND)
        if blk == 0:
            v_prev, first = vp_ref, first_step
            before = slice(0, BAND)
        else:
            v_prev, first = vc_ref, 0
            before = slice((blk - 1) * BAND, blk * BAND)

        for h in range(ATT_HEADS):
            cols = slice((h // 2) * LANES, (h // 2 + 1) * LANES)
            qm = q_ref[0, rows, cols] * lane_sel[h % 2]
            sp = _dot(qm, kt_ref[blk, cols, :]) + bias_ref[first, h, :, 0:BAND]
            sc = _dot(qm, kt_ref[blk + 1, cols, :]) + bias_ref[first, h, :, BAND:2 * BAND]
            s_ref[blk, h, :, 0:BAND] = sp
            s_ref[blk, h, :, BAND:2 * BAND] = sc
            m_ref[blk, h] = jnp.broadcast_to(jnp.max(jnp.maximum(sp, sc), axis=-1, keepdims=True), (BAND, LANES))

        lse_tile = jnp.zeros((BAND, LANES), F32)
        for j in range(HEAD_PAIRS):
            cols = slice(j * LANES, (j + 1) * LANES)
            res = []
            for half in range(2):
                h = 2 * j + half
                own, other = lane_sel[half], lane_sel[1 - half]
                pp = jnp.exp(s_ref[blk, h, :, 0:BAND] - m_ref[blk, h]).astype(BF16)
                pc = jnp.exp(s_ref[blk, h, :, BAND:2 * BAND] - m_ref[blk, h]).astype(BF16)
                res.append(_dot(pp, v_prev[0, before, cols] * own + other)
                           + _dot(pc, vc_ref[0, rows, cols] * own + other))
            num = jnp.where(low, res[0], res[1])
            den = pltpu.roll(jnp.where(low, res[1], res[0]), HEAD_DIM, axis=1)
            o_ref[0, rows, cols] = (num / den).astype(BF16)
            lse = jnp.where(low, m_ref[blk, 2 * j], m_ref[blk, 2 * j + 1]) + jnp.log(den)
            lse_tile = jnp.where((lane == j) | (lane == HEAD_DIM + j), lse, lse_tile)
        lse_ref[0, rows, :] = lse_tile


def _attn_pattern(q, k, v, bias, dilation, B):
    d = dilation
    rows = q.shape[0] // B
    q_blocks = max(n for n in (1, 2, 4) if rows % (n * BAND) == 0)
    qr = q_blocks * BAND
    view = lambda a: a.reshape(B, rows, d * KV_WIDTH)
    cur = lambda w: pl.BlockSpec((1, qr, w), lambda b, r, n: (b, n, r))
    prev = pl.BlockSpec((1, BAND, KV_WIDTH), lambda b, r, n: (b, jnp.maximum(q_blocks * n - 1, 0), r))
    o, lse = pl.pallas_call(
        _attn_kernel,
        grid=(B, d, rows // qr),
        in_specs=[cur(KV_WIDTH), prev, cur(KV_WIDTH), prev, cur(KV_WIDTH),
                  pl.BlockSpec(bias.shape, lambda b, r, n: (0, 0, 0, 0))],
        out_specs=(cur(KV_WIDTH), cur(LANES)),
        out_shape=(jax.ShapeDtypeStruct((B, rows, d * KV_WIDTH), BF16),
                   jax.ShapeDtypeStruct((B, rows, d * LANES), F32)),
        scratch_shapes=[pltpu.VMEM((q_blocks, ATT_HEADS, BAND, 2 * BAND), F32),
                        pltpu.VMEM((q_blocks, ATT_HEADS, BAND, LANES), F32),
                        pltpu.VMEM((q_blocks + 1, KV_WIDTH, BAND), BF16)],
        compiler_params=_params(("arbitrary", "arbitrary", "arbitrary")),
        name=f"attn_d{d}",
    )(view(q), view(k), view(k), view(v), view(v), bias)
    return o, lse


def _head_expand():
    e = np.zeros((LANES, KV_WIDTH), np.float32)
    for h in range(ATT_HEADS):
        e[(h % 2) * HEAD_DIM + h // 2, h * HEAD_DIM:(h + 1) * HEAD_DIM] = 1.0
    return np.concatenate([e, e], axis=0)


def _mix_kernel(x_ref, a_ref, ah_ref, o0_ref, o1_ref, o2_ref, l0_ref, l1_ref, l2_ref, ex_ref, gate_ref,
                wp_ref, ps_ref, wo_ref, g2_ref, wq_ref, sk_ref,
                h_ref, hn_ref, sc_ref, ext_ref, merged_ref, onat_ref, lnat_ref):
    tm = x_ref.shape[1]
    i = pl.program_id(1)

    def natural_rows(src_ref, d, slabs, dst_ref):
        n = src_ref.shape[1]
        for r in range(d):
            for s in range(slabs):
                col = (r * slabs + s) * LANES
                dst_ref[s, pl.ds(r, n, stride=d), :] = src_ref[0, :, col:col + LANES].astype(F32)

    for p, (o_ref, l_ref) in enumerate(((o1_ref, l1_ref), (o2_ref, l2_ref))):
        natural_rows(o_ref, DILATIONS[p + 1], KV_SLABS, onat_ref.at[p])
        natural_rows(l_ref, DILATIONS[p + 1], 1, lnat_ref.at[p])

    def attn_cols(p, g):
        if p == 0:
            return o0_ref[0, :, g * POOL_OUT:(g + 1) * POOL_OUT].astype(F32)
        return jnp.concatenate([onat_ref[p - 1, 2 * g], onat_ref[p - 1, 2 * g + 1]], axis=1)

    lses = [l0_ref[0], lnat_ref[0, 0], lnat_ref[1, 0]]
    top = jnp.maximum(jnp.maximum(lses[0], lses[1]), lses[2])
    es = [jnp.exp(l - top) for l in lses]
    tot = es[0] + es[1] + es[2]
    w_split = []
    for e in es:
        w = e / tot
        hi = w.astype(BF16)
        w_split.append(jnp.concatenate([hi, (w - hi.astype(F32)).astype(BF16)], axis=1))
    ext_ref[0:POOL_HALO, :] = jnp.where(i == 0, 0.0, ah_ref[0])
    ext_ref[POOL_HALO:POOL_HALO + tm, :] = a_ref[0]
    t = i * tm + lax.broadcasted_iota(jnp.int32, (tm, 1), 0)
    for g, w in enumerate(POOL_WINDOWS):
        cols = slice(g * POOL_IN, (g + 1) * POOL_IN)
        wsum = ext_ref[POOL_HALO:POOL_HALO + tm, cols]
        for j in range(1, w):
            wsum = wsum + ext_ref[POOL_HALO - j:POOL_HALO - j + tm, cols]
        count = jnp.minimum(t + 1, w).astype(F32)
        dlt = (wsum / count - a_ref[0, :, cols]).astype(BF16)
        ocol = slice(g * POOL_OUT, (g + 1) * POOL_OUT)
        pool = _dot(dlt, wp_ref[g]) * ps_ref[:, ocol]
        attn = jnp.zeros((tm, POOL_OUT), F32)
        for p, w2 in enumerate(w_split):
            attn = attn + _dot(w2, ex_ref[:, ocol]) * attn_cols(p, g)
        merged = (gate_ref[0, :, ocol].astype(F32) * pool
                  + gate_ref[0, :, D_MODEL + g * POOL_OUT:D_MODEL + (g + 1) * POOL_OUT].astype(F32) * attn)
        merged_ref[:, ocol] = merged.astype(BF16)
    h = x_ref[0] + _dot(merged_ref[...], wo_ref[...])
    h_ref[0] = h
    hn = (h * lax.rsqrt(jnp.mean(h * h, axis=-1, keepdims=True) + EPS) * g2_ref[...]).astype(BF16)
    hn_ref[0] = hn
    qt = _dot_nt(wq_ref[...], hn).astype(BF16)
    for g in range(PEER_GROUPS):
        head, half = divmod(g, 2)
        rows = slice(g * PEER_NKEYS, (g + 1) * PEER_NKEYS)
        sc = _dot(sk_ref[g], qt[rows, :])
        for j in range(tm // LANES):
            sc_ref[j, pl.ds(half * PEER_HEADS * PEER_NKEYS + head, PEER_NKEYS, stride=PEER_HEADS), :] = (
                sc[:, j * LANES:(j + 1) * LANES])


def _mix(x, a, attn_outs, attn_lses, expand, gates, w_pool, pool_scale, w_out, g2, wq_t, sub_keys, tm):
    B, S, _ = x.shape
    T = B * S
    nt = S // tm
    halo_per_tile = tm // POOL_HALO
    tile = lambda w: pl.BlockSpec((1, tm, w), lambda b, i: (b, i, 0))
    dil = lambda d, w: pl.BlockSpec((1, tm // d, d * w), lambda b, i: (b, i, 0))
    halo = pl.BlockSpec((1, POOL_HALO, POOL_WIDTH), lambda b, i: (b, jnp.maximum(i * halo_per_tile - 1, 0), 0))
    once = lambda shape: pl.BlockSpec(shape, lambda b, i: (0,) * len(shape), pipeline_mode=pl.Buffered(1))
    return pl.pallas_call(
        _mix_kernel,
        grid=(B, nt),
        in_specs=[tile(D_MODEL), tile(POOL_WIDTH), halo,
                  *[dil(d, KV_WIDTH) for d in DILATIONS], *[dil(d, LANES) for d in DILATIONS],
                  once(expand.shape), tile(2 * D_MODEL),
                  once(w_pool.shape), once((1, D_MODEL)), once((D_MODEL, D_MODEL)), once((1, D_MODEL)),
                  once(wq_t.shape), once(sub_keys.shape)],
        out_specs=(tile(D_MODEL), tile(D_MODEL),
                   pl.BlockSpec((tm // LANES, PEER_GROUPS * PEER_NKEYS, LANES), lambda b, i: (b * nt + i, 0, 0))),
        out_shape=(jax.ShapeDtypeStruct((B, S, D_MODEL), F32),
                   jax.ShapeDtypeStruct((B, S, D_MODEL), BF16),
                   jax.ShapeDtypeStruct((T // LANES, PEER_GROUPS * PEER_NKEYS, LANES), F32)),
        scratch_shapes=[pltpu.VMEM((POOL_HALO + tm, POOL_WIDTH), F32),
                        pltpu.VMEM((tm, D_MODEL), BF16),
                        pltpu.VMEM((N_PATTERNS - 1, KV_SLABS, tm, LANES), F32),
                        pltpu.VMEM((N_PATTERNS - 1, 1, tm, LANES), F32)],
        compiler_params=_params(("arbitrary", "arbitrary")),
        name="mix",
    )(x, a, a, *attn_outs, *attn_lses, expand, gates, w_pool, pool_scale, w_out, g2, wq_t, sub_keys)


_PAIRS = frozenset((a, b) for a in range(PEER_TOPK) for b in range(PEER_TOPK) if (a + 1) * (b + 1) <= PEER_TOPK)


def _oddeven_merge_sort_pairs(n):
    pairs = []

    def merge(lo, hi, r):
        step = r * 2
        if step < hi - lo:
            merge(lo, hi, step)
            merge(lo + r, hi, step)
            pairs.extend((i, i + r) for i in range(lo + r, hi - r, step))
        else:
            pairs.append((lo, lo + r))

    def sort(lo, hi):
        if hi - lo >= 1:
            mid = lo + (hi - lo) // 2
            sort(lo, mid)
            sort(mid + 1, hi)
            merge(lo, hi, 1)

    sort(0, n - 1)
    return tuple(pairs)


_SORT_PAIRS = _oddeven_merge_sort_pairs(PEER_TOPK)
LISTS_PER_HALF = PEER_NKEYS // PEER_TOPK


def _precedes(va, ia, vb, ib):
    if isinstance(ia, float) and isinstance(ib, float):
        return (va >= vb) if ia < ib else (va > vb)
    return (va > vb) | ((va == vb) & (ia < ib))


def _compare_exchange(a, b):
    (va, ia), (vb, ib) = a, b
    a_first = _precedes(va, ia, vb, ib)
    return ((jnp.maximum(va, vb), jnp.where(a_first, ia, ib)),
            (jnp.minimum(va, vb), jnp.where(a_first, ib, ia)))


def _sort_desc(items):
    items = list(items)
    for i, j in _SORT_PAIRS:
        items[i], items[j] = _compare_exchange(items[i], items[j])
    return items


def _bitonic_sort_desc(items):
    items = list(items)
    n = len(items)
    span = n // 2
    while span:
        for k in range(n):
            if not k & span:
                items[k], items[k + span] = _compare_exchange(items[k], items[k + span])
        span //= 2
    return items


def _merge_keep_top(a, b):
    n = len(a)
    out = []
    for k in range(n):
        if n - 1 - k >= len(b):
            out.append(a[k])
            continue
        (va, ia), (vb, ib) = a[k], b[n - 1 - k]
        out.append((jnp.maximum(va, vb), jnp.where(_precedes(va, ia, vb, ib), ia, ib)))
    return _bitonic_sort_desc(out)


def _topk_kernel(sc_ref, isel_ref, jsel_ref, g_ref, av_ref, ai_ref, bv_ref, bi_ref, ts_ref, ti_ref,
                 e_ref, s_ref):
    def sort_group(g, carry):
        base = pl.multiple_of(g * PEER_TOPK * PEER_HEADS, PEER_TOPK * PEER_HEADS)
        items = [(sc_ref[0, pl.ds(base + k * PEER_HEADS, PEER_HEADS), :], float(k)) for k in range(PEER_TOPK)]
        first_key = ((g % LISTS_PER_HALF) * PEER_TOPK).astype(F32)
        for r, (v, i) in enumerate(_sort_desc(items)):
            av_ref[g, r] = v
            ai_ref[g, r] = i + first_key
        return carry

    lax.fori_loop(0, 2 * LISTS_PER_HALF, sort_group, 0)

    def merge_level(src_v, src_i, dst_v, dst_i, n_out):
        def body(m, carry):
            lists = [[(src_v[2 * m + side, r], src_i[2 * m + side, r]) for r in range(PEER_TOPK)]
                     for side in range(2)]
            for r, (v, i) in enumerate(_merge_keep_top(*lists)):
                dst_v[m, r] = v
                dst_i[m, r] = i
            return carry

        lax.fori_loop(0, n_out, body, 0)

    merge_level(av_ref, ai_ref, bv_ref, bi_ref, LISTS_PER_HALF)
    merge_level(bv_ref, bi_ref, av_ref, ai_ref, LISTS_PER_HALF // 2)
    merge_level(av_ref, ai_ref, ts_ref, ti_ref, 2)

    def cand(a, b):
        expert = ti_ref[0, a] * float(PEER_NKEYS) + ti_ref[1, b]
        return ts_ref[0, a] + ts_ref[1, b], expert + float((a * PEER_TOPK + b) * PEER_N)

    by_row = {a: [cand(a, b) for b in range(PEER_TOPK) if (a, b) in _PAIRS] for a in range(PEER_TOPK)}
    single = [by_row[a][0] for a in range(8, PEER_TOPK)]
    second = _bitonic_sort_desc(by_row[1] + single[::-1])
    third = _sort_desc(by_row[2] + by_row[3] + by_row[4] + by_row[5] + by_row[6])
    third = _merge_keep_top(third, by_row[7])
    best = _merge_keep_top(_merge_keep_top(by_row[0], second), third)
    for r, (v, key) in enumerate(best):
        s_ref[r] = v
        e_ref[r] = key - jnp.floor(key * (1.0 / PEER_N)) * float(PEER_N)

    best = s_ref[...]
    ex = jnp.exp(best - best[0:1])
    gate = ex / jnp.sum(ex, axis=0, keepdims=True)
    e = e_ref[...].reshape(PEER_SEL, LANES)
    i_sel = jnp.floor(e * (1.0 / PEER_NKEYS))
    isel_ref[...] = i_sel.T
    jsel_ref[0] = e - i_sel * PEER_NKEYS
    g_ref[...] = gate.reshape(PEER_SEL, LANES).T


def _topk(sc_t):
    n_tiles = sc_t.shape[0]
    out = pl.BlockSpec((LANES, PEER_SEL), lambda i: (i, 0))
    shp = jax.ShapeDtypeStruct((n_tiles * LANES, PEER_SEL), F32)
    out_t = pl.BlockSpec((1, PEER_SEL, LANES), lambda i: (i, 0, 0))
    shp_t = jax.ShapeDtypeStruct((n_tiles, PEER_SEL, LANES), F32)
    vreg = (PEER_HEADS, LANES)
    return pl.pallas_call(
        _topk_kernel,
        grid=(n_tiles,),
        in_specs=[pl.BlockSpec((1, PEER_GROUPS * PEER_NKEYS, LANES), lambda i: (i, 0, 0))],
        out_specs=(out, out_t, out),
        out_shape=(shp, shp_t, shp),
        scratch_shapes=[pltpu.VMEM((2 * LISTS_PER_HALF, PEER_TOPK) + vreg, F32),
                        pltpu.VMEM((2 * LISTS_PER_HALF, PEER_TOPK) + vreg, F32),
                        pltpu.VMEM((LISTS_PER_HALF, PEER_TOPK) + vreg, F32),
                        pltpu.VMEM((LISTS_PER_HALF, PEER_TOPK) + vreg, F32),
                        pltpu.VMEM((2, PEER_TOPK) + vreg, F32),
                        pltpu.VMEM((2, PEER_TOPK) + vreg, F32),
                        pltpu.VMEM((PEER_TOPK,) + vreg, F32),
                        pltpu.VMEM((PEER_TOPK,) + vreg, F32)],
        compiler_params=_params(("arbitrary",)),
        name="topk",
    )(sc_t)


G_TILE = 256
G_PITCH = G_TILE + SUBLANES // 2


def _gbuild_kernel(isel_ref, jsel_ref, g_ref, out_ref, tile_ref):
    sub = lax.broadcasted_iota(jnp.int32, (PEER_NKEYS, PEER_SEL), 0).astype(F32)
    lane = lax.broadcasted_iota(jnp.int32, (PEER_SEL, PEER_NKEYS), 1).astype(F32).astype(BF16)
    one, zero = jnp.ones((), BF16), jnp.zeros((), BF16)
    for c in range(G_TILE):
        irow = isel_ref[c:c + 1, :]
        grow = g_ref[c:c + 1, :]
        slab, col = divmod(c, LANES)
        jcol = jsel_ref[slab, :, col:col + 1].astype(BF16)
        a = jnp.where(sub == irow, grow, 0.0).astype(BF16)
        b = jnp.where(lane == jcol, one, zero)
        tile_ref[pl.ds(c, PEER_NKEYS, stride=G_PITCH), :] = _dot(a, b)
    for i in range(PEER_NKEYS):
        out_ref[i] = tile_ref[i * G_PITCH:i * G_PITCH + G_TILE, :].astype(BF16)


def _gbuild(isel, jsel, gates):
    T = isel.shape[0]
    row = pl.BlockSpec((G_TILE, PEER_SEL), lambda t: (t, 0))
    return pl.pallas_call(
        _gbuild_kernel,
        grid=(T // G_TILE,),
        in_specs=[row, pl.BlockSpec((G_TILE // LANES, PEER_SEL, LANES), lambda t: (t, 0, 0)), row],
        out_specs=pl.BlockSpec((PEER_NKEYS, G_TILE, PEER_NKEYS), lambda t: (0, t, 0)),
        out_shape=jax.ShapeDtypeStruct((PEER_NKEYS, T, PEER_NKEYS), BF16),
        scratch_shapes=[pltpu.VMEM((PEER_NKEYS * G_PITCH, PEER_NKEYS), F32)],
        compiler_params=_params(("arbitrary",)),
        name="gbuild",
    )(isel, jsel, gates)


def _gelu_tanh(x):
    return 0.5 * x * (1.0 + jnp.tanh(math.sqrt(2.0 / math.pi) * (x + 0.044715 * (x * x * x))))


def _peer_kernel(hn_ref, u_ref, v_ref, g_ref, h_ref, out_ref, w_ref):
    e = pl.program_id(1)
    n_rows = g_ref.shape[0]

    @pl.when(e == 0)
    def _():
        out_ref[...] = h_ref[...]

    hn = hn_ref[...]
    for i in range(0, n_rows, 2):
        cols = slice(i * PEER_NKEYS, (i + 2) * PEER_NKEYS)
        act = _gelu_tanh(_dot(hn, u_ref[:, cols]))
        gate = jnp.concatenate([g_ref[i], g_ref[i + 1]], axis=1).astype(F32)
        w_ref[:, cols] = (act * gate).astype(BF16)
    out_ref[...] += _dot(w_ref[...], v_ref[...])


def _peer(hn2, h2, u_t, v, gmat, tt, eb):
    T = hn2.shape[0]
    n_rows = eb // PEER_NKEYS
    tok = lambda dt: pl.BlockSpec((tt, D_MODEL), lambda t, e: (t, 0))
    exp = pl.BlockSpec((eb, D_MODEL), lambda t, e: (e, 0))
    exp_t = pl.BlockSpec((D_MODEL, eb), lambda t, e: (0, e))
    return pl.pallas_call(
        _peer_kernel,
        grid=(T // tt, PEER_N // eb),
        in_specs=[tok(BF16), exp_t, exp,
                  pl.BlockSpec((n_rows, tt, PEER_NKEYS), lambda t, e: (e, t, 0)),
                  tok(F32)],
        out_specs=pl.BlockSpec((tt, D_MODEL), lambda t, e: (t, 0)),
        out_shape=jax.ShapeDtypeStruct((T, D_MODEL), F32),
        scratch_shapes=[pltpu.VMEM((tt, eb), BF16)],
        compiler_params=_params(("arbitrary", "arbitrary")),
        name="peer",
    )(hn2, u_t, v, gmat, h2)


def _layer(h, norm1_g, w_in, q_norm_g, k_norm_g, w_pool, pool_scale, w_out, norm2_g,
           w_query, sub_keys, expert_u, expert_v, biases):
    B, S, D = h.shape
    T = B * S
    tm, tt = _tile_sizes(S, T)
    assert D == D_MODEL and w_in.shape == (D_MODEL, IN_WIDTH), (h.shape, w_in.shape)
    assert S % (BAND * max(DILATIONS)) == 0, "every dilation pattern needs whole 128-key blocks"
    assert S % tm == 0 and tm % (2 * SUBLANES * max(DILATIONS)) == 0 and T % tt == 0 and tt % LANES == 0
    head = np.arange(NORM_COLS) // HEAD_DIM
    bd = jnp.asarray(head[:, None] == head[None, :], BF16)
    qg = (jnp.tile(q_norm_g, ATT_HEADS) * (HEAD_DIM ** -0.5)).reshape(1, KV_WIDTH)
    kg = jnp.tile(k_norm_g, ATT_HEADS).reshape(1, KV_WIDTH)

    res = _proj(h.reshape(T, D), norm1_g.reshape(1, D), w_in.astype(BF16), bd, qg, kg, tm)
    a, qs, ks, vs, gates = res[0], res[1:4], res[4:7], res[7:10], res[10]
    attn_outs, attn_lses = [], []
    for g, dilation in enumerate(DILATIONS):
        o, lse = _attn_pattern(qs[g], ks[g], vs[g], biases[g], dilation, B)
        attn_outs.append(o)
        attn_lses.append(lse)

    wq_t = w_query.T.astype(BF16)
    sk = sub_keys.reshape(PEER_GROUPS, PEER_NKEYS, PEER_HALF).astype(BF16)
    h_new, hn, sc_t = _mix(h, a.reshape(B, S, POOL_WIDTH), attn_outs, attn_lses,
                           jnp.asarray(_head_expand(), BF16), gates.reshape(B, S, 2 * D),
                           w_pool.astype(BF16), pool_scale.reshape(1, D), w_out.astype(BF16),
                           norm2_g.reshape(1, D), wq_t, sk, tm)
    isel, jsel, gsel = _topk(sc_t)
    gmat = _gbuild(isel, jsel, gsel)
    out = _peer(hn.reshape(T, D), h_new.reshape(T, D), expert_u.T.astype(BF16), expert_v.astype(BF16),
                gmat, tt, PEER_EXPERT_BLOCK)
    return out.reshape(B, S, D)


def kernel(x, norm1_g, w_in, q_norm_g, k_norm_g, w_pool, pool_scale, w_out, norm2_g, w_query, sub_keys,
           expert_u, expert_v):
    slopes = _alibi_slopes(N_PATTERNS * ATT_HEADS).reshape(N_PATTERNS, ATT_HEADS)
    biases = [jnp.asarray(_attn_bias(slopes[g], dil)) for g, (_, dil) in enumerate(DILATED_PATTERNS)]
    h = x
    for l in range(norm1_g.shape[0]):
        h = _layer(h, norm1_g[l], w_in[l], q_norm_g[l], k_norm_g[l], w_pool[l], pool_scale[l], w_out[l],
                   norm2_g[l], w_query[l], sub_keys[l], expert_u[l], expert_v[l], biases)
    return h
```

```python
import math

import numpy as np
import jax
import jax.numpy as jnp
from jax import lax
from jax.experimental import pallas as pl
from jax.experimental.pallas import tpu as pltpu

D_MODEL = 1024
EPS = 1e-6
NEG_INF = -1e30

POOL_WINDOWS = (2, 4, 8, 16)
POOL_GROUPS = 4
POOL_IN = 128
POOL_WIDTH = 512
POOL_OUT = 256
POOL_HALO = 16

ATT_HEADS = 16
HEAD_DIM = 64
DILATED_PATTERNS = ((128, 1), (512, 4), (2048, 16))
N_PATTERNS = 3
BAND = 128
KV_WIDTH = 1024
Q_OFF = POOL_WIDTH
K_OFF = Q_OFF + N_PATTERNS * KV_WIDTH
V_OFF = K_OFF + KV_WIDTH
GATE_OFF = V_OFF + KV_WIDTH
IN_WIDTH = GATE_OFF + 2 * D_MODEL

PEER_HEADS = 8
PEER_NKEYS = 128
PEER_N = PEER_NKEYS * PEER_NKEYS
PEER_HALF = 128
PEER_TOPK = 16
PEER_GROUPS = 2 * PEER_HEADS
PEER_SEL = PEER_HEADS * PEER_TOPK

LANES = 128
SUBLANES = 8
VMEM_LIMIT = 56 * 1024 * 1024

TOKEN_TILE = 512
PEER_TOKEN_TILE = 1024
PEER_EXPERT_BLOCK = 2048


def _tile_sizes(seq_len, n_tokens):
    return min(TOKEN_TILE, seq_len), min(PEER_TOKEN_TILE, n_tokens)

BF16 = jnp.bfloat16
F32 = jnp.float32


def _dot(a, b):
    return jnp.dot(a, b, preferred_element_type=F32)


def _dot_nt(a, b):
    return lax.dot_general(a, b, (((1,), (1,)), ((), ())), preferred_element_type=F32)


def _params(semantics):
    return pltpu.CompilerParams(dimension_semantics=semantics, vmem_limit_bytes=VMEM_LIMIT)


def _const_spec(shape):
    n = len(shape)
    return pl.BlockSpec(shape, lambda *_: (0,) * n)


KV_SLABS = KV_WIDTH // LANES
NORM_COLS = 256
DILATIONS = tuple(d for _, d in DILATED_PATTERNS)


def _store_dilated(y, slab_ref, out_refs):
    tm = y.shape[0]
    for d, ref in out_refs:
        if d == 1:
            ref[...] = y.astype(BF16)
    if all(d == 1 for d, _ in out_refs):
        return
    for s in range(KV_SLABS):
        slab_ref[s] = y[:, s * LANES:(s + 1) * LANES]
    for d, ref in out_refs:
        if d == 1:
            continue
        for r in range(d):
            for s in range(KV_SLABS):
                col = r * KV_WIDTH + s * LANES
                ref[:, col:col + LANES] = slab_ref[s, pl.ds(r, tm // d, stride=d), :].astype(BF16)


def _proj_kernel(x_ref, g1_ref, w_ref, bd_ref, qg_ref, kg_ref,
                 a_ref, q0_ref, q1_ref, q2_ref, k0_ref, k1_ref, k2_ref, v0_ref, v1_ref, v2_ref, gate_ref,
                 slab_ref):
    x = x_ref[...]
    inv = lax.rsqrt(jnp.mean(x * x, axis=-1, keepdims=True) + EPS)
    xn = (x * inv * g1_ref[...]).astype(BF16)

    def head_norm(y, gain):
        parts = []
        for c in range(0, KV_WIDTH, NORM_COLS):
            yc = y[:, c:c + NORM_COLS]
            ss = _dot((yc * yc).astype(BF16), bd_ref[...])
            parts.append(yc * lax.rsqrt(ss * (1.0 / HEAD_DIM) + EPS))
        return jnp.concatenate(parts, axis=1) * gain

    a_ref[...] = _dot(xn, w_ref[:, 0:POOL_WIDTH])
    for g, q_ref in enumerate((q0_ref, q1_ref, q2_ref)):
        lo = Q_OFF + g * KV_WIDTH
        q = head_norm(_dot(xn, w_ref[:, lo:lo + KV_WIDTH]), qg_ref[...])
        _store_dilated(q, slab_ref, [(DILATIONS[g], q_ref)])
    k = head_norm(_dot(xn, w_ref[:, K_OFF:K_OFF + KV_WIDTH]), kg_ref[...])
    _store_dilated(k, slab_ref, list(zip(DILATIONS, (k0_ref, k1_ref, k2_ref))))
    v = _dot(xn, w_ref[:, V_OFF:V_OFF + KV_WIDTH])
    _store_dilated(v, slab_ref, list(zip(DILATIONS, (v0_ref, v1_ref, v2_ref))))
    gate_ref[...] = jax.nn.sigmoid(_dot(xn, w_ref[:, GATE_OFF:IN_WIDTH])).astype(BF16)


def _proj(x2, g1, w_in, bd, qg, kg, tm):
    T = x2.shape[0]
    row = lambda w: pl.BlockSpec((tm, w), lambda i: (i, 0))
    dil = lambda d: pl.BlockSpec((tm // d, d * KV_WIDTH), lambda i: (i, 0))
    dil_shape = lambda d: jax.ShapeDtypeStruct((T // d, d * KV_WIDTH), BF16)
    once = lambda shape: pl.BlockSpec(shape, lambda i: (0, 0), pipeline_mode=pl.Buffered(1))
    views = [dil(d) for d in DILATIONS]
    view_shapes = [dil_shape(d) for d in DILATIONS]
    return pl.pallas_call(
        _proj_kernel,
        grid=(T // tm,),
        in_specs=[row(D_MODEL), once((1, D_MODEL)), once((D_MODEL, IN_WIDTH)),
                  once((NORM_COLS, NORM_COLS)), once((1, KV_WIDTH)), once((1, KV_WIDTH))],
        out_specs=(row(POOL_WIDTH), *views, *views, *views, row(2 * D_MODEL)),
        out_shape=(jax.ShapeDtypeStruct((T, POOL_WIDTH), F32), *view_shapes, *view_shapes, *view_shapes,
                   jax.ShapeDtypeStruct((T, 2 * D_MODEL), BF16)),
        scratch_shapes=[pltpu.VMEM((KV_SLABS, tm, LANES), F32)],
        compiler_params=_params(("arbitrary",)),
        name="proj",
    )(x2, g1, w_in, bd, qg, kg)


def _alibi_slopes(n):
    def geometric(k):
        start = 2.0 ** (-8.0 / k)
        return [start ** (i + 1) for i in range(k)]
    p = 2 ** int(math.floor(math.log2(n)))
    s = geometric(p) + geometric(2 * p)[0::2][: n - p]
    return np.sort(np.array(s, dtype=np.float32))[::-1].copy()


def _attn_bias(slopes, dilation):
    qi = np.arange(BAND)[:, None]
    ki = np.arange(2 * BAND)[None, :]
    step = qi + BAND - ki
    valid = (step >= 0) & (step <= BAND)
    alibi = -slopes[:, None, None] * (step * dilation).astype(np.float32)[None]
    general = np.where(valid[None], alibi, np.float32(NEG_INF)).astype(np.float32)
    first = np.where((ki >= BAND)[None], general, np.float32(NEG_INF)).astype(np.float32)
    return np.stack([general, first])


HEAD_PAIRS = ATT_HEADS // 2


def _attn_kernel(q_ref, kp_ref, kc_ref, vp_ref, vc_ref, bias_ref, o_ref, lse_ref, s_ref, m_ref, kt_ref):
    q_blocks = q_ref.shape[1] // BAND
    first_step = (pl.program_id(2) == 0).astype(jnp.int32)
    lane = lax.broadcasted_iota(jnp.int32, (BAND, LANES), 1)
    low = lane < HEAD_DIM
    low_b = (lax.broadcasted_iota(jnp.int32, (1, LANES), 1) < HEAD_DIM).astype(BF16)
    lane_sel = (low_b, 1 - low_b)

    for j in range(HEAD_PAIRS):
        cols = slice(j * LANES, (j + 1) * LANES)
        kt_ref[0, cols, :] = kp_ref[0, :, cols].T
        for blk in range(q_blocks):
            kt_ref[blk + 1, cols, :] = kc_ref[0, blk * BAND:(blk + 1) * BAND, cols].T

    for blk in range(q_blocks):
        rows = slice(blk * BAND, (blk + 1) * BAND)
        if blk == 0:
            v_prev, first = vp_ref, first_step
            before = slice(0, BAND)
        else:
            v_prev, first = vc_ref, 0
            before = slice((blk - 1) * BAND, blk * BAND)

        for h in range(ATT_HEADS):
            cols = slice((h // 2) * LANES, (h // 2 + 1) * LANES)
            qm = q_ref[0, rows, cols] * lane_sel[h % 2]
            sp = _dot(qm, kt_ref[blk, cols, :]) + bias_ref[first, h, :, 0:BAND]
            sc = _dot(qm, kt_ref[blk + 1, cols, :]) + bias_ref[first, h, :, BAND:2 * BAND]
            s_ref[blk, h, :, 0:BAND] = sp
            s_ref[blk, h, :, BAND:2 * BAND] = sc
            m_ref[blk, h] = jnp.broadcast_to(jnp.max(jnp.maximum(sp, sc), axis=-1, keepdims=True), (BAND, LANES))

        lse_tile = jnp.zeros((BAND, LANES), F32)
        for j in range(HEAD_PAIRS):
            cols = slice(j * LANES, (j + 1) * LANES)
            res = []
            for half in range(2):
                h = 2 * j + half
                own, other = lane_sel[half], lane_sel[1 - half]
                pp = jnp.exp(s_ref[blk, h, :, 0:BAND] - m_ref[blk, h]).astype(BF16)
                pc = jnp.exp(s_ref[blk, h, :, BAND:2 * BAND] - m_ref[blk, h]).astype(BF16)
                res.append(_dot(pp, v_prev[0, before, cols] * own + other)
                           + _dot(pc, vc_ref[0, rows, cols] * own + other))
            num = jnp.where(low, res[0], res[1])
            den = pltpu.roll(jnp.where(low, res[1], res[0]), HEAD_DIM, axis=1)
            o_ref[0, rows, cols] = (num / den).astype(BF16)
            lse = jnp.where(low, m_ref[blk, 2 * j], m_ref[blk, 2 * j + 1]) + jnp.log(den)
            lse_tile = jnp.where((lane == j) | (lane == HEAD_DIM + j), lse, lse_tile)
        lse_ref[0, rows, :] = lse_tile


def _attn_pattern(q, k, v, bias, dilation, B):
    d = dilation
    rows = q.shape[0] // B
    q_blocks = max(n for n in (1, 2, 4, 8) if rows % (n * BAND) == 0)
    qr = q_blocks * BAND
    view = lambda a: a.reshape(B, rows, d * KV_WIDTH)
    cur = lambda w: pl.BlockSpec((1, qr, w), lambda b, r, n: (b, n, r))
    prev = pl.BlockSpec((1, BAND, KV_WIDTH), lambda b, r, n: (b, jnp.maximum(q_blocks * n - 1, 0), r))
    o, lse = pl.pallas_call(
        _attn_kernel,
        grid=(B, d, rows // qr),
        in_specs=[cur(KV_WIDTH), prev, cur(KV_WIDTH), prev, cur(KV_WIDTH),
                  pl.BlockSpec(bias.shape, lambda b, r, n: (0, 0, 0, 0))],
        out_specs=(cur(KV_WIDTH), cur(LANES)),
        out_shape=(jax.ShapeDtypeStruct((B, rows, d * KV_WIDTH), BF16),
                   jax.ShapeDtypeStruct((B, rows, d * LANES), F32)),
        scratch_shapes=[pltpu.VMEM((q_blocks, ATT_HEADS, BAND, 2 * BAND), F32),
                        pltpu.VMEM((q_blocks, ATT_HEADS, BAND, LANES), F32),
                        pltpu.VMEM((q_blocks + 1, KV_WIDTH, BAND), BF16)],
        compiler_params=_params(("arbitrary", "arbitrary", "arbitrary")),
        name=f"attn_d{d}",
    )(view(q), view(k), view(k), view(v), view(v), bias)
    return o, lse


def _head_expand():
    e = np.zeros((LANES, KV_WIDTH), np.float32)
    for h in range(ATT_HEADS):
        e[(h % 2) * HEAD_DIM + h // 2, h * HEAD_DIM:(h + 1) * HEAD_DIM] = 1.0
    return np.concatenate([e, e], axis=0)


def _mix_kernel(x_ref, a_ref, ah_ref, o0_ref, o1_ref, o2_ref, l0_ref, l1_ref, l2_ref, ex_ref, gate_ref,
                wp_ref, ps_ref, wo_ref, g2_ref, wq_ref, sk_ref,
                h_ref, hn_ref, sc_ref, ext_ref, merged_ref, onat_ref, lnat_ref):
    tm = x_ref.shape[1]
    i = pl.program_id(1)

    def natural_rows(src_ref, d, slabs, dst_ref):
        n = src_ref.shape[1]
        for r in range(d):
            for s in range(slabs):
                col = (r * slabs + s) * LANES
                dst_ref[s, pl.ds(r, n, stride=d), :] = src_ref[0, :, col:col + LANES].astype(F32)

    for p, (o_ref, l_ref) in enumerate(((o1_ref, l1_ref), (o2_ref, l2_ref))):
        natural_rows(o_ref, DILATIONS[p + 1], KV_SLABS, onat_ref.at[p])
        natural_rows(l_ref, DILATIONS[p + 1], 1, lnat_ref.at[p])

    def attn_cols(p, g):
        if p == 0:
            return o0_ref[0, :, g * POOL_OUT:(g + 1) * POOL_OUT].astype(F32)
        return jnp.concatenate([onat_ref[p - 1, 2 * g], onat_ref[p - 1, 2 * g + 1]], axis=1)

    lses = [l0_ref[0], lnat_ref[0, 0], lnat_ref[1, 0]]
    top = jnp.maximum(jnp.maximum(lses[0], lses[1]), lses[2])
    es = [jnp.exp(l - top) for l in lses]
    tot = es[0] + es[1] + es[2]
    w_split = []
    for e in es:
        w = e / tot
        hi = w.astype(BF16)
        w_split.append(jnp.concatenate([hi, (w - hi.astype(F32)).astype(BF16)], axis=1))
    ext_ref[0:POOL_HALO, :] = jnp.where(i == 0, 0.0, ah_ref[0])
    ext_ref[POOL_HALO:POOL_HALO + tm, :] = a_ref[0]
    t = i * tm + lax.broadcasted_iota(jnp.int32, (tm, 1), 0)
    for g, w in enumerate(POOL_WINDOWS):
        cols = slice(g * POOL_IN, (g + 1) * POOL_IN)
        wsum = ext_ref[POOL_HALO:POOL_HALO + tm, cols]
        for j in range(1, w):
            wsum = wsum + ext_ref[POOL_HALO - j:POOL_HALO - j + tm, cols]
        count = jnp.minimum(t + 1, w).astype(F32)
        dlt = (wsum / count - a_ref[0, :, cols]).astype(BF16)
        ocol = slice(g * POOL_OUT, (g + 1) * POOL_OUT)
        pool = _dot(dlt, wp_ref[g]) * ps_ref[:, ocol]
        attn = jnp.zeros((tm, POOL_OUT), F32)
        for p, w2 in enumerate(w_split):
            attn = attn + _dot(w2, ex_ref[:, ocol]) * attn_cols(p, g)
        merged = (gate_ref[0, :, ocol].astype(F32) * pool
                  + gate_ref[0, :, D_MODEL + g * POOL_OUT:D_MODEL + (g + 1) * POOL_OUT].astype(F32) * attn)
        merged_ref[:, ocol] = merged.astype(BF16)
    h = x_ref[0] + _dot(merged_ref[...], wo_ref[...])
    h_ref[0] = h
    hn = (h * lax.rsqrt(jnp.mean(h * h, axis=-1, keepdims=True) + EPS) * g2_ref[...]).astype(BF16)
    hn_ref[0] = hn
    qt = _dot_nt(wq_ref[...], hn).astype(BF16)
    for g in range(PEER_GROUPS):
        head, half = divmod(g, 2)
        rows = slice(g * PEER_NKEYS, (g + 1) * PEER_NKEYS)
        sc = _dot(sk_ref[g], qt[rows, :])
        for j in range(tm // LANES):
            sc_ref[j, pl.ds(half * PEER_HEADS * PEER_NKEYS + head, PEER_NKEYS, stride=PEER_HEADS), :] = (
                sc[:, j * LANES:(j + 1) * LANES])


def _mix(x, a, attn_outs, attn_lses, expand, gates, w_pool, pool_scale, w_out, g2, wq_t, sub_keys, tm):
    B, S, _ = x.shape
    T = B * S
    nt = S // tm
    halo_per_tile = tm // POOL_HALO
    tile = lambda w: pl.BlockSpec((1, tm, w), lambda b, i: (b, i, 0))
    dil = lambda d, w: pl.BlockSpec((1, tm // d, d * w), lambda b, i: (b, i, 0))
    halo = pl.BlockSpec((1, POOL_HALO, POOL_WIDTH), lambda b, i: (b, jnp.maximum(i * halo_per_tile - 1, 0), 0))
    once = lambda shape: pl.BlockSpec(shape, lambda b, i: (0,) * len(shape), pipeline_mode=pl.Buffered(1))
    return pl.pallas_call(
        _mix_kernel,
        grid=(B, nt),
        in_specs=[tile(D_MODEL), tile(POOL_WIDTH), halo,
                  *[dil(d, KV_WIDTH) for d in DILATIONS], *[dil(d, LANES) for d in DILATIONS],
                  once(expand.shape), tile(2 * D_MODEL),
                  once(w_pool.shape), once((1, D_MODEL)), once((D_MODEL, D_MODEL)), once((1, D_MODEL)),
                  once(wq_t.shape), once(sub_keys.shape)],
        out_specs=(tile(D_MODEL), tile(D_MODEL),
                   pl.BlockSpec((tm // LANES, PEER_GROUPS * PEER_NKEYS, LANES), lambda b, i: (b * nt + i, 0, 0))),
        out_shape=(jax.ShapeDtypeStruct((B, S, D_MODEL), F32),
                   jax.ShapeDtypeStruct((B, S, D_MODEL), BF16),
                   jax.ShapeDtypeStruct((T // LANES, PEER_GROUPS * PEER_NKEYS, LANES), F32)),
        scratch_shapes=[pltpu.VMEM((POOL_HALO + tm, POOL_WIDTH), F32),
                        pltpu.VMEM((tm, D_MODEL), BF16),
                        pltpu.VMEM((N_PATTERNS - 1, KV_SLABS, tm, LANES), F32),
                        pltpu.VMEM((N_PATTERNS - 1, 1, tm, LANES), F32)],
        compiler_params=_params(("arbitrary", "arbitrary")),
        name="mix",
    )(x, a, a, *attn_outs, *attn_lses, expand, gates, w_pool, pool_scale, w_out, g2, wq_t, sub_keys)


_PAIRS = frozenset((a, b) for a in range(PEER_TOPK) for b in range(PEER_TOPK) if (a + 1) * (b + 1) <= PEER_TOPK)


def _oddeven_merge_sort_pairs(n):
    pairs = []

    def merge(lo, hi, r):
        step = r * 2
        if step < hi - lo:
            merge(lo, hi, step)
            merge(lo + r, hi, step)
            pairs.extend((i, i + r) for i in range(lo + r, hi - r, step))
        else:
            pairs.append((lo, lo + r))

    def sort(lo, hi):
        if hi - lo >= 1:
            mid = lo + (hi - lo) // 2
            sort(lo, mid)
            sort(mid + 1, hi)
            merge(lo, hi, 1)

    sort(0, n - 1)
    return tuple(pairs)


_SORT_PAIRS = _oddeven_merge_sort_pairs(PEER_TOPK)
LISTS_PER_HALF = PEER_NKEYS // PEER_TOPK


def _precedes(va, ia, vb, ib):
    if isinstance(ia, float) and isinstance(ib, float):
        return (va >= vb) if ia < ib else (va > vb)
    return (va > vb) | ((va == vb) & (ia < ib))


def _compare_exchange(a, b):
    (va, ia), (vb, ib) = a, b
    a_first = _precedes(va, ia, vb, ib)
    return ((jnp.maximum(va, vb), jnp.where(a_first, ia, ib)),
            (jnp.minimum(va, vb), jnp.where(a_first, ib, ia)))


def _sort_desc(items):
    items = list(items)
    for i, j in _SORT_PAIRS:
        items[i], items[j] = _compare_exchange(items[i], items[j])
    return items


def _bitonic_sort_desc(items):
    items = list(items)
    n = len(items)
    span = n // 2
    while span:
        for k in range(n):
            if not k & span:
                items[k], items[k + span] = _compare_exchange(items[k], items[k + span])
        span //= 2
    return items


def _merge_keep_top(a, b):
    n = len(a)
    out = []
    for k in range(n):
        if n - 1 - k >= len(b):
            out.append(a[k])
            continue
        (va, ia), (vb, ib) = a[k], b[n - 1 - k]
        out.append((jnp.maximum(va, vb), jnp.where(_precedes(va, ia, vb, ib), ia, ib)))
    return _bitonic_sort_desc(out)


def _topk_kernel(sc_ref, isel_ref, jsel_ref, g_ref, av_ref, ai_ref, bv_ref, bi_ref, ts_ref, ti_ref,
                 e_ref, s_ref):
    def sort_group(g, carry):
        base = pl.multiple_of(g * PEER_TOPK * PEER_HEADS, PEER_TOPK * PEER_HEADS)
        items = [(sc_ref[0, pl.ds(base + k * PEER_HEADS, PEER_HEADS), :], float(k)) for k in range(PEER_TOPK)]
        first_key = ((g % LISTS_PER_HALF) * PEER_TOPK).astype(F32)
        for r, (v, i) in enumerate(_sort_desc(items)):
            av_ref[g, r] = v
            ai_ref[g, r] = i + first_key
        return carry

    lax.fori_loop(0, 2 * LISTS_PER_HALF, sort_group, 0)

    def merge_level(src_v, src_i, dst_v, dst_i, n_out):
        def body(m, carry):
            lists = [[(src_v[2 * m + side, r], src_i[2 * m + side, r]) for r in range(PEER_TOPK)]
                     for side in range(2)]
            for r, (v, i) in enumerate(_merge_keep_top(*lists)):
                dst_v[m, r] = v
                dst_i[m, r] = i
            return carry

        lax.fori_loop(0, n_out, body, 0)

    merge_level(av_ref, ai_ref, bv_ref, bi_ref, LISTS_PER_HALF)
    merge_level(bv_ref, bi_ref, av_ref, ai_ref, LISTS_PER_HALF // 2)
    merge_level(av_ref, ai_ref, ts_ref, ti_ref, 2)

    def cand(a, b):
        expert = ti_ref[0, a] * float(PEER_NKEYS) + ti_ref[1, b]
        return ts_ref[0, a] + ts_ref[1, b], expert + float((a * PEER_TOPK + b) * PEER_N)

    by_row = {a: [cand(a, b) for b in range(PEER_TOPK) if (a, b) in _PAIRS] for a in range(PEER_TOPK)}
    single = [by_row[a][0] for a in range(8, PEER_TOPK)]
    second = _bitonic_sort_desc(by_row[1] + single[::-1])
    third = _sort_desc(by_row[2] + by_row[3] + by_row[4] + by_row[5] + by_row[6])
    third = _merge_keep_top(third, by_row[7])
    best = _merge_keep_top(_merge_keep_top(by_row[0], second), third)
    for r, (v, key) in enumerate(best):
        s_ref[r] = v
        e_ref[r] = key - jnp.floor(key * (1.0 / PEER_N)) * float(PEER_N)

    best = s_ref[...]
    ex = jnp.exp(best - best[0:1])
    gate = ex / jnp.sum(ex, axis=0, keepdims=True)
    e = e_ref[...].reshape(PEER_SEL, LANES)
    i_sel = jnp.floor(e * (1.0 / PEER_NKEYS))
    isel_ref[...] = i_sel.T
    jsel_ref[0] = e - i_sel * PEER_NKEYS
    g_ref[...] = gate.reshape(PEER_SEL, LANES).T


def _topk(sc_t):
    n_tiles = sc_t.shape[0]
    out = pl.BlockSpec((LANES, PEER_SEL), lambda i: (i, 0))
    shp = jax.ShapeDtypeStruct((n_tiles * LANES, PEER_SEL), F32)
    out_t = pl.BlockSpec((1, PEER_SEL, LANES), lambda i: (i, 0, 0))
    shp_t = jax.ShapeDtypeStruct((n_tiles, PEER_SEL, LANES), F32)
    vreg = (PEER_HEADS, LANES)
    return pl.pallas_call(
        _topk_kernel,
        grid=(n_tiles,),
        in_specs=[pl.BlockSpec((1, PEER_GROUPS * PEER_NKEYS, LANES), lambda i: (i, 0, 0))],
        out_specs=(out, out_t, out),
        out_shape=(shp, shp_t, shp),
        scratch_shapes=[pltpu.VMEM((2 * LISTS_PER_HALF, PEER_TOPK) + vreg, F32),
                        pltpu.VMEM((2 * LISTS_PER_HALF, PEER_TOPK) + vreg, F32),
                        pltpu.VMEM((LISTS_PER_HALF, PEER_TOPK) + vreg, F32),
                        pltpu.VMEM((LISTS_PER_HALF, PEER_TOPK) + vreg, F32),
                        pltpu.VMEM((2, PEER_TOPK) + vreg, F32),
                        pltpu.VMEM((2, PEER_TOPK) + vreg, F32),
                        pltpu.VMEM((PEER_TOPK,) + vreg, F32),
                        pltpu.VMEM((PEER_TOPK,) + vreg, F32)],
        compiler_params=_params(("arbitrary",)),
        name="topk",
    )(sc_t)


G_TILE = 256
G_PITCH = G_TILE + SUBLANES // 2


def _gbuild_kernel(isel_ref, jsel_ref, g_ref, out_ref, tile_ref):
    sub = lax.broadcasted_iota(jnp.int32, (PEER_NKEYS, PEER_SEL), 0).astype(F32)
    lane = lax.broadcasted_iota(jnp.int32, (PEER_SEL, PEER_NKEYS), 1).astype(F32).astype(BF16)
    one, zero = jnp.ones((), BF16), jnp.zeros((), BF16)
    for c in range(G_TILE):
        irow = isel_ref[c:c + 1, :]
        grow = g_ref[c:c + 1, :]
        slab, col = divmod(c, LANES)
        jcol = jsel_ref[slab, :, col:col + 1].astype(BF16)
        a = jnp.where(sub == irow, grow, 0.0).astype(BF16)
        b = jnp.where(lane == jcol, one, zero)
        tile_ref[pl.ds(c, PEER_NKEYS, stride=G_PITCH), :] = _dot(a, b)
    for i in range(PEER_NKEYS):
        out_ref[i] = tile_ref[i * G_PITCH:i * G_PITCH + G_TILE, :].astype(BF16)


def _gbuild(isel, jsel, gates):
    T = isel.shape[0]
    row = pl.BlockSpec((G_TILE, PEER_SEL), lambda t: (t, 0))
    return pl.pallas_call(
        _gbuild_kernel,
        grid=(T // G_TILE,),
        in_specs=[row, pl.BlockSpec((G_TILE // LANES, PEER_SEL, LANES), lambda t: (t, 0, 0)), row],
        out_specs=pl.BlockSpec((PEER_NKEYS, G_TILE, PEER_NKEYS), lambda t: (0, t, 0)),
        out_shape=jax.ShapeDtypeStruct((PEER_NKEYS, T, PEER_NKEYS), BF16),
        scratch_shapes=[pltpu.VMEM((PEER_NKEYS * G_PITCH, PEER_NKEYS), F32)],
        compiler_params=_params(("arbitrary",)),
        name="gbuild",
    )(isel, jsel, gates)


def _gelu_tanh(x):
    return 0.5 * x * (1.0 + jnp.tanh(math.sqrt(2.0 / math.pi) * (x + 0.044715 * (x * x * x))))


def _peer_kernel(hn_ref, u_ref, v_ref, g_ref, h_ref, out_ref, w_ref):
    e = pl.program_id(1)
    n_rows = g_ref.shape[0]

    @pl.when(e == 0)
    def _():
        out_ref[...] = h_ref[...]

    hn = hn_ref[...]
    for i in range(0, n_rows, 2):
        cols = slice(i * PEER_NKEYS, (i + 2) * PEER_NKEYS)
        act = _gelu_tanh(_dot(hn, u_ref[:, cols]))
        gate = jnp.concatenate([g_ref[i], g_ref[i + 1]], axis=1).astype(F32)
        w_ref[:, cols] = (act * gate).astype(BF16)
    out_ref[...] += _dot(w_ref[...], v_ref[...])


def _peer(hn2, h2, u_t, v, gmat, tt, eb):
    T = hn2.shape[0]
    n_rows = eb // PEER_NKEYS
    tok = lambda dt: pl.BlockSpec((tt, D_MODEL), lambda t, e: (t, 0))
    exp = pl.BlockSpec((eb, D_MODEL), lambda t, e: (e, 0))
    exp_t = pl.BlockSpec((D_MODEL, eb), lambda t, e: (0, e))
    return pl.pallas_call(
        _peer_kernel,
        grid=(T // tt, PEER_N // eb),
        in_specs=[tok(BF16), exp_t, exp,
                  pl.BlockSpec((n_rows, tt, PEER_NKEYS), lambda t, e: (e, t, 0)),
                  tok(F32)],
        out_specs=pl.BlockSpec((tt, D_MODEL), lambda t, e: (t, 0)),
        out_shape=jax.ShapeDtypeStruct((T, D_MODEL), F32),
        scratch_shapes=[pltpu.VMEM((tt, eb), BF16)],
        compiler_params=_params(("arbitrary", "arbitrary")),
        name="peer",
    )(hn2, u_t, v, gmat, h2)


def _layer(h, norm1_g, w_in, q_norm_g, k_norm_g, w_pool, pool_scale, w_out, norm2_g,
           w_query, sub_keys, expert_u, expert_v, biases):
    B, S, D = h.shape
    T = B * S
    tm, tt = _tile_sizes(S, T)
    assert D == D_MODEL and w_in.shape == (D_MODEL, IN_WIDTH), (h.shape, w_in.shape)
    assert S % (BAND * max(DILATIONS)) == 0, "every dilation pattern needs whole 128-key blocks"
    assert S % tm == 0 and tm % (2 * SUBLANES * max(DILATIONS)) == 0 and T % tt == 0 and tt % LANES == 0
    head = np.arange(NORM_COLS) // HEAD_DIM
    bd = jnp.asarray(head[:, None] == head[None, :], BF16)
    qg = (jnp.tile(q_norm_g, ATT_HEADS) * (HEAD_DIM ** -0.5)).reshape(1, KV_WIDTH)
    kg = jnp.tile(k_norm_g, ATT_HEADS).reshape(1, KV_WIDTH)

    res = _proj(h.reshape(T, D), norm1_g.reshape(1, D), w_in.astype(BF16), bd, qg, kg, tm)
    a, qs, ks, vs, gates = res[0], res[1:4], res[4:7], res[7:10], res[10]
    attn_outs, attn_lses = [], []
    for g, dilation in enumerate(DILATIONS):
        o, lse = _attn_pattern(qs[g], ks[g], vs[g], biases[g], dilation, B)
        attn_outs.append(o)
        attn_lses.append(lse)

    wq_t = w_query.T.astype(BF16)
    sk = sub_keys.reshape(PEER_GROUPS, PEER_NKEYS, PEER_HALF).astype(BF16)
    h_new, hn, sc_t = _mix(h, a.reshape(B, S, POOL_WIDTH), attn_outs, attn_lses,
                           jnp.asarray(_head_expand(), BF16), gates.reshape(B, S, 2 * D),
                           w_pool.astype(BF16), pool_scale.reshape(1, D), w_out.astype(BF16),
                           norm2_g.reshape(1, D), wq_t, sk, tm)
    isel, jsel, gsel = _topk(sc_t)
    gmat = _gbuild(isel, jsel, gsel)
    out = _peer(hn.reshape(T, D), h_new.reshape(T, D), expert_u.T.astype(BF16), expert_v.astype(BF16),
                gmat, tt, PEER_EXPERT_BLOCK)
    return out.reshape(B, S, D)


def kernel(x, norm1_g, w_in, q_norm_g, k_norm_g, w_pool, pool_scale, w_out, norm2_g, w_query, sub_keys,
           expert_u, expert_v):
    slopes = _alibi_slopes(N_PATTERNS * ATT_HEADS).reshape(N_PATTERNS, ATT_HEADS)
    biases = [jnp.asarray(_attn_bias(slopes[g], dil)) for g, (_, dil) in enumerate(DILATED_PATTERNS)]
    h = x
    for l in range(norm1_g.shape[0]):
        h = _layer(h, norm1_g[l], w_in[l], q_norm_g[l], k_norm_g[l], w_pool[l], pool_scale[l], w_out[l],
                   norm2_g[l], w_query[l], sub_keys[l], expert_u[l], expert_v[l], biases)
    return h
```
